```python
import jax, jax.numpy as jnp
from jax import lax
import numpy as np

D_MODEL = 1024
BATCH = 8
SEQ = 4096
DEPTH = 4

GRID_W = 64
CTX_LEN = 256
MIX_WIDTH = D_MODEL
RET_WIDTH = D_MODEL // 2
POOL_WIDTH = MIX_WIDTH - RET_WIDTH
RET_HEADS = 4
RET_HEAD_DIM = RET_WIDTH // RET_HEADS
RET_CHUNK = 128
ROPE_BASE = 10000.0
POOL_WINDOWS = (2, 4, 8, 16)
POOL_GROUPS = len(POOL_WINDOWS)
POOL_CH = POOL_WIDTH // POOL_GROUPS
IN_COLS = 4 * RET_WIDTH + POOL_WIDTH
D_FF = 2816
N_EXPERTS = 8
MOE_TOP_K = 2
MOE_D_FF = 2816
NORM_EPS = 1e-6
GN_EPS = 1e-5

kernel_name = 'hybrid_retention_pool_moe_dit'

F32 = jnp.float32


def rmsnorm(x, gain):
    xf = x.astype(F32)
    y = xf * lax.rsqrt(jnp.mean(xf * xf, axis=-1, keepdims=True) + NORM_EPS)
    return (y * gain.astype(F32)).astype(x.dtype)


def modulate(h, shift, scale):
    return h * (1 + scale) + shift


def rope_2d(t, rows, cols):
    half = t.shape[-1] // 2
    inv_freq = ROPE_BASE ** (-jnp.arange(0, half, 2, dtype=F32) / half)

    def rotate(u, pos):
        ang = pos.astype(F32)[:, None] * inv_freq[None, :]
        cos = jnp.cos(ang)[None, :, None, :]
        sin = jnp.sin(ang)[None, :, None, :]
        u1, u2 = jnp.split(u, 2, axis=-1)
        return jnp.concatenate([u1 * cos - u2 * sin, u1 * sin + u2 * cos], axis=-1)

    return jnp.concatenate([rotate(t[..., :half], rows), rotate(t[..., half:], cols)], axis=-1)


def retention_chunked(q, k, v, log_gamma, s0, include_diag):
    b, n, h, dk = q.shape
    dv = v.shape[-1]
    nc = n // RET_CHUNK
    q = q.reshape(b, nc, RET_CHUNK, h, dk)
    k = k.reshape(b, nc, RET_CHUNK, h, dk)
    v = v.reshape(b, nc, RET_CHUNK, h, dv)
    pos = jnp.arange(RET_CHUNK, dtype=F32)
    diff = pos[:, None] - pos[None, :]
    past_mask = diff >= 0 if include_diag else diff > 0
    decay = jnp.where(past_mask[None], jnp.exp(log_gamma[:, None, None] * jnp.maximum(diff, 0.0)[None]), 0.0)
    scores = jnp.einsum('bcihd,bcjhd->bchij', q, k) * decay[None, None]
    intra = jnp.einsum('bchij,bcjhe->bcihe', scores, v)
    k_tail = k * jnp.exp((RET_CHUNK - 1 - pos)[:, None] * log_gamma[None, :])[:, :, None]
    kv = jnp.einsum('bcjhd,bcjhe->cbhde', k_tail, v)
    gamma_chunk = jnp.exp(log_gamma * RET_CHUNK)[None, :, None, None]

    def step(state, kv_c):
        return gamma_chunk * state + kv_c, state

    s_final, s_prev = lax.scan(step, s0, kv)
    q_head = q * jnp.exp((pos + 1.0)[:, None] * log_gamma[None, :])[:, :, None]
    inter = jnp.einsum('bcihd,cbhde->bcihe', q_head, s_prev)
    return (intra + inter).reshape(b, n, h, dv), s_final


def retention_final_state(k, v, log_gamma, reverse):
    n = k.shape[1]
    pos = jnp.arange(n, dtype=F32)
    dist = pos if reverse else (n - 1 - pos)
    wgt = jnp.exp(dist[:, None] * log_gamma[None, :])
    return jnp.einsum('bnhd,bnhe->bhde', k * wgt[None, :, :, None], v)


def retention_readout(o, g, gain):
    mu = jnp.mean(o, axis=-1, keepdims=True)
    var = jnp.mean(jnp.square(o - mu), axis=-1, keepdims=True)
    on = ((o - mu) * lax.rsqrt(var + GN_EPS)).reshape(o.shape[0], o.shape[1], RET_WIDTH)
    return (on * gain.astype(F32) * jax.nn.silu(g.astype(F32))).astype(g.dtype)


def centred_mean(u, w, axis):
    n = u.shape[axis]
    cs = jnp.cumsum(u.astype(F32), axis=axis)
    pad = [(0, 0)] * u.ndim
    pad[axis] = (1, 0)
    cs = jnp.pad(cs, pad)
    i = jnp.arange(n)
    lo = jnp.clip(i - w // 2, 0, n)
    hi = jnp.clip(i - w // 2 + w, 0, n)
    total = jnp.take(cs, hi, axis=axis) - jnp.take(cs, lo, axis=axis)
    shape = [1] * u.ndim
    shape[axis] = n
    return (total / (hi - lo).astype(F32).reshape(shape)).astype(u.dtype)


def pool_grid_diff(p, rows_n):
    b, l, _ = p.shape
    pg = p.reshape(b, rows_n, GRID_W, POOL_GROUPS, POOL_CH)
    outs = [centred_mean(centred_mean(pg[..., gi, :], w, 1), w, 2) - pg[..., gi, :]
            for gi, w in enumerate(POOL_WINDOWS)]
    return jnp.stack(outs, axis=3).reshape(b, l, POOL_GROUPS, POOL_CH)


def pool_seq_diff(p):
    b, n, _ = p.shape
    pg = p.reshape(b, n, POOL_GROUPS, POOL_CH)
    outs = [centred_mean(pg[:, :, gi, :], w, 1) - pg[:, :, gi, :] for gi, w in enumerate(POOL_WINDOWS)]
    return jnp.stack(outs, axis=2)


def pool_project(d, pool_w, pool_scale):
    b, n = d.shape[0], d.shape[1]
    return jnp.einsum('bngc,gcd->bngd', d, pool_w).reshape(b, n, POOL_WIDTH) * pool_scale


def token_mix(h, hc, w_in, log_gamma, gn_g, pool_w, pool_scale, w_out, with_ctx_out):
    b, l, _ = h.shape
    rows_n = l // GRID_W
    t = jnp.arange(l)
    rows, cols = t // GRID_W, t % GRID_W
    splits = [RET_WIDTH, 2 * RET_WIDTH, 3 * RET_WIDTH, 4 * RET_WIDTH]
    scale = RET_HEAD_DIM ** -0.5

    def heads(u):
        return u.reshape(u.shape[0], u.shape[1], RET_HEADS, RET_HEAD_DIM).astype(F32)

    lg_f, lg_b = log_gamma[0], log_gamma[1]
    zeros = jnp.zeros((b, RET_HEADS, RET_HEAD_DIM, RET_HEAD_DIM), F32)

    if with_ctx_out:
        qc, kc, vc, gc, pc = jnp.split(hc @ w_in, splits, axis=-1)
        qc, kc, vc = heads(qc), heads(kc) * scale, heads(vc)
        oc_f, s_f = retention_chunked(qc, kc, vc, lg_f, zeros, True)
        oc_b, s_b = retention_chunked(qc[:, ::-1], kc[:, ::-1], vc[:, ::-1], lg_b, zeros, False)
        ret_c = retention_readout(oc_f + oc_b[:, ::-1], gc, gn_g)
        pool_c = pool_project(pool_seq_diff(pc), pool_w, pool_scale)
        y_ctx = (jnp.concatenate([ret_c, pool_c.astype(ret_c.dtype)], axis=-1) @ w_out).astype(hc.dtype)
    else:
        kc, vc = jnp.split(hc @ w_in[:, RET_WIDTH:3 * RET_WIDTH], 2, axis=-1)
        kc, vc = heads(kc) * scale, heads(vc)
        s_f = retention_final_state(kc, vc, lg_f, reverse=False)
        s_b = retention_final_state(kc, vc, lg_b, reverse=True)
        y_ctx = None

    q, k, v, g, p = jnp.split(h @ w_in, splits, axis=-1)
    q = rope_2d(heads(q), rows, cols)
    k = rope_2d(heads(k), rows, cols) * scale
    v = heads(v)
    o_f, _ = retention_chunked(q, k, v, lg_f, s_f, True)
    o_b, _ = retention_chunked(q[:, ::-1], k[:, ::-1], v[:, ::-1], lg_b, s_b, False)
    ret = retention_readout(o_f + o_b[:, ::-1], g, gn_g)
    pool = pool_project(pool_grid_diff(p, rows_n), pool_w, pool_scale)
    y = (jnp.concatenate([ret, pool.astype(ret.dtype)], axis=-1) @ w_out).astype(h.dtype)
    return y, y_ctx


def swiglu(h, w13, w2):
    u, gt = jnp.split(h @ w13, 2, axis=-1)
    return (jax.nn.silu(gt) * u) @ w2


def moe_swiglu(h, router_w, w13, w2):
    shp = h.shape
    t = h.reshape(-1, shp[-1])
    logits = (t @ router_w).astype(F32)
    top_val, top_idx = lax.top_k(logits, MOE_TOP_K)
    top_w = jax.nn.softmax(top_val, axis=-1)
    gate = jnp.sum(jax.nn.one_hot(top_idx, N_EXPERTS, dtype=F32) * top_w[..., None], axis=1)
    out = jnp.zeros(t.shape, F32)
    for e in range(N_EXPERTS):
        out = out + gate[:, e:e + 1] * swiglu(t, w13[e], w2[e]).astype(F32)
    return out.reshape(shp).astype(h.dtype)


def setup_inputs(seed: int = 0) -> dict:
    key = jax.random.key(seed)
    ks = jax.random.split(key, 20)
    n_dense = (DEPTH + 1) // 2
    n_moe = DEPTH // 2

    def nrm(k, shape, s=1.0):
        return jax.random.normal(k, shape, F32) * s

    base_logit = jnp.log(2.0 ** (5.0 + jnp.arange(RET_HEADS, dtype=F32)) - 1.0)
    return {
        'x': nrm(ks[0], (BATCH, SEQ, D_MODEL)),
        'c': nrm(ks[1], (BATCH, D_MODEL)),
        'ctx': nrm(ks[2], (BATCH, CTX_LEN, D_MODEL)),
        'c_ctx': nrm(ks[3], (D_MODEL,)),
        'w_ada': nrm(ks[4], (DEPTH, D_MODEL, 6 * D_MODEL), 0.5 * D_MODEL ** -0.5),
        'b_ada': nrm(ks[5], (DEPTH, 6 * D_MODEL), 0.02),
        'norm1_g': 1.0 + nrm(ks[6], (DEPTH, D_MODEL), 0.02),
        'norm2_g': 1.0 + nrm(ks[7], (DEPTH, D_MODEL), 0.02),
        'w_in': nrm(ks[8], (DEPTH, D_MODEL, IN_COLS), D_MODEL ** -0.5),
        'ret_decay_logit': base_logit[None, None, :] + nrm(ks[9], (DEPTH, 2, RET_HEADS), 0.01),
        'ret_gn_g': 1.0 + nrm(ks[10], (DEPTH, RET_WIDTH), 0.02),
        'pool_w': nrm(ks[11], (DEPTH, POOL_GROUPS, POOL_CH, POOL_CH), POOL_CH ** -0.5),
        'pool_scale': 1.0 + nrm(ks[12], (DEPTH, POOL_WIDTH), 0.02),
        'w_out': nrm(ks[13], (DEPTH, MIX_WIDTH, D_MODEL), MIX_WIDTH ** -0.5),
        'ffn_w13': nrm(ks[14], (n_dense, D_MODEL, 2 * D_FF), D_MODEL ** -0.5),
        'ffn_w2': nrm(ks[15], (n_dense, D_FF, D_MODEL), D_FF ** -0.5),
        'router_w': nrm(ks[16], (n_moe, D_MODEL, N_EXPERTS), D_MODEL ** -0.5),
        'moe_w13': nrm(ks[17], (n_moe, N_EXPERTS, D_MODEL, 2 * MOE_D_FF), D_MODEL ** -0.5),
        'moe_w2': nrm(ks[18], (n_moe, N_EXPERTS, MOE_D_FF, D_MODEL), MOE_D_FF ** -0.5),
        'final_norm_g': 1.0 + nrm(ks[19], (D_MODEL,), 0.02),
    }


def reference(x, c, ctx, c_ctx, w_ada, b_ada, norm1_g, norm2_g, w_in, ret_decay_logit, ret_gn_g,
              pool_w, pool_scale, w_out, ffn_w13, ffn_w2, router_w, moe_w13, moe_w2, final_norm_g):
    xc = ctx
    for l in range(DEPTH):
        last = l == DEPTH - 1
        mod_lat = (jax.nn.silu(c) @ w_ada[l] + b_ada[l])[:, None, :]
        sh1, sc1, gt1, sh2, sc2, gt2 = jnp.split(mod_lat, 6, axis=-1)
        mod_ctx = jax.nn.silu(c_ctx) @ w_ada[l] + b_ada[l]
        csh1, csc1, cgt1, csh2, csc2, cgt2 = jnp.split(mod_ctx, 6)
        log_gamma = jax.nn.log_sigmoid(ret_decay_logit[l].astype(F32))

        h = modulate(rmsnorm(x, norm1_g[l]), sh1, sc1)
        hc = modulate(rmsnorm(xc, norm1_g[l]), csh1, csc1)
        y, y_ctx = token_mix(h, hc, w_in[l], log_gamma, ret_gn_g[l], pool_w[l], pool_scale[l],
                             w_out[l], not last)
        x = x + gt1 * y

        if l % 2 == 0:
            def ffn(z, i=l // 2):
                return swiglu(z, ffn_w13[i], ffn_w2[i])
        else:
            def ffn(z, i=l // 2):
                return moe_swiglu(z, router_w[i], moe_w13[i], moe_w2[i])
        x = x + gt2 * ffn(modulate(rmsnorm(x, norm2_g[l]), sh2, sc2))

        if not last:
            xc = xc + cgt1 * y_ctx
            xc = xc + cgt2 * ffn(modulate(rmsnorm(xc, norm2_g[l]), csh2, csc2))
    return rmsnorm(x, final_norm_g)
```

```python
import functools

import jax
import jax.numpy as jnp
from jax import lax
from jax.experimental import pallas as pl
from jax.experimental.pallas import tpu as pltpu

F32 = jnp.float32
BF16 = jnp.bfloat16

D_MODEL = 1024
BATCH = 8
SEQ = 4096
DEPTH = 4
GRID_W = 64
GRID_SHIFT = 6
CTX_LEN = 256
RET_WIDTH = 512
POOL_WIDTH = 512
RET_HEADS = 4
HEAD_DIM = 128
CHUNK = 128
ROPE_BASE = 10000.0
POOL_WINDOWS = (2, 4, 8, 16)
POOL_CH = 128
IN_COLS = 4 * RET_WIDTH + POOL_WIDTH
D_FF = 2816
N_EXPERTS = 8
NORM_EPS = 1e-6
GN_EPS = 1e-5

N_LAT = BATCH * SEQ
N_CTX = BATCH * CTX_LEN
N_ROWS = N_LAT + N_CTX
MOD_ROWS = 16
CTX_MOD_ROW = BATCH

TM = 512
LAT_TILES = N_LAT // TM
CTX_TILES = N_CTX // TM
TILES_PER_BATCH = SEQ // TM
FF_CHUNK = 256
VMEM_LIMIT = 56 * 1024 * 1024


def _mod_row(i):
    return jnp.where(i < LAT_TILES, i // TILES_PER_BATCH, CTX_MOD_ROW)


def _mod_spec(layer, part):
    return pl.BlockSpec((None, None, None, 1, D_MODEL),
                        lambda i, *_: (layer, _mod_row(i), part, 0, 0))


def _silu(v):
    return v * jax.nn.sigmoid(v)


def _rms_mod(x, gain, shift, scale):
    ms = jnp.mean(x * x, axis=-1, keepdims=True)
    y = x * lax.rsqrt(ms + NORM_EPS) * gain
    return y * (1.0 + scale) + shift


def _ada_kernel(c_ref, w_ref, b_ref, o_ref):
    s = _silu(c_ref[...])
    o_ref[0] = jnp.dot(s.astype(BF16), w_ref[0].astype(BF16), preferred_element_type=F32) + b_ref[0]


def _ada_mod(c_all, w_ada, b_ada):
    tn = 1536
    return pl.pallas_call(
        _ada_kernel,
        out_shape=jax.ShapeDtypeStruct((DEPTH, MOD_ROWS, 6 * D_MODEL), F32),
        grid=(DEPTH, 6 * D_MODEL // tn),
        in_specs=[pl.BlockSpec((MOD_ROWS, D_MODEL), lambda l, n: (0, 0)),
                  pl.BlockSpec((1, D_MODEL, tn), lambda l, n: (l, 0, n)),
                  pl.BlockSpec((1, 1, tn), lambda l, n: (l, 0, n))],
        out_specs=pl.BlockSpec((1, MOD_ROWS, tn), lambda l, n: (l, 0, n)),
        compiler_params=pltpu.CompilerParams(vmem_limit_bytes=VMEM_LIMIT),
        name="ada_mod",
    )(c_all, w_ada, b_ada.reshape(DEPTH, 1, 6 * D_MODEL))


def _inproj_kernel(x_ref, g_ref, sh_ref, sc_ref, w_ref, cos_ref, sa_ref, sb_ref, o_ref):
    h = _rms_mod(x_ref[...], g_ref[...], sh_ref[...], sc_ref[...]).astype(BF16)
    cos, sa, sb = cos_ref[...], sa_ref[...], sb_ref[...]
    k_scale = HEAD_DIM ** -0.5
    for part, mul in ((0, 1.0), (1, k_scale)):
        z = jnp.dot(h, w_ref[:, part * RET_WIDTH:(part + 1) * RET_WIDTH], preferred_element_type=F32)
        for hh in range(RET_HEADS):
            t = z[:, hh * HEAD_DIM:(hh + 1) * HEAD_DIM]
            r = t * cos + pltpu.roll(t, 96, axis=1) * sa + pltpu.roll(t, 32, axis=1) * sb
            if mul != 1.0:
                r = r * mul
            col = part * RET_WIDTH + hh * HEAD_DIM
            o_ref[:, col:col + HEAD_DIM] = r.astype(BF16)
    z = jnp.dot(h, w_ref[:, 2 * RET_WIDTH:], preferred_element_type=F32)
    o_ref[:, 2 * RET_WIDTH:] = z.astype(BF16)


def _rope_tables():
    half = HEAD_DIM // 2
    inv_freq = ROPE_BASE ** (-jnp.arange(0, half, 2, dtype=F32) / half)
    t = jnp.arange(SEQ)
    rows, cols = (t // GRID_W).astype(F32), (t % GRID_W).astype(F32)
    ang_r = rows[:, None] * inv_freq[None, :]
    ang_c = cols[:, None] * inv_freq[None, :]
    zero = jnp.zeros_like(ang_r)
    cos = jnp.concatenate([jnp.cos(ang_r), jnp.cos(ang_r), jnp.cos(ang_c), jnp.cos(ang_c)], axis=1)
    sa = jnp.concatenate([-jnp.sin(ang_r), zero, -jnp.sin(ang_c), zero], axis=1)
    sb = jnp.concatenate([zero, jnp.sin(ang_r), zero, jnp.sin(ang_c)], axis=1)
    ident = jnp.ones((TM, HEAD_DIM), F32)
    none = jnp.zeros((TM, HEAD_DIM), F32)
    return (jnp.concatenate([cos, ident]), jnp.concatenate([sa, none]), jnp.concatenate([sb, none]))


def _inproj(x_all, norm_g, mod5, w_in, tables, layer, n_tiles):
    tab_spec = pl.BlockSpec((TM, HEAD_DIM),
                            lambda i: (jnp.where(i < LAT_TILES, i % TILES_PER_BATCH, TILES_PER_BATCH), 0))
    return pl.pallas_call(
        _inproj_kernel,
        out_shape=jax.ShapeDtypeStruct((N_ROWS, IN_COLS), BF16),
        grid=(n_tiles,),
        in_specs=[pl.BlockSpec((TM, D_MODEL), lambda i: (i, 0)),
                  pl.BlockSpec((1, D_MODEL), lambda i: (0, 0)),
                  _mod_spec(layer, 0), _mod_spec(layer, 1),
                  pl.BlockSpec((D_MODEL, IN_COLS), lambda i: (0, 0)),
                  tab_spec, tab_spec, tab_spec],
        out_specs=pl.BlockSpec((TM, IN_COLS), lambda i: (i, 0)),
        compiler_params=pltpu.CompilerParams(vmem_limit_bytes=VMEM_LIMIT),
        name="inproj",
    )(x_all, norm_g, mod5, mod5, w_in, *tables)


N_CHUNK_LAT = SEQ // CHUNK
N_CHUNK_CTX = CTX_LEN // CHUNK
N_CHUNK = N_CHUNK_LAT + N_CHUNK_CTX


def _ret_kernel(lg_ref, ql, kl, vl, gl, qc, kc, vc, gc, gain_ref, ol, oc, kv_scr, st_scr):
    hd = pl.program_id(1)
    lgf = lg_ref[0, hd]
    lgb = lg_ref[1, hd]
    pi = lax.broadcasted_iota(jnp.int32, (CHUNK, CHUNK), 0).astype(F32)
    pj = lax.broadcasted_iota(jnp.int32, (CHUNK, CHUNK), 1).astype(F32)
    tail_f = jnp.exp(lgf * (CHUNK - 1.0 - pi))
    tail_b = jnp.exp(lgb * pi)
    head_f = jnp.exp(lgf * (pi + 1.0))
    head_b = jnp.exp(lgb * (CHUNK - pi))
    dif = pi - pj
    decay = jnp.where(dif >= 0, jnp.exp(lgf * jnp.maximum(dif, 0.0)), jnp.exp(lgb * jnp.maximum(-dif, 0.0)))
    zeros = jnp.zeros((CHUNK, CHUNK), F32)
    gchunk_f = jnp.exp(zeros + lgf * CHUNK)
    gchunk_b = jnp.exp(zeros + lgb * CHUNK)
    gain = gain_ref[...]

    def kv_of(k, v):
        kf = k.astype(F32)
        kk = jnp.concatenate([(kf * tail_f).astype(BF16), (kf * tail_b).astype(BF16)], axis=1)
        return lax.dot_general(kk, v, (((0,), (0,)), ((), ())), preferred_element_type=F32)

    for c in range(N_CHUNK_CTX):
        kv_scr[c] = kv_of(kc[c * CHUNK:(c + 1) * CHUNK, :], vc[c * CHUNK:(c + 1) * CHUNK, :])

    def kv_body(c, carry):
        r = pl.multiple_of(c * CHUNK, CHUNK)
        kv_scr[N_CHUNK_CTX + c] = kv_of(kl[pl.ds(r, CHUNK), :], vl[pl.ds(r, CHUNK), :])
        return carry

    lax.fori_loop(0, N_CHUNK_LAT, kv_body, 0)

    def fwd_body(c, s):
        st_scr[c, 0:HEAD_DIM, :] = s.astype(BF16)
        return gchunk_f * s + kv_scr[c, 0:HEAD_DIM, :]

    lax.fori_loop(0, N_CHUNK, fwd_body, zeros)

    def bwd_step(c, s):
        st_scr[c, HEAD_DIM:2 * HEAD_DIM, :] = s.astype(BF16)
        return gchunk_b * s + kv_scr[c, HEAD_DIM:2 * HEAD_DIM, :]

    s = zeros
    for c in reversed(range(N_CHUNK_CTX)):
        s = bwd_step(c, s)
    lax.fori_loop(0, N_CHUNK_LAT, lambda t, s: bwd_step(N_CHUNK - 1 - t, s), s)

    def out_of(q, k, v, g, st):
        sc = lax.dot_general(q, k, (((1,), (1,)), ((), ())), preferred_element_type=F32)
        intra = jnp.dot((sc * decay).astype(BF16), v, preferred_element_type=F32)
        qf = q.astype(F32)
        qq = jnp.concatenate([(qf * head_f).astype(BF16), (qf * head_b).astype(BF16)], axis=1)
        o = intra + jnp.dot(qq, st, preferred_element_type=F32)
        mu = jnp.mean(o, axis=-1, keepdims=True)
        var = jnp.mean(jnp.square(o - mu), axis=-1, keepdims=True)
        on = (o - mu) * lax.rsqrt(var + GN_EPS)
        return (on * gain * _silu(g.astype(F32))).astype(BF16)

    for c in range(N_CHUNK_CTX):
        sl = slice(c * CHUNK, (c + 1) * CHUNK)
        oc[sl, :] = out_of(qc[sl, :], kc[sl, :], vc[sl, :], gc[sl, :], st_scr[c])

    def out_body(c, carry):
        r = pl.multiple_of(c * CHUNK, CHUNK)
        sl = pl.ds(r, CHUNK)
        ol[sl, :] = out_of(ql[sl, :], kl[sl, :], vl[sl, :], gl[sl, :], st_scr[N_CHUNK_CTX + c])
        return carry

    lax.fori_loop(0, N_CHUNK_LAT, out_body, 0)


def _retention(qkvgp, log_gamma, gn_g):
    def lat(part):
        return pl.BlockSpec((SEQ, HEAD_DIM), lambda b, h: (b, part * RET_HEADS + h))

    def ctx(part):
        return pl.BlockSpec((CTX_LEN, HEAD_DIM), lambda b, h: (N_LAT // CTX_LEN + b, part * RET_HEADS + h))

    return pl.pallas_call(
        _ret_kernel,
        out_shape=(jax.ShapeDtypeStruct((N_LAT, RET_WIDTH), BF16),
                   jax.ShapeDtypeStruct((N_CTX, RET_WIDTH), BF16)),
        grid=(BATCH, RET_HEADS),
        in_specs=[pl.BlockSpec(memory_space=pltpu.SMEM),
                  lat(0), lat(1), lat(2), lat(3), ctx(0), ctx(1), ctx(2), ctx(3),
                  pl.BlockSpec((1, HEAD_DIM), lambda b, h: (0, h))],
        out_specs=(pl.BlockSpec((SEQ, HEAD_DIM), lambda b, h: (b, h)),
                   pl.BlockSpec((CTX_LEN, HEAD_DIM), lambda b, h: (b, h))),
        scratch_shapes=[pltpu.VMEM((N_CHUNK, 2 * HEAD_DIM, HEAD_DIM), F32),
                        pltpu.VMEM((N_CHUNK, 2 * HEAD_DIM, HEAD_DIM), BF16)],
        compiler_params=pltpu.CompilerParams(vmem_limit_bytes=VMEM_LIMIT),
        name="retention",
    )(log_gamma, *([qkvgp] * 8), gn_g)


POOL_BLK = 256


def _split_dot(m, x):
    hi = x.astype(BF16)
    lo = (x - hi.astype(F32)).astype(BF16)
    return jnp.dot(m, hi, preferred_element_type=F32) + jnp.dot(m, lo, preferred_element_type=F32)


def _window_count(idx, w, n):
    return (jnp.minimum(idx - w // 2 + w, n) - jnp.maximum(idx - w // 2, 0)).astype(F32)


def _pool_kernel(pl_ref, pc_ref, w_ref, scale_ref, ol, oc):
    ti = lax.broadcasted_iota(jnp.int32, (POOL_BLK, POOL_BLK), 0)
    tj = lax.broadcasted_iota(jnp.int32, (POOL_BLK, POOL_BLK), 1)
    tok = lax.broadcasted_iota(jnp.int32, (SEQ, POOL_CH), 0)
    tok_c = lax.broadcasted_iota(jnp.int32, (CTX_LEN, POOL_CH), 0)
    for gi, w in enumerate(POOL_WINDOWS):
        cols = slice(gi * POOL_CH, (gi + 1) * POOL_CH)
        lo_off, hi_off = -(w // 2), w - 1 - w // 2
        wmat = w_ref[gi]
        scale = scale_ref[:, cols]

        u = pl_ref[:, cols].astype(F32)
        pad = jnp.zeros((w // 2 * GRID_W, POOL_CH), F32)
        s = jnp.concatenate([pad, u, pad], axis=0)
        span = 1
        while span < w:
            n = s.shape[0] - span * GRID_W
            s = s[:n] + s[span * GRID_W:]
            span *= 2
        row_mean = s[:SEQ] / _window_count(tok >> GRID_SHIFT, w, GRID_W)
        ci, cj = ti & (GRID_W - 1), tj & (GRID_W - 1)
        band = ((ti >> GRID_SHIFT == tj >> GRID_SHIFT) & (cj >= ci + lo_off) & (cj <= ci + hi_off))
        band = jnp.where(band, 1.0, 0.0).astype(BF16)
        col_cnt = _window_count(tok & (GRID_W - 1), w, GRID_W)
        for blk in range(SEQ // POOL_BLK):
            rs = slice(blk * POOL_BLK, (blk + 1) * POOL_BLK)
            m = _split_dot(band, row_mean[rs]) / col_cnt[rs]
            d = (m - u[rs]).astype(BF16)
            y = jnp.dot(d, wmat, preferred_element_type=F32) * scale
            ol[rs, cols] = y.astype(BF16)

        uc = pc_ref[:, cols].astype(F32)
        band_c = jnp.where((tj >= ti + lo_off) & (tj <= ti + hi_off), 1.0, 0.0).astype(BF16)
        mc = _split_dot(band_c, uc) / _window_count(tok_c, w, CTX_LEN)
        dc = (mc - uc).astype(BF16)
        oc[:, cols] = (jnp.dot(dc, wmat, preferred_element_type=F32) * scale).astype(BF16)


def _pool(qkvgp, pool_w, pool_scale):
    pcol = 4 * RET_WIDTH // POOL_WIDTH
    return pl.pallas_call(
        _pool_kernel,
        out_shape=(jax.ShapeDtypeStruct((N_LAT, POOL_WIDTH), BF16),
                   jax.ShapeDtypeStruct((N_CTX, POOL_WIDTH), BF16)),
        grid=(BATCH,),
        in_specs=[pl.BlockSpec((SEQ, POOL_WIDTH), lambda b: (b, pcol)),
                  pl.BlockSpec((CTX_LEN, POOL_WIDTH), lambda b: (N_LAT // CTX_LEN + b, pcol)),
                  pl.BlockSpec((len(POOL_WINDOWS), POOL_CH, POOL_CH), lambda b: (0, 0, 0)),
                  pl.BlockSpec((1, POOL_WIDTH), lambda b: (0, 0))],
        out_specs=(pl.BlockSpec((SEQ, POOL_WIDTH), lambda b: (b, 0)),
                   pl.BlockSpec((CTX_LEN, POOL_WIDTH), lambda b: (b, 0))),
        compiler_params=pltpu.CompilerParams(vmem_limit_bytes=VMEM_LIMIT),
        name="pool",
    )(qkvgp, qkvgp, pool_w, pool_scale)


def _mix_out(i, rl, rc, pl_, pc, w_ref, x_ref, gt_ref):
    is_lat = i < LAT_TILES
    ret = jnp.where(is_lat, rl[...], rc[...])
    pool = jnp.where(is_lat, pl_[...], pc[...])
    y = (jnp.dot(ret, w_ref[0:RET_WIDTH, :], preferred_element_type=F32)
         + jnp.dot(pool, w_ref[RET_WIDTH:, :], preferred_element_type=F32))
    return x_ref[...] + gt_ref[...] * y


def _outproj_kernel(rl, rc, pl_, pc, w_ref, x_ref, gt_ref, g_ref, sh_ref, sc_ref, xo_ref, h_ref):
    x = _mix_out(pl.program_id(0), rl, rc, pl_, pc, w_ref, x_ref, gt_ref)
    xo_ref[...] = x
    h_ref[...] = _rms_mod(x, g_ref[...], sh_ref[...], sc_ref[...]).astype(BF16)


def _outproj_router_kernel(rl, rc, pl_, pc, w_ref, x_ref, gt_ref, g_ref, sh_ref, sc_ref, wr_ref,
                           xo_ref, h_ref, gate_ref):
    x = _mix_out(pl.program_id(0), rl, rc, pl_, pc, w_ref, x_ref, gt_ref)
    xo_ref[...] = x
    h = _rms_mod(x, g_ref[...], sh_ref[...], sc_ref[...]).astype(BF16)
    h_ref[...] = h
    logits = jnp.dot(h, wr_ref[...], preferred_element_type=F32)
    lane = lax.broadcasted_iota(jnp.int32, logits.shape, 1)
    neg = jnp.float32(-jnp.inf)
    logits = jnp.where(lane < N_EXPERTS, logits, neg)
    m1 = jnp.max(logits, axis=-1, keepdims=True)
    i1 = jnp.min(jnp.where(logits == m1, lane, 128), axis=-1, keepdims=True)
    rest = jnp.where(lane == i1, neg, logits)
    m2 = jnp.max(rest, axis=-1, keepdims=True)
    i2 = jnp.min(jnp.where(rest == m2, lane, 128), axis=-1, keepdims=True)
    e2 = jnp.exp(m2 - m1)
    den = 1.0 + e2
    gate_ref[...] = jnp.where(lane == i1, 1.0 / den, jnp.where(lane == i2, e2 / den, 0.0))


def _outproj(ret_l, ret_c, pool_l, pool_c, w_out, x_all, mod5, norm_g, layer, n_tiles, router_w=None):
    def lat_spec():
        return pl.BlockSpec((TM, RET_WIDTH), lambda i: (jnp.minimum(i, LAT_TILES - 1), 0))

    def ctx_spec():
        return pl.BlockSpec((TM, RET_WIDTH), lambda i: (jnp.maximum(i - LAT_TILES, 0), 0))

    row = pl.BlockSpec((TM, D_MODEL), lambda i: (i, 0))
    in_specs = [lat_spec(), ctx_spec(), lat_spec(), ctx_spec(),
                pl.BlockSpec((D_MODEL, D_MODEL), lambda i: (0, 0)),
                row, _mod_spec(layer, 2),
                pl.BlockSpec((1, D_MODEL), lambda i: (0, 0)),
                _mod_spec(layer, 3), _mod_spec(layer, 4)]
    args = [ret_l, ret_c, pool_l, pool_c, w_out, x_all, mod5, norm_g, mod5, mod5]
    out_shape = [jax.ShapeDtypeStruct((N_ROWS, D_MODEL), F32), jax.ShapeDtypeStruct((N_ROWS, D_MODEL), BF16)]
    out_specs = [row, row]
    if router_w is None:
        body = _outproj_kernel
    else:
        body = _outproj_router_kernel
        in_specs.append(pl.BlockSpec((D_MODEL, 128), lambda i: (0, 0)))
        args.append(router_w)
        out_shape.append(jax.ShapeDtypeStruct((N_ROWS, 128), F32))
        out_specs.append(pl.BlockSpec((TM, 128), lambda i: (i, 0)))
    return pl.pallas_call(
        body, out_shape=tuple(out_shape), grid=(n_tiles,), in_specs=in_specs, out_specs=tuple(out_specs),
        compiler_params=pltpu.CompilerParams(vmem_limit_bytes=VMEM_LIMIT),
        name="outproj",
    )(*args)


def _ffn_kernel(h_ref, x_ref, gt_ref, gate_ref, w13_ref, w2_ref, o_ref, act_scr, acc_scr, *, n_e):
    e = pl.program_id(1)
    h = h_ref[...]
    for j in range(D_FF // FF_CHUNK):
        u = jnp.dot(h, w13_ref[0, :, j * FF_CHUNK:(j + 1) * FF_CHUNK], preferred_element_type=F32)
        g = jnp.dot(h, w13_ref[0, :, D_FF + j * FF_CHUNK:D_FF + (j + 1) * FF_CHUNK], preferred_element_type=F32)
        act_scr[:, j * FF_CHUNK:(j + 1) * FF_CHUNK] = (_silu(g) * u).astype(BF16)
    y = jnp.dot(act_scr[...], w2_ref[0], preferred_element_type=F32)
    if n_e == 1:
        o_ref[...] = x_ref[...] + gt_ref[...] * y
    else:
        y = gate_ref[0] * y

        @pl.when(e == 0)
        def _():
            acc_scr[...] = y

        @pl.when(e > 0)
        def _():
            acc_scr[...] += y

        @pl.when(e == n_e - 1)
        def _():
            o_ref[...] = x_ref[...] + gt_ref[...] * acc_scr[...]


def _ffn(h2, x_all, mod5, w13, w2, layer, n_tiles, gate_t=None):
    n_e = w13.shape[0]
    if gate_t is None:
        gate_t = jnp.ones((1, N_ROWS, 1), F32)
    row = pl.BlockSpec((TM, D_MODEL), lambda i, e: (i, 0))
    return pl.pallas_call(
        functools.partial(_ffn_kernel, n_e=n_e),
        out_shape=jax.ShapeDtypeStruct((N_ROWS, D_MODEL), F32),
        grid=(n_tiles, n_e),
        in_specs=[row, row, _mod_spec(layer, 5),
                  pl.BlockSpec((1, TM, 1), lambda i, e: (e, i, 0)),
                  pl.BlockSpec((1, D_MODEL, 2 * D_FF), lambda i, e: (e, 0, 0)),
                  pl.BlockSpec((1, D_FF, D_MODEL), lambda i, e: (e, 0, 0))],
        out_specs=row,
        scratch_shapes=[pltpu.VMEM((TM, D_FF), BF16), pltpu.VMEM((TM, D_MODEL), F32)],
        compiler_params=pltpu.CompilerParams(vmem_limit_bytes=VMEM_LIMIT),
        name="ffn",
    )(h2, x_all, mod5, gate_t, w13, w2)


def _final_kernel(x_ref, g_ref, o_ref):
    x = x_ref[...]
    ms = jnp.mean(x * x, axis=-1, keepdims=True)
    o_ref[...] = x * lax.rsqrt(ms + NORM_EPS) * g_ref[...]


def _final_norm(x_all, gain):
    return pl.pallas_call(
        _final_kernel,
        out_shape=jax.ShapeDtypeStruct((N_LAT, D_MODEL), F32),
        grid=(LAT_TILES,),
        in_specs=[pl.BlockSpec((TM, D_MODEL), lambda i: (i, 0)),
                  pl.BlockSpec((1, D_MODEL), lambda i: (0, 0))],
        out_specs=pl.BlockSpec((TM, D_MODEL), lambda i: (i, 0)),
        name="final_norm",
    )(x_all, gain)


def kernel(x, c, ctx, c_ctx, w_ada, b_ada, norm1_g, norm2_g, w_in, ret_decay_logit, ret_gn_g, pool_w, pool_scale,
           w_out, ffn_w13, ffn_w2, router_w, moe_w13, moe_w2, final_norm_g):
    x_all = jnp.concatenate([x.reshape(N_LAT, D_MODEL), ctx.reshape(N_CTX, D_MODEL)], axis=0)
    c_all = jnp.concatenate([c, c_ctx[None, :], jnp.zeros((MOD_ROWS - BATCH - 1, D_MODEL), F32)], axis=0)
    mod5 = _ada_mod(c_all, w_ada, b_ada).reshape(DEPTH, MOD_ROWS, 6, 1, D_MODEL)
    tables = _rope_tables()
    log_gamma = jax.nn.log_sigmoid(ret_decay_logit.astype(F32))
    router_pad = jnp.pad(router_w, ((0, 0), (0, 0), (0, 128 - N_EXPERTS))).astype(BF16)

    for l in range(DEPTH):
        last = l == DEPTH - 1
        n_tiles = LAT_TILES if last else LAT_TILES + CTX_TILES
        qkvgp = _inproj(x_all, norm1_g[l][None, :], mod5, w_in[l].astype(BF16), tables, l,
                        LAT_TILES + CTX_TILES)
        ret_l, ret_c = _retention(qkvgp, log_gamma[l], ret_gn_g[l][None, :])
        pool_l, pool_c = _pool(qkvgp, pool_w[l].astype(BF16), pool_scale[l][None, :])
        if l % 2 == 0:
            x_all, h2 = _outproj(ret_l, ret_c, pool_l, pool_c, w_out[l].astype(BF16), x_all, mod5,
                                 norm2_g[l][None, :], l, n_tiles)
            i = l // 2
            x_all = _ffn(h2, x_all, mod5, ffn_w13[i][None].astype(BF16), ffn_w2[i][None].astype(BF16), l, n_tiles)
        else:
            i = l // 2
            x_all, h2, gate = _outproj(ret_l, ret_c, pool_l, pool_c, w_out[l].astype(BF16), x_all, mod5,
                                       norm2_g[l][None, :], l, n_tiles, router_w=router_pad[i])
            gate_t = gate[:, :N_EXPERTS].T[:, :, None]
            x_all = _ffn(h2, x_all, mod5, moe_w13[i].astype(BF16), moe_w2[i].astype(BF16), l, n_tiles,
                         gate_t=gate_t)
    out = _final_norm(x_all, final_norm_g[None, :])
    return out.reshape(BATCH, SEQ, D_MODEL)
```

```python
import functools

import jax
import jax.numpy as jnp
from jax import lax
from jax.experimental import pallas as pl
from jax.experimental.pallas import tpu as pltpu

F32 = jnp.float32
BF16 = jnp.bfloat16

D_MODEL = 1024
BATCH = 8
SEQ = 4096
DEPTH = 4
GRID_W = 64
GRID_SHIFT = 6
CTX_LEN = 256
RET_WIDTH = 512
POOL_WIDTH = 512
RET_HEADS = 4
HEAD_DIM = 128
CHUNK = 128
ROPE_BASE = 10000.0
POOL_WINDOWS = (2, 4, 8, 16)
POOL_CH = 128
IN_COLS = 4 * RET_WIDTH + POOL_WIDTH
D_FF = 2816
N_EXPERTS = 8
NORM_EPS = 1e-6
GN_EPS = 1e-5

N_LAT = BATCH * SEQ
N_CTX = BATCH * CTX_LEN
N_ROWS = N_LAT + N_CTX
MOD_ROWS = 16
CTX_MOD_ROW = BATCH

TM = 512
LAT_TILES = N_LAT // TM
CTX_TILES = N_CTX // TM
TILES_PER_BATCH = SEQ // TM
FF_CHUNK = 256
VMEM_LIMIT = 56 * 1024 * 1024


def _mod_row(i):
    return jnp.where(i < LAT_TILES, i // TILES_PER_BATCH, CTX_MOD_ROW)


def _mod_spec(layer, part):
    return pl.BlockSpec((None, None, None, 1, D_MODEL),
                        lambda i, *_: (layer, _mod_row(i), part, 0, 0))


def _silu(v):
    return v * jax.nn.sigmoid(v)


def _rms_mod(x, gain, shift, scale):
    ms = jnp.mean(x * x, axis=-1, keepdims=True)
    y = x * lax.rsqrt(ms + NORM_EPS) * gain
    return y * (1.0 + scale) + shift


def _ada_kernel(c_ref, w_ref, b_ref, o_ref):
    s = _silu(c_ref[...])
    o_ref[0] = jnp.dot(s.astype(BF16), w_ref[0].astype(BF16), preferred_element_type=F32) + b_ref[0]


def _ada_mod(c_all, w_ada, b_ada):
    tn = 1536
    return pl.pallas_call(
        _ada_kernel,
        out_shape=jax.ShapeDtypeStruct((DEPTH, MOD_ROWS, 6 * D_MODEL), F32),
        grid=(DEPTH, 6 * D_MODEL // tn),
        in_specs=[pl.BlockSpec((MOD_ROWS, D_MODEL), lambda l, n: (0, 0)),
                  pl.BlockSpec((1, D_MODEL, tn), lambda l, n: (l, 0, n)),
                  pl.BlockSpec((1, 1, tn), lambda l, n: (l, 0, n))],
        out_specs=pl.BlockSpec((1, MOD_ROWS, tn), lambda l, n: (l, 0, n)),
        compiler_params=pltpu.CompilerParams(vmem_limit_bytes=VMEM_LIMIT),
        name="ada_mod",
    )(c_all, w_ada, b_ada.reshape(DEPTH, 1, 6 * D_MODEL))


def _inproj_kernel(x_ref, g_ref, sh_ref, sc_ref, w_ref, cos_ref, sa_ref, sb_ref, o_ref):
    h = _rms_mod(x_ref[...], g_ref[...], sh_ref[...], sc_ref[...]).astype(BF16)
    cos, sa, sb = cos_ref[...], sa_ref[...], sb_ref[...]
    k_scale = HEAD_DIM ** -0.5
    for part, mul in ((0, 1.0), (1, k_scale)):
        z = jnp.dot(h, w_ref[:, part * RET_WIDTH:(part + 1) * RET_WIDTH], preferred_element_type=F32)
        for hh in range(RET_HEADS):
            t = z[:, hh * HEAD_DIM:(hh + 1) * HEAD_DIM]
            r = t * cos + pltpu.roll(t, 96, axis=1) * sa + pltpu.roll(t, 32, axis=1) * sb
            if mul != 1.0:
                r = r * mul
            col = part * RET_WIDTH + hh * HEAD_DIM
            o_ref[:, col:col + HEAD_DIM] = r.astype(BF16)
    z = jnp.dot(h, w_ref[:, 2 * RET_WIDTH:], preferred_element_type=F32)
    o_ref[:, 2 * RET_WIDTH:] = z.astype(BF16)


def _rope_tables():
    half = HEAD_DIM // 2
    inv_freq = ROPE_BASE ** (-jnp.arange(0, half, 2, dtype=F32) / half)
    t = jnp.arange(SEQ)
    rows, cols = (t // GRID_W).astype(F32), (t % GRID_W).astype(F32)
    ang_r = rows[:, None] * inv_freq[None, :]
    ang_c = cols[:, None] * inv_freq[None, :]
    zero = jnp.zeros_like(ang_r)
    cos = jnp.concatenate([jnp.cos(ang_r), jnp.cos(ang_r), jnp.cos(ang_c), jnp.cos(ang_c)], axis=1)
    sa = jnp.concatenate([-jnp.sin(ang_r), zero, -jnp.sin(ang_c), zero], axis=1)
    sb = jnp.concatenate([zero, jnp.sin(ang_r), zero, jnp.sin(ang_c)], axis=1)
    ident = jnp.ones((TM, HEAD_DIM), F32)
    none = jnp.zeros((TM, HEAD_DIM), F32)
    return (jnp.concatenate([cos, ident]), jnp.concatenate([sa, none]), jnp.concatenate([sb, none]))


def _inproj(x_all, norm_g, mod5, w_in, tables, layer, n_tiles):
    tab_spec = pl.BlockSpec((TM, HEAD_DIM),
                            lambda i: (jnp.where(i < LAT_TILES, i % TILES_PER_BATCH, TILES_PER_BATCH), 0))
    return pl.pallas_call(
        _inproj_kernel,
        out_shape=jax.ShapeDtypeStruct((N_ROWS, IN_COLS), BF16),
        grid=(n_tiles,),
        in_specs=[pl.BlockSpec((TM, D_MODEL), lambda i: (i, 0)),
                  pl.BlockSpec((1, D_MODEL), lambda i: (0, 0)),
                  _mod_spec(layer, 0), _mod_spec(layer, 1),
                  pl.BlockSpec((D_MODEL, IN_COLS), lambda i: (0, 0)),
                  tab_spec, tab_spec, tab_spec],
        out_specs=pl.BlockSpec((TM, IN_COLS), lambda i: (i, 0)),
        compiler_params=pltpu.CompilerParams(vmem_limit_bytes=VMEM_LIMIT),
        name="inproj",
    )(x_all, norm_g, mod5, mod5, w_in, *tables)


N_CHUNK_LAT = SEQ // CHUNK
N_CHUNK_CTX = CTX_LEN // CHUNK
N_CHUNK = N_CHUNK_LAT + N_CHUNK_CTX


def _ret_kernel(lg_ref, ql, kl, vl, gl, qc, kc, vc, gc, gain_ref, ol, oc, kv_scr, st_scr):
    hd = pl.program_id(1)
    lgf = lg_ref[0, hd]
    lgb = lg_ref[1, hd]
    pi = lax.broadcasted_iota(jnp.int32, (CHUNK, CHUNK), 0).astype(F32)
    pj = lax.broadcasted_iota(jnp.int32, (CHUNK, CHUNK), 1).astype(F32)
    tail_f = jnp.exp(lgf * (CHUNK - 1.0 - pi))
    tail_b = jnp.exp(lgb * pi)
    head_f = jnp.exp(lgf * (pi + 1.0))
    head_b = jnp.exp(lgb * (CHUNK - pi))
    dif = pi - pj
    decay = jnp.where(dif >= 0, jnp.exp(lgf * jnp.maximum(dif, 0.0)), jnp.exp(lgb * jnp.maximum(-dif, 0.0)))
    zeros = jnp.zeros((CHUNK, CHUNK), F32)
    gchunk_f = jnp.exp(zeros + lgf * CHUNK)
    gchunk_b = jnp.exp(zeros + lgb * CHUNK)
    gain = gain_ref[...]

    def kv_of(k, v):
        kf = k.astype(F32)
        kk = jnp.concatenate([(kf * tail_f).astype(BF16), (kf * tail_b).astype(BF16)], axis=1)
        return lax.dot_general(kk, v, (((0,), (0,)), ((), ())), preferred_element_type=F32)

    for c in range(N_CHUNK_CTX):
        kv_scr[c] = kv_of(kc[c * CHUNK:(c + 1) * CHUNK, :], vc[c * CHUNK:(c + 1) * CHUNK, :])

    def kv_body(c, carry):
        r = pl.multiple_of(c * CHUNK, CHUNK)
        kv_scr[N_CHUNK_CTX + c] = kv_of(kl[pl.ds(r, CHUNK), :], vl[pl.ds(r, CHUNK), :])
        return carry

    lax.fori_loop(0, N_CHUNK_LAT, kv_body, 0)

    def fwd_body(c, s):
        st_scr[c, 0:HEAD_DIM, :] = s.astype(BF16)
        return gchunk_f * s + kv_scr[c, 0:HEAD_DIM, :]

    lax.fori_loop(0, N_CHUNK, fwd_body, zeros)

    def bwd_step(c, s):
        st_scr[c, HEAD_DIM:2 * HEAD_DIM, :] = s.astype(BF16)
        return gchunk_b * s + kv_scr[c, HEAD_DIM:2 * HEAD_DIM, :]

    s = zeros
    for c in reversed(range(N_CHUNK_CTX)):
        s = bwd_step(c, s)
    lax.fori_loop(0, N_CHUNK_LAT, lambda t, s: bwd_step(N_CHUNK - 1 - t, s), s)

    def out_of(q, k, v, g, st):
        sc = lax.dot_general(q, k, (((1,), (1,)), ((), ())), preferred_element_type=F32)
        intra = jnp.dot((sc * decay).astype(BF16), v, preferred_element_type=F32)
        qf = q.astype(F32)
        qq = jnp.concatenate([(qf * head_f).astype(BF16), (qf * head_b).astype(BF16)], axis=1)
        o = intra + jnp.dot(qq, st, preferred_element_type=F32)
        mu = jnp.mean(o, axis=-1, keepdims=True)
        var = jnp.mean(jnp.square(o - mu), axis=-1, keepdims=True)
        on = (o - mu) * lax.rsqrt(var + GN_EPS)
        return (on * gain * _silu(g.astype(F32))).astype(BF16)

    for c in range(N_CHUNK_CTX):
        sl = slice(c * CHUNK, (c + 1) * CHUNK)
        oc[sl, :] = out_of(qc[sl, :], kc[sl, :], vc[sl, :], gc[sl, :], st_scr[c])

    def out_body(c, carry):
        r = pl.multiple_of(c * CHUNK, CHUNK)
        sl = pl.ds(r, CHUNK)
        ol[sl, :] = out_of(ql[sl, :], kl[sl, :], vl[sl, :], gl[sl, :], st_scr[N_CHUNK_CTX + c])
        return carry

    lax.fori_loop(0, N_CHUNK_LAT, out_body, 0)


def _retention(qkvgp, log_gamma, gn_g):
    def lat(part):
        return pl.BlockSpec((SEQ, HEAD_DIM), lambda b, h: (b, part * RET_HEADS + h))

    def ctx(part):
        return pl.BlockSpec((CTX_LEN, HEAD_DIM), lambda b, h: (N_LAT // CTX_LEN + b, part * RET_HEADS + h))

    return pl.pallas_call(
        _ret_kernel,
        out_shape=(jax.ShapeDtypeStruct((N_LAT, RET_WIDTH), BF16),
                   jax.ShapeDtypeStruct((N_CTX, RET_WIDTH), BF16)),
        grid=(BATCH, RET_HEADS),
        in_specs=[pl.BlockSpec(memory_space=pltpu.SMEM),
                  lat(0), lat(1), lat(2), lat(3), ctx(0), ctx(1), ctx(2), ctx(3),
                  pl.BlockSpec((1, HEAD_DIM), lambda b, h: (0, h))],
        out_specs=(pl.BlockSpec((SEQ, HEAD_DIM), lambda b, h: (b, h)),
                   pl.BlockSpec((CTX_LEN, HEAD_DIM), lambda b, h: (b, h))),
        scratch_shapes=[pltpu.VMEM((N_CHUNK, 2 * HEAD_DIM, HEAD_DIM), F32),
                        pltpu.VMEM((N_CHUNK, 2 * HEAD_DIM, HEAD_DIM), BF16)],
        compiler_params=pltpu.CompilerParams(vmem_limit_bytes=VMEM_LIMIT),
        name="retention",
    )(log_gamma, *([qkvgp] * 8), gn_g)


POOL_BLK = 256


def _split_dot(m, x):
    hi = x.astype(BF16)
    lo = (x - hi.astype(F32)).astype(BF16)
    return jnp.dot(m, hi, preferred_element_type=F32) + jnp.dot(m, lo, preferred_element_type=F32)


def _window_count(idx, w, n):
    return (jnp.minimum(idx - w // 2 + w, n) - jnp.maximum(idx - w // 2, 0)).astype(F32)


def _pool_kernel(pl_ref, pc_ref, w_ref, scale_ref, ol, oc):
    ti = lax.broadcasted_iota(jnp.int32, (POOL_BLK, POOL_BLK), 0)
    tj = lax.broadcasted_iota(jnp.int32, (POOL_BLK, POOL_BLK), 1)
    tok = lax.broadcasted_iota(jnp.int32, (SEQ, POOL_CH), 0)
    tok_c = lax.broadcasted_iota(jnp.int32, (CTX_LEN, POOL_CH), 0)
    for gi, w in enumerate(POOL_WINDOWS):
        cols = slice(gi * POOL_CH, (gi + 1) * POOL_CH)
        lo_off, hi_off = -(w // 2), w - 1 - w // 2
        wmat = w_ref[gi]
        scale = scale_ref[:, cols]

        u = pl_ref[:, cols].astype(F32)
        pad = jnp.zeros((w // 2 * GRID_W, POOL_CH), F32)
        s = jnp.concatenate([pad, u, pad], axis=0)
        span = 1
        while span < w:
            n = s.shape[0] - span * GRID_W
            s = s[:n] + s[span * GRID_W:]
            span *= 2
        row_mean = s[:SEQ] / _window_count(tok >> GRID_SHIFT, w, GRID_W)
        ci, cj = ti & (GRID_W - 1), tj & (GRID_W - 1)
        band = ((ti >> GRID_SHIFT == tj >> GRID_SHIFT) & (cj >= ci + lo_off) & (cj <= ci + hi_off))
        band = jnp.where(band, 1.0, 0.0).astype(BF16)
        col_cnt = _window_count(tok & (GRID_W - 1), w, GRID_W)
        for blk in range(SEQ // POOL_BLK):
            rs = slice(blk * POOL_BLK, (blk + 1) * POOL_BLK)
            m = _split_dot(band, row_mean[rs]) / col_cnt[rs]
            d = (m - u[rs]).astype(BF16)
            y = jnp.dot(d, wmat, preferred_element_type=F32) * scale
            ol[rs, cols] = y.astype(BF16)

        uc = pc_ref[:, cols].astype(F32)
        band_c = jnp.where((tj >= ti + lo_off) & (tj <= ti + hi_off), 1.0, 0.0).astype(BF16)
        mc = _split_dot(band_c, uc) / _window_count(tok_c, w, CTX_LEN)
        dc = (mc - uc).astype(BF16)
        oc[:, cols] = (jnp.dot(dc, wmat, preferred_element_type=F32) * scale).astype(BF16)


def _pool(qkvgp, pool_w, pool_scale):
    pcol = 4 * RET_WIDTH // POOL_WIDTH
    return pl.pallas_call(
        _pool_kernel,
        out_shape=(jax.ShapeDtypeStruct((N_LAT, POOL_WIDTH), BF16),
                   jax.ShapeDtypeStruct((N_CTX, POOL_WIDTH), BF16)),
        grid=(BATCH,),
        in_specs=[pl.BlockSpec((SEQ, POOL_WIDTH), lambda b: (b, pcol)),
                  pl.BlockSpec((CTX_LEN, POOL_WIDTH), lambda b: (N_LAT // CTX_LEN + b, pcol)),
                  pl.BlockSpec((len(POOL_WINDOWS), POOL_CH, POOL_CH), lambda b: (0, 0, 0)),
                  pl.BlockSpec((1, POOL_WIDTH), lambda b: (0, 0))],
        out_specs=(pl.BlockSpec((SEQ, POOL_WIDTH), lambda b: (b, 0)),
                   pl.BlockSpec((CTX_LEN, POOL_WIDTH), lambda b: (b, 0))),
        compiler_params=pltpu.CompilerParams(vmem_limit_bytes=VMEM_LIMIT),
        name="pool",
    )(qkvgp, qkvgp, pool_w, pool_scale)


def _mix_out(i, rl, rc, pl_, pc, w_ref, x_ref, gt_ref):
    is_lat = i < LAT_TILES
    ret = jnp.where(is_lat, rl[...], rc[...])
    pool = jnp.where(is_lat, pl_[...], pc[...])
    y = (jnp.dot(ret, w_ref[0:RET_WIDTH, :], preferred_element_type=F32)
         + jnp.dot(pool, w_ref[RET_WIDTH:, :], preferred_element_type=F32))
    return x_ref[...] + gt_ref[...] * y


def _outproj_kernel(rl, rc, pl_, pc, w_ref, x_ref, gt_ref, g_ref, sh_ref, sc_ref, xo_ref, h_ref):
    x = _mix_out(pl.program_id(0), rl, rc, pl_, pc, w_ref, x_ref, gt_ref)
    xo_ref[...] = x
    h_ref[...] = _rms_mod(x, g_ref[...], sh_ref[...], sc_ref[...]).astype(BF16)


def _outproj_router_kernel(rl, rc, pl_, pc, w_ref, x_ref, gt_ref, g_ref, sh_ref, sc_ref, wr_ref,
                           xo_ref, h_ref, route_ref, cnt_ref, carry_scr):
    i = pl.program_id(0)
    x = _mix_out(i, rl, rc, pl_, pc, w_ref, x_ref, gt_ref)
    xo_ref[...] = x
    h = _rms_mod(x, g_ref[...], sh_ref[...], sc_ref[...])
    h_ref[...] = h
    logits = jnp.dot(h.astype(BF16), wr_ref[...], preferred_element_type=F32)
    lane = lax.broadcasted_iota(jnp.int32, logits.shape, 1)
    neg = jnp.float32(-jnp.inf)
    logits = jnp.where(lane < N_EXPERTS, logits, neg)
    m1 = jnp.max(logits, axis=-1, keepdims=True)
    i1 = jnp.min(jnp.where(logits == m1, lane, 128), axis=-1, keepdims=True)
    rest = jnp.where(lane == i1, neg, logits)
    m2 = jnp.max(rest, axis=-1, keepdims=True)
    i2 = jnp.min(jnp.where(rest == m2, lane, 128), axis=-1, keepdims=True)
    e2 = jnp.exp(m2 - m1)
    den = 1.0 + e2

    @pl.when(i == 0)
    def _():
        carry_scr[...] = jnp.zeros_like(carry_scr)

    sel1, sel2 = lane == i1, lane == i2
    picks = jnp.where(sel1 | sel2, 1.0, 0.0)
    ti = lax.broadcasted_iota(jnp.int32, (TM, TM), 0)
    tj = lax.broadcasted_iota(jnp.int32, (TM, TM), 1)
    earlier = jnp.where(tj < ti, 1.0, 0.0).astype(BF16)
    before = jnp.dot(earlier, picks.astype(BF16), preferred_element_type=F32) + carry_scr[...]
    r1 = jnp.sum(jnp.where(sel1, before, 0.0), axis=-1, keepdims=True)
    r2 = jnp.sum(jnp.where(sel2, before, 0.0), axis=-1, keepdims=True)
    carry = carry_scr[...] + jnp.sum(picks, axis=0, keepdims=True)
    carry_scr[...] = carry
    cnt_ref[...] = jnp.broadcast_to(carry, cnt_ref.shape)

    cols = (i1.astype(F32), i2.astype(F32), r1, r2, 1.0 / den, e2 / den)
    route = jnp.zeros(logits.shape, F32)
    for k, v in enumerate(cols):
        route = jnp.where(lane == k, v, route)
    route_ref[...] = route


def _outproj(ret_l, ret_c, pool_l, pool_c, w_out, x_all, mod5, norm_g, layer, n_tiles, router_w=None):
    def lat_spec():
        return pl.BlockSpec((TM, RET_WIDTH), lambda i: (jnp.minimum(i, LAT_TILES - 1), 0))

    def ctx_spec():
        return pl.BlockSpec((TM, RET_WIDTH), lambda i: (jnp.maximum(i - LAT_TILES, 0), 0))

    row = pl.BlockSpec((TM, D_MODEL), lambda i: (i, 0))
    in_specs = [lat_spec(), ctx_spec(), lat_spec(), ctx_spec(),
                pl.BlockSpec((D_MODEL, D_MODEL), lambda i: (0, 0)),
                row, _mod_spec(layer, 2),
                pl.BlockSpec((1, D_MODEL), lambda i: (0, 0)),
                _mod_spec(layer, 3), _mod_spec(layer, 4)]
    args = [ret_l, ret_c, pool_l, pool_c, w_out, x_all, mod5, norm_g, mod5, mod5]
    out_shape = [jax.ShapeDtypeStruct((N_ROWS, D_MODEL), F32), jax.ShapeDtypeStruct((N_ROWS, D_MODEL), BF16)]
    out_specs = [row, row]
    scratch = []
    if router_w is None:
        body = _outproj_kernel
    else:
        body = _outproj_router_kernel
        in_specs.append(pl.BlockSpec((D_MODEL, 128), lambda i: (0, 0)))
        args.append(router_w)
        out_shape[1] = jax.ShapeDtypeStruct((N_ROWS, D_MODEL), F32)
        out_shape += [jax.ShapeDtypeStruct((N_ROWS, 128), F32), jax.ShapeDtypeStruct((8, 128), F32)]
        out_specs += [pl.BlockSpec((TM, 128), lambda i: (i, 0)), pl.BlockSpec((8, 128), lambda i: (0, 0))]
        scratch = [pltpu.VMEM((1, 128), F32)]
    return pl.pallas_call(
        body, out_shape=tuple(out_shape), grid=(n_tiles,), in_specs=in_specs, out_specs=tuple(out_specs),
        scratch_shapes=scratch,
        compiler_params=pltpu.CompilerParams(vmem_limit_bytes=VMEM_LIMIT),
        name="outproj",
    )(*args)


def _swiglu(h, w13_ref, w2_ref, act_scr):
    for j in range(D_FF // FF_CHUNK):
        u = jnp.dot(h, w13_ref[0, :, j * FF_CHUNK:(j + 1) * FF_CHUNK], preferred_element_type=F32)
        g = jnp.dot(h, w13_ref[0, :, D_FF + j * FF_CHUNK:D_FF + (j + 1) * FF_CHUNK], preferred_element_type=F32)
        act_scr[:, j * FF_CHUNK:(j + 1) * FF_CHUNK] = (_silu(g) * u).astype(BF16)
    return jnp.dot(act_scr[...], w2_ref[0], preferred_element_type=F32)


def _ffn_kernel(h_ref, x_ref, gt_ref, w13_ref, w2_ref, o_ref, act_scr):
    o_ref[...] = x_ref[...] + gt_ref[...] * _swiglu(h_ref[...], w13_ref, w2_ref, act_scr)


def _ffn(h2, x_all, mod5, w13, w2, layer, n_tiles):
    row = pl.BlockSpec((TM, D_MODEL), lambda i: (i, 0))
    return pl.pallas_call(
        _ffn_kernel,
        out_shape=jax.ShapeDtypeStruct((N_ROWS, D_MODEL), F32),
        grid=(n_tiles,),
        in_specs=[row, row, _mod_spec(layer, 5),
                  pl.BlockSpec((1, D_MODEL, 2 * D_FF), lambda i: (0, 0, 0)),
                  pl.BlockSpec((1, D_FF, D_MODEL), lambda i: (0, 0, 0))],
        out_specs=row,
        scratch_shapes=[pltpu.VMEM((TM, D_FF), BF16)],
        compiler_params=pltpu.CompilerParams(vmem_limit_bytes=VMEM_LIMIT),
        name="ffn",
    )(h2, x_all, mod5, w13, w2)


def _moe_plan(counts, n_steps):
    cnt = counts.astype(jnp.int32)
    n_tiles = (cnt + TM - 1) // TM
    tile_end = jnp.cumsum(n_tiles)
    tile_start = tile_end - n_tiles
    total = tile_end[-1]
    step = jnp.minimum(jnp.arange(n_steps, dtype=jnp.int32), total - 1)
    eid = jnp.minimum(jnp.sum(step[:, None] >= tile_end[None, :], axis=1), N_EXPERTS - 1).astype(jnp.int32)
    valid = jnp.minimum(TM, cnt[eid] - (step - tile_start[eid]) * TM)
    valid = jnp.where(jnp.arange(n_steps) < total, valid, 0).astype(jnp.int32)
    return tile_start * TM, step, eid, valid


def _row_copy_wait(src, dst, sem):
    pltpu.make_async_copy(src, dst, sem).wait()


def _load_positions(pos_hbm, pos_smem, sem, i):
    cp = pltpu.make_async_copy(pos_hbm.at[i], pos_smem, sem)
    cp.start()
    cp.wait()


def _dispatch_kernel(pos_hbm, h_hbm, xs_hbm, pos_smem, sem_pos, sem):
    i = pl.program_id(0)
    _load_positions(pos_hbm, pos_smem, sem_pos, i)
    base = i * TM

    def issue(r, carry):
        for k in range(2):
            p = pos_smem[0, k * TM + r]
            pltpu.make_async_copy(h_hbm.at[pl.ds(base + r, 1)], xs_hbm.at[pl.ds(p, 1)], sem).start()
        return carry

    lax.fori_loop(0, TM, issue, 0, unroll=8)
    for k in range(2):
        _row_copy_wait(h_hbm.at[pl.ds(0, TM)], xs_hbm.at[pl.ds(0, TM)], sem)


def _dispatch(pos, h, n_tiles, n_sorted):
    return pl.pallas_call(
        _dispatch_kernel,
        out_shape=jax.ShapeDtypeStruct((n_sorted, D_MODEL), F32),
        grid=(n_tiles,),
        in_specs=[pl.BlockSpec(memory_space=pl.ANY), pl.BlockSpec(memory_space=pl.ANY)],
        out_specs=pl.BlockSpec(memory_space=pl.ANY),
        scratch_shapes=[pltpu.SMEM((1, 2 * TM), jnp.int32), pltpu.SemaphoreType.DMA, pltpu.SemaphoreType.DMA],
        name="dispatch",
    )(pos, h)


def _expert_ffn_kernel(tile_ref, eid_ref, valid_ref, x_ref, w13_ref, w2_ref, o_ref, act_scr):
    valid = valid_ref[pl.program_id(0)]

    @pl.when(valid > 0)
    def _():
        row = lax.broadcasted_iota(jnp.int32, (TM, D_MODEL), 0)
        h = jnp.where(row < valid, x_ref[...], 0.0).astype(BF16)
        o_ref[...] = _swiglu(h, w13_ref, w2_ref, act_scr)


def _expert_ffn(plan, xs, w13, w2, n_steps):
    _, tile, eid, valid = plan
    row = pl.BlockSpec((TM, D_MODEL), lambda s, tile, eid, valid: (tile[s], 0))
    return pl.pallas_call(
        _expert_ffn_kernel,
        out_shape=jax.ShapeDtypeStruct(xs.shape, F32),
        grid_spec=pltpu.PrefetchScalarGridSpec(
            num_scalar_prefetch=3,
            grid=(n_steps,),
            in_specs=[row,
                      pl.BlockSpec((1, D_MODEL, 2 * D_FF), lambda s, tile, eid, valid: (eid[s], 0, 0)),
                      pl.BlockSpec((1, D_FF, D_MODEL), lambda s, tile, eid, valid: (eid[s], 0, 0))],
            out_specs=row,
            scratch_shapes=[pltpu.VMEM((TM, D_FF), BF16)]),
        compiler_params=pltpu.CompilerParams(vmem_limit_bytes=VMEM_LIMIT),
        name="expert_ffn",
    )(tile, eid, valid, xs, w13, w2)


def _combine_kernel(pos_hbm, y_hbm, route_ref, x_ref, gt_ref, o_ref, pos_smem, ybuf, sem_pos, sem):
    i = pl.program_id(0)
    _load_positions(pos_hbm, pos_smem, sem_pos, i)

    def issue(r, carry):
        for k in range(2):
            p = pos_smem[0, k * TM + r]
            pltpu.make_async_copy(y_hbm.at[pl.ds(p, 1)], ybuf.at[k, pl.ds(r, 1)], sem).start()
        return carry

    lax.fori_loop(0, TM, issue, 0, unroll=8)
    for k in range(2):
        _row_copy_wait(y_hbm.at[pl.ds(0, TM)], ybuf.at[k], sem)
    w1 = route_ref[:, 4:5]
    w2 = route_ref[:, 5:6]
    o_ref[...] = x_ref[...] + gt_ref[...] * (w1 * ybuf[0] + w2 * ybuf[1])


def _combine(pos, y, route, x_all, mod5, layer, n_tiles):
    row = pl.BlockSpec((TM, D_MODEL), lambda i: (i, 0))
    return pl.pallas_call(
        _combine_kernel,
        out_shape=jax.ShapeDtypeStruct((N_ROWS, D_MODEL), F32),
        grid=(n_tiles,),
        in_specs=[pl.BlockSpec(memory_space=pl.ANY), pl.BlockSpec(memory_space=pl.ANY),
                  pl.BlockSpec((TM, 128), lambda i: (i, 0)), row, _mod_spec(layer, 5)],
        out_specs=row,
        scratch_shapes=[pltpu.SMEM((1, 2 * TM), jnp.int32), pltpu.VMEM((2, TM, D_MODEL), F32),
                        pltpu.SemaphoreType.DMA, pltpu.SemaphoreType.DMA],
        compiler_params=pltpu.CompilerParams(vmem_limit_bytes=VMEM_LIMIT),
        name="combine",
    )(pos, y, route, x_all, mod5)


def _moe(h, route, counts, x_all, mod5, w13, w2, layer, n_tiles):
    n_tok = n_tiles * TM
    n_steps = 2 * n_tiles + N_EXPERTS
    plan = _moe_plan(counts[0, :N_EXPERTS], n_steps)
    meta = route[:n_tok, :4].astype(jnp.int32)
    pos = plan[0][meta[:, 0:2]] + meta[:, 2:4]
    pos = pos.reshape(n_tiles, TM, 2).transpose(0, 2, 1).reshape(n_tiles, 1, 2 * TM)
    xs = _dispatch(pos, h, n_tiles, n_steps * TM)
    ys = _expert_ffn(plan, xs, w13, w2, n_steps)
    return _combine(pos, ys, route, x_all, mod5, layer, n_tiles)


def _final_kernel(x_ref, g_ref, o_ref):
    x = x_ref[...]
    ms = jnp.mean(x * x, axis=-1, keepdims=True)
    o_ref[...] = x * lax.rsqrt(ms + NORM_EPS) * g_ref[...]


def _final_norm(x_all, gain):
    return pl.pallas_call(
        _final_kernel,
        out_shape=jax.ShapeDtypeStruct((N_LAT, D_MODEL), F32),
        grid=(LAT_TILES,),
        in_specs=[pl.BlockSpec((TM, D_MODEL), lambda i: (i, 0)),
                  pl.BlockSpec((1, D_MODEL), lambda i: (0, 0))],
        out_specs=pl.BlockSpec((TM, D_MODEL), lambda i: (i, 0)),
        name="final_norm",
    )(x_all, gain)


def kernel(x, c, ctx, c_ctx, w_ada, b_ada, norm1_g, norm2_g, w_in, ret_decay_logit, ret_gn_g, pool_w, pool_scale,
           w_out, ffn_w13, ffn_w2, router_w, moe_w13, moe_w2, final_norm_g):
    x_all = jnp.concatenate([x.reshape(N_LAT, D_MODEL), ctx.reshape(N_CTX, D_MODEL)], axis=0)
    c_all = jnp.concatenate([c, c_ctx[None, :], jnp.zeros((MOD_ROWS - BATCH - 1, D_MODEL), F32)], axis=0)
    mod5 = _ada_mod(c_all, w_ada, b_ada).reshape(DEPTH, MOD_ROWS, 6, 1, D_MODEL)
    tables = _rope_tables()
    log_gamma = jax.nn.log_sigmoid(ret_decay_logit.astype(F32))
    router_pad = jnp.pad(router_w, ((0, 0), (0, 0), (0, 128 - N_EXPERTS))).astype(BF16)

    for l in range(DEPTH):
        last = l == DEPTH - 1
        n_tiles = LAT_TILES if last else LAT_TILES + CTX_TILES
        qkvgp = _inproj(x_all, norm1_g[l][None, :], mod5, w_in[l].astype(BF16), tables, l,
                        LAT_TILES + CTX_TILES)
        ret_l, ret_c = _retention(qkvgp, log_gamma[l], ret_gn_g[l][None, :])
        pool_l, pool_c = _pool(qkvgp, pool_w[l].astype(BF16), pool_scale[l][None, :])
        if l % 2 == 0:
            x_all, h2 = _outproj(ret_l, ret_c, pool_l, pool_c, w_out[l].astype(BF16), x_all, mod5,
                                 norm2_g[l][None, :], l, n_tiles)
            i = l // 2
            x_all = _ffn(h2, x_all, mod5, ffn_w13[i][None].astype(BF16), ffn_w2[i][None].astype(BF16), l, n_tiles)
        else:
            i = l // 2
            x_all, h2, route, counts = _outproj(ret_l, ret_c, pool_l, pool_c, w_out[l].astype(BF16), x_all, mod5,
                                                norm2_g[l][None, :], l, n_tiles, router_w=router_pad[i])
            x_all = _moe(h2, route, counts, x_all, mod5, moe_w13[i].astype(BF16), moe_w2[i].astype(BF16),
                         l, n_tiles)
    out = _final_norm(x_all, final_norm_g[None, :])
    return out.reshape(BATCH, SEQ, D_MODEL)
```

```python
import functools

import jax
import jax.numpy as jnp
from jax import lax
from jax.experimental import pallas as pl
from jax.experimental.pallas import tpu as pltpu

F32 = jnp.float32
BF16 = jnp.bfloat16

D_MODEL = 1024
BATCH = 8
SEQ = 4096
DEPTH = 4
GRID_W = 64
GRID_SHIFT = 6
CTX_LEN = 256
RET_WIDTH = 512
POOL_WIDTH = 512
RET_HEADS = 4
HEAD_DIM = 128
CHUNK = 128
ROPE_BASE = 10000.0
POOL_WINDOWS = (2, 4, 8, 16)
POOL_CH = 128
IN_COLS = 4 * RET_WIDTH + POOL_WIDTH
D_FF = 2816
N_EXPERTS = 8
NORM_EPS = 1e-6
GN_EPS = 1e-5

N_LAT = BATCH * SEQ
N_CTX = BATCH * CTX_LEN
N_ROWS = N_LAT + N_CTX
MOD_ROWS = 16
CTX_MOD_ROW = BATCH

TM = 512
LAT_TILES = N_LAT // TM
CTX_TILES = N_CTX // TM
TILES_PER_BATCH = SEQ // TM
FF_CHUNK = 256
VMEM_LIMIT = 56 * 1024 * 1024


def _mod_row(i):
    return jnp.where(i < LAT_TILES, i // TILES_PER_BATCH, CTX_MOD_ROW)


def _mod_spec(layer, part):
    return pl.BlockSpec((None, None, None, 1, D_MODEL),
                        lambda i, *_: (layer, _mod_row(i), part, 0, 0))


def _silu(v):
    return v * jax.nn.sigmoid(v)


def _rms_mod(x, gain, shift, scale):
    ms = jnp.mean(x * x, axis=-1, keepdims=True)
    y = x * lax.rsqrt(ms + NORM_EPS) * gain
    return y * (1.0 + scale) + shift


def _ada_kernel(c_ref, w_ref, b_ref, o_ref):
    s = _silu(c_ref[...])
    o_ref[0] = jnp.dot(s.astype(BF16), w_ref[0].astype(BF16), preferred_element_type=F32) + b_ref[0]


def _ada_mod(c_all, w_ada, b_ada):
    tn = 1536
    return pl.pallas_call(
        _ada_kernel,
        out_shape=jax.ShapeDtypeStruct((DEPTH, MOD_ROWS, 6 * D_MODEL), F32),
        grid=(DEPTH, 6 * D_MODEL // tn),
        in_specs=[pl.BlockSpec((MOD_ROWS, D_MODEL), lambda l, n: (0, 0)),
                  pl.BlockSpec((1, D_MODEL, tn), lambda l, n: (l, 0, n)),
                  pl.BlockSpec((1, 1, tn), lambda l, n: (l, 0, n))],
        out_specs=pl.BlockSpec((1, MOD_ROWS, tn), lambda l, n: (l, 0, n)),
        compiler_params=pltpu.CompilerParams(vmem_limit_bytes=VMEM_LIMIT),
        name="ada_mod",
    )(c_all, w_ada, b_ada.reshape(DEPTH, 1, 6 * D_MODEL))


def _inproj_kernel(x_ref, g_ref, sh_ref, sc_ref, w_ref, cos_ref, sa_ref, sb_ref, o_ref):
    h = _rms_mod(x_ref[...], g_ref[...], sh_ref[...], sc_ref[...]).astype(BF16)
    cos, sa, sb = cos_ref[...], sa_ref[...], sb_ref[...]
    k_scale = HEAD_DIM ** -0.5
    for part, mul in ((0, 1.0), (1, k_scale)):
        z = jnp.dot(h, w_ref[:, part * RET_WIDTH:(part + 1) * RET_WIDTH], preferred_element_type=F32)
        for hh in range(RET_HEADS):
            t = z[:, hh * HEAD_DIM:(hh + 1) * HEAD_DIM]
            r = t * cos + pltpu.roll(t, 96, axis=1) * sa + pltpu.roll(t, 32, axis=1) * sb
            if mul != 1.0:
                r = r * mul
            col = part * RET_WIDTH + hh * HEAD_DIM
            o_ref[:, col:col + HEAD_DIM] = r.astype(BF16)
    z = jnp.dot(h, w_ref[:, 2 * RET_WIDTH:], preferred_element_type=F32)
    o_ref[:, 2 * RET_WIDTH:] = z.astype(BF16)


def _rope_tables():
    half = HEAD_DIM // 2
    inv_freq = ROPE_BASE ** (-jnp.arange(0, half, 2, dtype=F32) / half)
    t = jnp.arange(SEQ)
    rows, cols = (t // GRID_W).astype(F32), (t % GRID_W).astype(F32)
    ang_r = rows[:, None] * inv_freq[None, :]
    ang_c = cols[:, None] * inv_freq[None, :]
    zero = jnp.zeros_like(ang_r)
    cos = jnp.concatenate([jnp.cos(ang_r), jnp.cos(ang_r), jnp.cos(ang_c), jnp.cos(ang_c)], axis=1)
    sa = jnp.concatenate([-jnp.sin(ang_r), zero, -jnp.sin(ang_c), zero], axis=1)
    sb = jnp.concatenate([zero, jnp.sin(ang_r), zero, jnp.sin(ang_c)], axis=1)
    ident = jnp.ones((TM, HEAD_DIM), F32)
    none = jnp.zeros((TM, HEAD_DIM), F32)
    return (jnp.concatenate([cos, ident]), jnp.concatenate([sa, none]), jnp.concatenate([sb, none]))


def _inproj(x_all, norm_g, mod5, w_in, tables, layer, n_tiles):
    tab_spec = pl.BlockSpec((TM, HEAD_DIM),
                            lambda i: (jnp.where(i < LAT_TILES, i % TILES_PER_BATCH, TILES_PER_BATCH), 0))
    return pl.pallas_call(
        _inproj_kernel,
        out_shape=jax.ShapeDtypeStruct((N_ROWS, IN_COLS), BF16),
        grid=(n_tiles,),
        in_specs=[pl.BlockSpec((TM, D_MODEL), lambda i: (i, 0)),
                  pl.BlockSpec((1, D_MODEL), lambda i: (0, 0)),
                  _mod_spec(layer, 0), _mod_spec(layer, 1),
                  pl.BlockSpec((D_MODEL, IN_COLS), lambda i: (0, 0)),
                  tab_spec, tab_spec, tab_spec],
        out_specs=pl.BlockSpec((TM, IN_COLS), lambda i: (i, 0)),
        compiler_params=pltpu.CompilerParams(vmem_limit_bytes=VMEM_LIMIT),
        name="inproj",
    )(x_all, norm_g, mod5, mod5, w_in, *tables)


N_CHUNK_LAT = SEQ // CHUNK
N_CHUNK_CTX = CTX_LEN // CHUNK
N_CHUNK = N_CHUNK_LAT + N_CHUNK_CTX
RET_UNROLL = 4


def _ret_kernel(lg_ref, ql, kl, vl, gl, qc, kc, vc, gc, gain_ref, ol, oc, kv_scr, st_scr):
    hd = pl.program_id(1)
    lgf = lg_ref[0, hd]
    lgb = lg_ref[1, hd]
    pi = lax.broadcasted_iota(jnp.int32, (CHUNK, CHUNK), 0).astype(F32)
    pj = lax.broadcasted_iota(jnp.int32, (CHUNK, CHUNK), 1).astype(F32)
    tail_f = jnp.exp(lgf * (CHUNK - 1.0 - pi))
    tail_b = jnp.exp(lgb * pi)
    head_f = jnp.exp(lgf * (pi + 1.0))
    head_b = jnp.exp(lgb * (CHUNK - pi))
    dif = pi - pj
    decay = jnp.where(dif >= 0, jnp.exp(lgf * jnp.maximum(dif, 0.0)), jnp.exp(lgb * jnp.maximum(-dif, 0.0)))
    zeros = jnp.zeros((CHUNK, CHUNK), F32)
    gchunk_f = jnp.exp(zeros + lgf * CHUNK)
    gchunk_b = jnp.exp(zeros + lgb * CHUNK)
    gain = gain_ref[...]

    def kv_of(k, v):
        kf = k.astype(F32)
        kk = jnp.concatenate([(kf * tail_f).astype(BF16), (kf * tail_b).astype(BF16)], axis=1)
        return lax.dot_general(kk, v, (((0,), (0,)), ((), ())), preferred_element_type=F32)

    for c in range(N_CHUNK_CTX):
        kv_scr[c] = kv_of(kc[c * CHUNK:(c + 1) * CHUNK, :], vc[c * CHUNK:(c + 1) * CHUNK, :])

    def kv_body(c, carry):
        r = pl.multiple_of(c * CHUNK, CHUNK)
        kv_scr[N_CHUNK_CTX + c] = kv_of(kl[pl.ds(r, CHUNK), :], vl[pl.ds(r, CHUNK), :])
        return carry

    lax.fori_loop(0, N_CHUNK_LAT, kv_body, 0, unroll=RET_UNROLL)

    def fwd_body(c, s):
        st_scr[c, 0:HEAD_DIM, :] = s.astype(BF16)
        return gchunk_f * s + kv_scr[c, 0:HEAD_DIM, :]

    lax.fori_loop(0, N_CHUNK, fwd_body, zeros)

    def bwd_step(c, s):
        st_scr[c, HEAD_DIM:2 * HEAD_DIM, :] = s.astype(BF16)
        return gchunk_b * s + kv_scr[c, HEAD_DIM:2 * HEAD_DIM, :]

    s = zeros
    for c in reversed(range(N_CHUNK_CTX)):
        s = bwd_step(c, s)
    lax.fori_loop(0, N_CHUNK_LAT, lambda t, s: bwd_step(N_CHUNK - 1 - t, s), s)

    def out_of(q, k, v, g, st):
        sc = lax.dot_general(q, k, (((1,), (1,)), ((), ())), preferred_element_type=F32)
        intra = jnp.dot((sc * decay).astype(BF16), v, preferred_element_type=F32)
        qf = q.astype(F32)
        qq = jnp.concatenate([(qf * head_f).astype(BF16), (qf * head_b).astype(BF16)], axis=1)
        o = intra + jnp.dot(qq, st, preferred_element_type=F32)
        mu = jnp.mean(o, axis=-1, keepdims=True)
        var = jnp.mean(jnp.square(o - mu), axis=-1, keepdims=True)
        on = (o - mu) * lax.rsqrt(var + GN_EPS)
        return (on * gain * _silu(g.astype(F32))).astype(BF16)

    for c in range(N_CHUNK_CTX):
        sl = slice(c * CHUNK, (c + 1) * CHUNK)
        oc[sl, :] = out_of(qc[sl, :], kc[sl, :], vc[sl, :], gc[sl, :], st_scr[c])

    def out_body(c, carry):
        r = pl.multiple_of(c * CHUNK, CHUNK)
        sl = pl.ds(r, CHUNK)
        ol[sl, :] = out_of(ql[sl, :], kl[sl, :], vl[sl, :], gl[sl, :], st_scr[N_CHUNK_CTX + c])
        return carry

    lax.fori_loop(0, N_CHUNK_LAT, out_body, 0, unroll=RET_UNROLL)


def _retention(qkvgp, log_gamma, gn_g):
    def lat(part):
        return pl.BlockSpec((SEQ, HEAD_DIM), lambda b, h: (b, part * RET_HEADS + h))

    def ctx(part):
        return pl.BlockSpec((CTX_LEN, HEAD_DIM), lambda b, h: (N_LAT // CTX_LEN + b, part * RET_HEADS + h))

    return pl.pallas_call(
        _ret_kernel,
        out_shape=(jax.ShapeDtypeStruct((N_LAT, RET_WIDTH), BF16),
                   jax.ShapeDtypeStruct((N_CTX, RET_WIDTH), BF16)),
        grid=(BATCH, RET_HEADS),
        in_specs=[pl.BlockSpec(memory_space=pltpu.SMEM),
                  lat(0), lat(1), lat(2), lat(3), ctx(0), ctx(1), ctx(2), ctx(3),
                  pl.BlockSpec((1, HEAD_DIM), lambda b, h: (0, h))],
        out_specs=(pl.BlockSpec((SEQ, HEAD_DIM), lambda b, h: (b, h)),
                   pl.BlockSpec((CTX_LEN, HEAD_DIM), lambda b, h: (b, h))),
        scratch_shapes=[pltpu.VMEM((N_CHUNK, 2 * HEAD_DIM, HEAD_DIM), F32),
                        pltpu.VMEM((N_CHUNK, 2 * HEAD_DIM, HEAD_DIM), BF16)],
        compiler_params=pltpu.CompilerParams(vmem_limit_bytes=VMEM_LIMIT),
        name="retention",
    )(log_gamma, *([qkvgp] * 8), gn_g)


POOL_BLK = 256


def _split_dot(m, x):
    hi = x.astype(BF16)
    lo = (x - hi.astype(F32)).astype(BF16)
    return jnp.dot(m, hi, preferred_element_type=F32) + jnp.dot(m, lo, preferred_element_type=F32)


def _window_count(idx, w, n):
    return (jnp.minimum(idx - w // 2 + w, n) - jnp.maximum(idx - w // 2, 0)).astype(F32)


def _pool_kernel(pl_ref, pc_ref, w_ref, scale_ref, ol, oc):
    ti = lax.broadcasted_iota(jnp.int32, (POOL_BLK, POOL_BLK), 0)
    tj = lax.broadcasted_iota(jnp.int32, (POOL_BLK, POOL_BLK), 1)
    tok = lax.broadcasted_iota(jnp.int32, (SEQ, POOL_CH), 0)
    tok_c = lax.broadcasted_iota(jnp.int32, (CTX_LEN, POOL_CH), 0)
    for gi, w in enumerate(POOL_WINDOWS):
        cols = slice(gi * POOL_CH, (gi + 1) * POOL_CH)
        lo_off, hi_off = -(w // 2), w - 1 - w // 2
        wmat = w_ref[gi]
        scale = scale_ref[:, cols]

        u = pl_ref[:, cols].astype(F32)
        pad = jnp.zeros((w // 2 * GRID_W, POOL_CH), F32)
        s = jnp.concatenate([pad, u, pad], axis=0)
        span = 1
        while span < w:
            n = s.shape[0] - span * GRID_W
            s = s[:n] + s[span * GRID_W:]
            span *= 2
        row_mean = s[:SEQ] / _window_count(tok >> GRID_SHIFT, w, GRID_W)
        ci, cj = ti & (GRID_W - 1), tj & (GRID_W - 1)
        band = ((ti >> GRID_SHIFT == tj >> GRID_SHIFT) & (cj >= ci + lo_off) & (cj <= ci + hi_off))
        band = jnp.where(band, 1.0, 0.0).astype(BF16)
        col_cnt = _window_count(tok & (GRID_W - 1), w, GRID_W)
        for blk in range(SEQ // POOL_BLK):
            rs = slice(blk * POOL_BLK, (blk + 1) * POOL_BLK)
            m = _split_dot(band, row_mean[rs]) / col_cnt[rs]
            d = (m - u[rs]).astype(BF16)
            y = jnp.dot(d, wmat, preferred_element_type=F32) * scale
            ol[rs, cols] = y.astype(BF16)

        uc = pc_ref[:, cols].astype(F32)
        band_c = jnp.where((tj >= ti + lo_off) & (tj <= ti + hi_off), 1.0, 0.0).astype(BF16)
        mc = _split_dot(band_c, uc) / _window_count(tok_c, w, CTX_LEN)
        dc = (mc - uc).astype(BF16)
        oc[:, cols] = (jnp.dot(dc, wmat, preferred_element_type=F32) * scale).astype(BF16)


def _pool(qkvgp, pool_w, pool_scale):
    pcol = 4 * RET_WIDTH // POOL_WIDTH
    return pl.pallas_call(
        _pool_kernel,
        out_shape=(jax.ShapeDtypeStruct((N_LAT, POOL_WIDTH), BF16),
                   jax.ShapeDtypeStruct((N_CTX, POOL_WIDTH), BF16)),
        grid=(BATCH,),
        in_specs=[pl.BlockSpec((SEQ, POOL_WIDTH), lambda b: (b, pcol)),
                  pl.BlockSpec((CTX_LEN, POOL_WIDTH), lambda b: (N_LAT // CTX_LEN + b, pcol)),
                  pl.BlockSpec((len(POOL_WINDOWS), POOL_CH, POOL_CH), lambda b: (0, 0, 0)),
                  pl.BlockSpec((1, POOL_WIDTH), lambda b: (0, 0))],
        out_specs=(pl.BlockSpec((SEQ, POOL_WIDTH), lambda b: (b, 0)),
                   pl.BlockSpec((CTX_LEN, POOL_WIDTH), lambda b: (b, 0))),
        compiler_params=pltpu.CompilerParams(vmem_limit_bytes=VMEM_LIMIT),
        name="pool",
    )(qkvgp, qkvgp, pool_w, pool_scale)


def _mix_out(i, rl, rc, pl_, pc, w_ref, x_ref, gt_ref):
    is_lat = i < LAT_TILES
    ret = jnp.where(is_lat, rl[...], rc[...])
    pool = jnp.where(is_lat, pl_[...], pc[...])
    y = (jnp.dot(ret, w_ref[0:RET_WIDTH, :], preferred_element_type=F32)
         + jnp.dot(pool, w_ref[RET_WIDTH:, :], preferred_element_type=F32))
    return x_ref[...] + gt_ref[...] * y


def _outproj_kernel(rl, rc, pl_, pc, w_ref, x_ref, gt_ref, g_ref, sh_ref, sc_ref, xo_ref, h_ref):
    x = _mix_out(pl.program_id(0), rl, rc, pl_, pc, w_ref, x_ref, gt_ref)
    xo_ref[...] = x
    h_ref[...] = _rms_mod(x, g_ref[...], sh_ref[...], sc_ref[...]).astype(BF16)


def _outproj_router_kernel(rl, rc, pl_, pc, w_ref, x_ref, gt_ref, g_ref, sh_ref, sc_ref, wr_ref,
                           xo_ref, h_ref, route_ref, cnt_ref, carry_scr):
    i = pl.program_id(0)
    x = _mix_out(i, rl, rc, pl_, pc, w_ref, x_ref, gt_ref)
    xo_ref[...] = x
    h = _rms_mod(x, g_ref[...], sh_ref[...], sc_ref[...])
    h_ref[...] = h
    logits = jnp.dot(h.astype(BF16), wr_ref[...], preferred_element_type=F32)
    lane = lax.broadcasted_iota(jnp.int32, logits.shape, 1)
    neg = jnp.float32(-jnp.inf)
    logits = jnp.where(lane < N_EXPERTS, logits, neg)
    m1 = jnp.max(logits, axis=-1, keepdims=True)
    i1 = jnp.min(jnp.where(logits == m1, lane, 128), axis=-1, keepdims=True)
    rest = jnp.where(lane == i1, neg, logits)
    m2 = jnp.max(rest, axis=-1, keepdims=True)
    i2 = jnp.min(jnp.where(rest == m2, lane, 128), axis=-1, keepdims=True)
    e2 = jnp.exp(m2 - m1)
    den = 1.0 + e2

    @pl.when(i == 0)
    def _():
        carry_scr[...] = jnp.zeros_like(carry_scr)

    sel1, sel2 = lane == i1, lane == i2
    picks = jnp.where(sel1 | sel2, 1.0, 0.0)
    ti = lax.broadcasted_iota(jnp.int32, (TM, TM), 0)
    tj = lax.broadcasted_iota(jnp.int32, (TM, TM), 1)
    earlier = jnp.where(tj < ti, 1.0, 0.0).astype(BF16)
    before = jnp.dot(earlier, picks.astype(BF16), preferred_element_type=F32) + carry_scr[...]
    r1 = jnp.sum(jnp.where(sel1, before, 0.0), axis=-1, keepdims=True)
    r2 = jnp.sum(jnp.where(sel2, before, 0.0), axis=-1, keepdims=True)
    carry = carry_scr[...] + jnp.sum(picks, axis=0, keepdims=True)
    carry_scr[...] = carry
    cnt_ref[...] = jnp.broadcast_to(carry, cnt_ref.shape)

    cols = (i1.astype(F32), i2.astype(F32), r1, r2, 1.0 / den, e2 / den)
    route = jnp.zeros(logits.shape, F32)
    for k, v in enumerate(cols):
        route = jnp.where(lane == k, v, route)
    route_ref[...] = route


def _outproj(ret_l, ret_c, pool_l, pool_c, w_out, x_all, mod5, norm_g, layer, n_tiles, router_w=None):
    def lat_spec():
        return pl.BlockSpec((TM, RET_WIDTH), lambda i: (jnp.minimum(i, LAT_TILES - 1), 0))

    def ctx_spec():
        return pl.BlockSpec((TM, RET_WIDTH), lambda i: (jnp.maximum(i - LAT_TILES, 0), 0))

    row = pl.BlockSpec((TM, D_MODEL), lambda i: (i, 0))
    in_specs = [lat_spec(), ctx_spec(), lat_spec(), ctx_spec(),
                pl.BlockSpec((D_MODEL, D_MODEL), lambda i: (0, 0)),
                row, _mod_spec(layer, 2),
                pl.BlockSpec((1, D_MODEL), lambda i: (0, 0)),
                _mod_spec(layer, 3), _mod_spec(layer, 4)]
    args = [ret_l, ret_c, pool_l, pool_c, w_out, x_all, mod5, norm_g, mod5, mod5]
    n_rows = n_tiles * TM
    out_shape = [jax.ShapeDtypeStruct((n_rows, D_MODEL), F32), jax.ShapeDtypeStruct((n_rows, D_MODEL), BF16)]
    out_specs = [row, row]
    scratch = []
    if router_w is None:
        body = _outproj_kernel
    else:
        body = _outproj_router_kernel
        in_specs.append(pl.BlockSpec((D_MODEL, 128), lambda i: (0, 0)))
        args.append(router_w)
        out_shape[1] = jax.ShapeDtypeStruct((n_rows, D_MODEL), F32)
        out_shape += [jax.ShapeDtypeStruct((n_rows, 128), F32), jax.ShapeDtypeStruct((8, 128), F32)]
        out_specs += [pl.BlockSpec((TM, 128), lambda i: (i, 0)), pl.BlockSpec((8, 128), lambda i: (0, 0))]
        scratch = [pltpu.VMEM((1, 128), F32)]
    return pl.pallas_call(
        body, out_shape=tuple(out_shape), grid=(n_tiles,), in_specs=in_specs, out_specs=tuple(out_specs),
        scratch_shapes=scratch,
        compiler_params=pltpu.CompilerParams(vmem_limit_bytes=VMEM_LIMIT),
        name="outproj",
    )(*args)


def _swiglu(h, w13_ref, w2_ref, act_scr):
    for j in range(D_FF // FF_CHUNK):
        u = jnp.dot(h, w13_ref[0, :, j * FF_CHUNK:(j + 1) * FF_CHUNK], preferred_element_type=F32)
        g = jnp.dot(h, w13_ref[0, :, D_FF + j * FF_CHUNK:D_FF + (j + 1) * FF_CHUNK], preferred_element_type=F32)
        act_scr[:, j * FF_CHUNK:(j + 1) * FF_CHUNK] = (_silu(g) * u).astype(BF16)
    return jnp.dot(act_scr[...], w2_ref[0], preferred_element_type=F32)


def _ffn_kernel(h_ref, x_ref, gt_ref, w13_ref, w2_ref, o_ref, act_scr):
    o_ref[...] = x_ref[...] + gt_ref[...] * _swiglu(h_ref[...], w13_ref, w2_ref, act_scr)


def _ffn(h2, x_all, mod5, w13, w2, layer, n_tiles):
    row = pl.BlockSpec((TM, D_MODEL), lambda i: (i, 0))
    return pl.pallas_call(
        _ffn_kernel,
        out_shape=jax.ShapeDtypeStruct((n_tiles * TM, D_MODEL), F32),
        grid=(n_tiles,),
        in_specs=[row, row, _mod_spec(layer, 5),
                  pl.BlockSpec((1, D_MODEL, 2 * D_FF), lambda i: (0, 0, 0)),
                  pl.BlockSpec((1, D_FF, D_MODEL), lambda i: (0, 0, 0))],
        out_specs=row,
        scratch_shapes=[pltpu.VMEM((TM, D_FF), BF16)],
        compiler_params=pltpu.CompilerParams(vmem_limit_bytes=VMEM_LIMIT),
        name="ffn",
    )(h2, x_all, mod5, w13, w2)


def _moe_plan(counts, n_steps):
    cnt = counts.astype(jnp.int32)
    end = jnp.cumsum(cnt)
    start = end - cnt
    first = start // TM
    visits = jnp.where(cnt > 0, (end - 1) // TM - first + 1, 0)
    visit_end = jnp.cumsum(visits)
    visit_start = visit_end - visits
    total = visit_end[-1]
    step = jnp.minimum(jnp.arange(n_steps, dtype=jnp.int32), total - 1)
    eid = jnp.minimum(jnp.sum(step[:, None] >= visit_end[None, :], axis=1), N_EXPERTS - 1).astype(jnp.int32)
    tile = first[eid] + step - visit_start[eid]
    lo = jnp.clip(start[eid] - tile * TM, 0, TM)
    hi = jnp.clip(end[eid] - tile * TM, 0, TM)
    hi = jnp.where(jnp.arange(n_steps) < total, hi, lo)
    return start, tile.astype(jnp.int32), eid, lo.astype(jnp.int32), hi.astype(jnp.int32)


def _row_copy_wait(src, dst, sem):
    pltpu.make_async_copy(src, dst, sem).wait()


def _load_positions(pos_hbm, pos_smem, sem, i):
    cp = pltpu.make_async_copy(pos_hbm.at[i], pos_smem, sem)
    cp.start()
    cp.wait()


def _dispatch_kernel(pos_hbm, h_ref, xs_hbm, pos_smem, sem_pos, sem):
    _load_positions(pos_hbm, pos_smem, sem_pos, pl.program_id(0))

    def issue(r, carry):
        for k in range(2):
            p = pos_smem[0, k * TM + r]
            pltpu.make_async_copy(h_ref.at[pl.ds(r, 1)], xs_hbm.at[pl.ds(p, 1)], sem).start()
        return carry

    lax.fori_loop(0, TM, issue, 0, unroll=8)
    for k in range(2):
        _row_copy_wait(h_ref, xs_hbm.at[pl.ds(0, TM)], sem)


def _dispatch(pos, h, n_tiles):
    return pl.pallas_call(
        _dispatch_kernel,
        out_shape=jax.ShapeDtypeStruct((2 * n_tiles * TM, D_MODEL), F32),
        grid=(n_tiles,),
        in_specs=[pl.BlockSpec(memory_space=pl.ANY), pl.BlockSpec((TM, D_MODEL), lambda i: (i, 0))],
        out_specs=pl.BlockSpec(memory_space=pl.ANY),
        scratch_shapes=[pltpu.SMEM((1, 2 * TM), jnp.int32), pltpu.SemaphoreType.DMA, pltpu.SemaphoreType.DMA],
        name="dispatch",
    )(pos, h)


def _expert_ffn_kernel(tile_ref, eid_ref, lo_ref, hi_ref, x_ref, w13_ref, w2_ref, o_ref, act_scr):
    s = pl.program_id(0)
    lo, hi = lo_ref[s], hi_ref[s]

    @pl.when(hi > lo)
    def _():
        row = lax.broadcasted_iota(jnp.int32, (TM, D_MODEL), 0)
        h = jnp.where((row >= lo) & (row < hi), x_ref[...], 0.0).astype(BF16)
        y = _swiglu(h, w13_ref, w2_ref, act_scr)

        @pl.when(lo == 0)
        def _():
            o_ref[...] = y

        @pl.when(lo > 0)
        def _():
            o_ref[...] += y


def _expert_ffn(plan, xs, w13, w2, n_steps):
    _, tile, eid, lo, hi = plan
    row = pl.BlockSpec((TM, D_MODEL), lambda s, tile, eid, lo, hi: (tile[s], 0))
    return pl.pallas_call(
        _expert_ffn_kernel,
        out_shape=jax.ShapeDtypeStruct(xs.shape, F32),
        grid_spec=pltpu.PrefetchScalarGridSpec(
            num_scalar_prefetch=4,
            grid=(n_steps,),
            in_specs=[row,
                      pl.BlockSpec((1, D_MODEL, 2 * D_FF), lambda s, tile, eid, lo, hi: (eid[s], 0, 0)),
                      pl.BlockSpec((1, D_FF, D_MODEL), lambda s, tile, eid, lo, hi: (eid[s], 0, 0))],
            out_specs=row,
            scratch_shapes=[pltpu.VMEM((TM, D_FF), BF16)]),
        compiler_params=pltpu.CompilerParams(vmem_limit_bytes=VMEM_LIMIT),
        name="expert_ffn",
    )(tile, eid, lo, hi, xs, w13, w2)


def _combine_kernel(pos_hbm, y_hbm, route_ref, x_ref, gt_ref, o_ref, pos_smem, ybuf, sem_pos, sem):
    i = pl.program_id(0)
    _load_positions(pos_hbm, pos_smem, sem_pos, i)

    def issue(r, carry):
        for k in range(2):
            p = pos_smem[0, k * TM + r]
            pltpu.make_async_copy(y_hbm.at[pl.ds(p, 1)], ybuf.at[k, pl.ds(r, 1)], sem).start()
        return carry

    lax.fori_loop(0, TM, issue, 0, unroll=8)
    for k in range(2):
        _row_copy_wait(y_hbm.at[pl.ds(0, TM)], ybuf.at[k], sem)
    w1 = route_ref[:, 4:5]
    w2 = route_ref[:, 5:6]
    o_ref[...] = x_ref[...] + gt_ref[...] * (w1 * ybuf[0] + w2 * ybuf[1])


def _combine(pos, y, route, x_all, mod5, layer, n_tiles):
    row = pl.BlockSpec((TM, D_MODEL), lambda i: (i, 0))
    return pl.pallas_call(
        _combine_kernel,
        out_shape=jax.ShapeDtypeStruct((n_tiles * TM, D_MODEL), F32),
        grid=(n_tiles,),
        in_specs=[pl.BlockSpec(memory_space=pl.ANY), pl.BlockSpec(memory_space=pl.ANY),
                  pl.BlockSpec((TM, 128), lambda i: (i, 0)), row, _mod_spec(layer, 5)],
        out_specs=row,
        scratch_shapes=[pltpu.SMEM((1, 2 * TM), jnp.int32), pltpu.VMEM((2, TM, D_MODEL), F32),
                        pltpu.SemaphoreType.DMA, pltpu.SemaphoreType.DMA],
        compiler_params=pltpu.CompilerParams(vmem_limit_bytes=VMEM_LIMIT),
        name="combine",
    )(pos, y, route, x_all, mod5)


def _moe(h, route, counts, x_all, mod5, w13, w2, layer, n_tiles):
    n_steps = 2 * n_tiles + N_EXPERTS
    plan = _moe_plan(counts[0, :N_EXPERTS], n_steps)
    meta = route[:, :4].astype(jnp.int32)
    pos = plan[0][meta[:, 0:2]] + meta[:, 2:4]
    pos = pos.reshape(n_tiles, TM, 2).transpose(0, 2, 1).reshape(n_tiles, 1, 2 * TM)
    xs = _dispatch(pos, h, n_tiles)
    ys = _expert_ffn(plan, xs, w13, w2, n_steps)
    return _combine(pos, ys, route, x_all, mod5, layer, n_tiles)


def _final_kernel(x_ref, g_ref, o_ref):
    x = x_ref[...]
    ms = jnp.mean(x * x, axis=-1, keepdims=True)
    o_ref[...] = x * lax.rsqrt(ms + NORM_EPS) * g_ref[...]


def _final_norm(x_all, gain):
    return pl.pallas_call(
        _final_kernel,
        out_shape=jax.ShapeDtypeStruct((N_LAT, D_MODEL), F32),
        grid=(LAT_TILES,),
        in_specs=[pl.BlockSpec((TM, D_MODEL), lambda i: (i, 0)),
                  pl.BlockSpec((1, D_MODEL), lambda i: (0, 0))],
        out_specs=pl.BlockSpec((TM, D_MODEL), lambda i: (i, 0)),
        name="final_norm",
    )(x_all, gain)


def kernel(x, c, ctx, c_ctx, w_ada, b_ada, norm1_g, norm2_g, w_in, ret_decay_logit, ret_gn_g, pool_w, pool_scale,
           w_out, ffn_w13, ffn_w2, router_w, moe_w13, moe_w2, final_norm_g):
    x_all = jnp.concatenate([x.reshape(N_LAT, D_MODEL), ctx.reshape(N_CTX, D_MODEL)], axis=0)
    c_all = jnp.concatenate([c, c_ctx[None, :], jnp.zeros((MOD_ROWS - BATCH - 1, D_MODEL), F32)], axis=0)
    mod5 = _ada_mod(c_all, w_ada, b_ada).reshape(DEPTH, MOD_ROWS, 6, 1, D_MODEL)
    tables = _rope_tables()
    log_gamma = jax.nn.log_sigmoid(ret_decay_logit.astype(F32))
    router_pad = jnp.pad(router_w, ((0, 0), (0, 0), (0, 128 - N_EXPERTS))).astype(BF16)

    for l in range(DEPTH):
        last = l == DEPTH - 1
        n_tiles = LAT_TILES if last else LAT_TILES + CTX_TILES
        qkvgp = _inproj(x_all, norm1_g[l][None, :], mod5, w_in[l].astype(BF16), tables, l,
                        LAT_TILES + CTX_TILES)
        ret_l, ret_c = _retention(qkvgp, log_gamma[l], ret_gn_g[l][None, :])
        pool_l, pool_c = _pool(qkvgp, pool_w[l].astype(BF16), pool_scale[l][None, :])
        if l % 2 == 0:
            x_all, h2 = _outproj(ret_l, ret_c, pool_l, pool_c, w_out[l].astype(BF16), x_all, mod5,
                                 norm2_g[l][None, :], l, n_tiles)
            i = l // 2
            x_all = _ffn(h2, x_all, mod5, ffn_w13[i][None].astype(BF16), ffn_w2[i][None].astype(BF16), l, n_tiles)
        else:
            i = l // 2
            x_all, h2, route, counts = _outproj(ret_l, ret_c, pool_l, pool_c, w_out[l].astype(BF16), x_all, mod5,
                                                norm2_g[l][None, :], l, n_tiles, router_w=router_pad[i])
            x_all = _moe(h2, route, counts, x_all, mod5, moe_w13[i].astype(BF16), moe_w2[i].astype(BF16),
                         l, n_tiles)
    out = _final_norm(x_all, final_norm_g[None, :])
    return out.reshape(BATCH, SEQ, D_MODEL)
```

```python
import functools

import jax
import jax.numpy as jnp
from jax import lax
from jax.experimental import pallas as pl
from jax.experimental.pallas import tpu as pltpu

F32 = jnp.float32
BF16 = jnp.bfloat16

D_MODEL = 1024
BATCH = 8
SEQ = 4096
DEPTH = 4
GRID_W = 64
GRID_SHIFT = 6
CTX_LEN = 256
RET_WIDTH = 512
POOL_WIDTH = 512
RET_HEADS = 4
HEAD_DIM = 128
CHUNK = 128
ROPE_BASE = 10000.0
POOL_WINDOWS = (2, 4, 8, 16)
POOL_CH = 128
IN_COLS = 4 * RET_WIDTH + POOL_WIDTH
D_FF = 2816
N_EXPERTS = 8
NORM_EPS = 1e-6
GN_EPS = 1e-5

N_LAT = BATCH * SEQ
N_CTX = BATCH * CTX_LEN
N_ROWS = N_LAT + N_CTX
MOD_ROWS = 16
CTX_MOD_ROW = BATCH

TM = 512
LAT_TILES = N_LAT // TM
CTX_TILES = N_CTX // TM
TILES_PER_BATCH = SEQ // TM
FF_CHUNK = 256
VMEM_LIMIT = 56 * 1024 * 1024


def _mod_row(i):
    return jnp.where(i < LAT_TILES, i // TILES_PER_BATCH, CTX_MOD_ROW)


def _mod_spec(layer, part):
    return pl.BlockSpec((None, None, None, 1, D_MODEL),
                        lambda i, *_: (layer, _mod_row(i), part, 0, 0))


def _silu(v):
    return v * jax.nn.sigmoid(v)


def _x_specs(ctx_block_offset):
    return [pl.BlockSpec((TM, D_MODEL), lambda i, *_: (jnp.minimum(i, LAT_TILES - 1), 0)),
            pl.BlockSpec((TM, D_MODEL), lambda i, *_: (ctx_block_offset + jnp.maximum(i - LAT_TILES, 0), 0))]


def _x_tile(xl_ref, xc_ref):
    return jnp.where(pl.program_id(0) < LAT_TILES, xl_ref[...], xc_ref[...])


def _rms_mod(x, gain, shift, scale):
    ms = jnp.mean(x * x, axis=-1, keepdims=True)
    y = x * lax.rsqrt(ms + NORM_EPS) * gain
    return y * (1.0 + scale) + shift


def _ada_kernel(c_ref, w_ref, b_ref, o_ref):
    s = _silu(c_ref[...])
    o_ref[0] = jnp.dot(s.astype(BF16), w_ref[0].astype(BF16), preferred_element_type=F32) + b_ref[0]


def _ada_mod(c_all, w_ada, b_ada):
    tn = 1536
    return pl.pallas_call(
        _ada_kernel,
        out_shape=jax.ShapeDtypeStruct((DEPTH, MOD_ROWS, 6 * D_MODEL), F32),
        grid=(DEPTH, 6 * D_MODEL // tn),
        in_specs=[pl.BlockSpec((MOD_ROWS, D_MODEL), lambda l, n: (0, 0)),
                  pl.BlockSpec((1, D_MODEL, tn), lambda l, n: (l, 0, n)),
                  pl.BlockSpec((1, 1, tn), lambda l, n: (l, 0, n))],
        out_specs=pl.BlockSpec((1, MOD_ROWS, tn), lambda l, n: (l, 0, n)),
        compiler_params=pltpu.CompilerParams(vmem_limit_bytes=VMEM_LIMIT),
        name="ada_mod",
    )(c_all, w_ada, b_ada.reshape(DEPTH, 1, 6 * D_MODEL))


def _inproj_kernel(xl_ref, xc_ref, g_ref, sh_ref, sc_ref, w_ref, cos_ref, sa_ref, sb_ref, o_ref):
    h = _rms_mod(_x_tile(xl_ref, xc_ref), g_ref[...], sh_ref[...], sc_ref[...]).astype(BF16)
    cos, sa, sb = cos_ref[...], sa_ref[...], sb_ref[...]
    k_scale = HEAD_DIM ** -0.5
    for part, mul in ((0, 1.0), (1, k_scale)):
        z = jnp.dot(h, w_ref[:, part * RET_WIDTH:(part + 1) * RET_WIDTH], preferred_element_type=F32)
        for hh in range(RET_HEADS):
            t = z[:, hh * HEAD_DIM:(hh + 1) * HEAD_DIM]
            r = t * cos + pltpu.roll(t, 96, axis=1) * sa + pltpu.roll(t, 32, axis=1) * sb
            if mul != 1.0:
                r = r * mul
            col = part * RET_WIDTH + hh * HEAD_DIM
            o_ref[:, col:col + HEAD_DIM] = r.astype(BF16)
    z = jnp.dot(h, w_ref[:, 2 * RET_WIDTH:], preferred_element_type=F32)
    o_ref[:, 2 * RET_WIDTH:] = z.astype(BF16)


def _rope_tables():
    half = HEAD_DIM // 2
    inv_freq = ROPE_BASE ** (-jnp.arange(0, half, 2, dtype=F32) / half)
    t = jnp.arange(SEQ)
    rows, cols = (t // GRID_W).astype(F32), (t % GRID_W).astype(F32)
    ang_r = rows[:, None] * inv_freq[None, :]
    ang_c = cols[:, None] * inv_freq[None, :]
    zero = jnp.zeros_like(ang_r)
    cos = jnp.concatenate([jnp.cos(ang_r), jnp.cos(ang_r), jnp.cos(ang_c), jnp.cos(ang_c)], axis=1)
    sa = jnp.concatenate([-jnp.sin(ang_r), zero, -jnp.sin(ang_c), zero], axis=1)
    sb = jnp.concatenate([zero, jnp.sin(ang_r), zero, jnp.sin(ang_c)], axis=1)
    ident = jnp.ones((TM, HEAD_DIM), F32)
    none = jnp.zeros((TM, HEAD_DIM), F32)
    return (jnp.concatenate([cos, ident]), jnp.concatenate([sa, none]), jnp.concatenate([sb, none]))


def _inproj(x_lat, x_ctx, ctx_off, norm_g, mod5, w_in, tables, layer, n_tiles):
    tab_spec = pl.BlockSpec((TM, HEAD_DIM),
                            lambda i: (jnp.where(i < LAT_TILES, i % TILES_PER_BATCH, TILES_PER_BATCH), 0))
    return pl.pallas_call(
        _inproj_kernel,
        out_shape=jax.ShapeDtypeStruct((N_ROWS, IN_COLS), BF16),
        grid=(n_tiles,),
        in_specs=_x_specs(ctx_off) + [
            pl.BlockSpec((1, D_MODEL), lambda i: (0, 0)),
            _mod_spec(layer, 0), _mod_spec(layer, 1),
            pl.BlockSpec((D_MODEL, IN_COLS), lambda i: (0, 0)),
            tab_spec, tab_spec, tab_spec],
        out_specs=pl.BlockSpec((TM, IN_COLS), lambda i: (i, 0)),
        compiler_params=pltpu.CompilerParams(vmem_limit_bytes=VMEM_LIMIT),
        name="inproj",
    )(x_lat, x_ctx, norm_g, mod5, mod5, w_in, *tables)


N_CHUNK_LAT = SEQ // CHUNK
N_CHUNK_CTX = CTX_LEN // CHUNK
N_CHUNK = N_CHUNK_LAT + N_CHUNK_CTX
RET_UNROLL = 8


def _ret_kernel(lg_ref, ql, kl, vl, gl, qc, kc, vc, gc, gain_ref, ol, oc, kv_scr, st_scr):
    hd = pl.program_id(1)
    lgf = lg_ref[0, hd]
    lgb = lg_ref[1, hd]
    pi = lax.broadcasted_iota(jnp.int32, (CHUNK, CHUNK), 0).astype(F32)
    pj = lax.broadcasted_iota(jnp.int32, (CHUNK, CHUNK), 1).astype(F32)
    tail_f = jnp.exp(lgf * (CHUNK - 1.0 - pi))
    tail_b = jnp.exp(lgb * pi)
    head_f = jnp.exp(lgf * (pi + 1.0))
    head_b = jnp.exp(lgb * (CHUNK - pi))
    dif = pi - pj
    decay = jnp.where(dif >= 0, jnp.exp(lgf * jnp.maximum(dif, 0.0)), jnp.exp(lgb * jnp.maximum(-dif, 0.0)))
    zeros = jnp.zeros((CHUNK, CHUNK), F32)
    gchunk_f = jnp.exp(zeros + lgf * CHUNK)
    gchunk_b = jnp.exp(zeros + lgb * CHUNK)
    gain = gain_ref[...]

    def kv_of(k, v):
        kf = k.astype(F32)
        kk = jnp.concatenate([(kf * tail_f).astype(BF16), (kf * tail_b).astype(BF16)], axis=1)
        return lax.dot_general(kk, v, (((0,), (0,)), ((), ())), preferred_element_type=F32)

    for c in range(N_CHUNK_CTX):
        kv_scr[c] = kv_of(kc[c * CHUNK:(c + 1) * CHUNK, :], vc[c * CHUNK:(c + 1) * CHUNK, :])

    def kv_body(c, carry):
        r = pl.multiple_of(c * CHUNK, CHUNK)
        kv_scr[N_CHUNK_CTX + c] = kv_of(kl[pl.ds(r, CHUNK), :], vl[pl.ds(r, CHUNK), :])
        return carry

    lax.fori_loop(0, N_CHUNK_LAT, kv_body, 0, unroll=RET_UNROLL)

    def fwd_body(c, s):
        st_scr[c, 0:HEAD_DIM, :] = s.astype(BF16)
        return gchunk_f * s + kv_scr[c, 0:HEAD_DIM, :]

    lax.fori_loop(0, N_CHUNK, fwd_body, zeros)

    def bwd_step(c, s):
        st_scr[c, HEAD_DIM:2 * HEAD_DIM, :] = s.astype(BF16)
        return gchunk_b * s + kv_scr[c, HEAD_DIM:2 * HEAD_DIM, :]

    s = zeros
    for c in reversed(range(N_CHUNK_CTX)):
        s = bwd_step(c, s)
    lax.fori_loop(0, N_CHUNK_LAT, lambda t, s: bwd_step(N_CHUNK - 1 - t, s), s)

    def out_of(q, k, v, g, st):
        sc = lax.dot_general(q, k, (((1,), (1,)), ((), ())), preferred_element_type=F32)
        intra = jnp.dot((sc * decay).astype(BF16), v, preferred_element_type=F32)
        qf = q.astype(F32)
        qq = jnp.concatenate([(qf * head_f).astype(BF16), (qf * head_b).astype(BF16)], axis=1)
        o = intra + jnp.dot(qq, st, preferred_element_type=F32)
        mu = jnp.mean(o, axis=-1, keepdims=True)
        var = jnp.mean(jnp.square(o - mu), axis=-1, keepdims=True)
        on = (o - mu) * lax.rsqrt(var + GN_EPS)
        return (on * gain * _silu(g.astype(F32))).astype(BF16)

    for c in range(N_CHUNK_CTX):
        sl = slice(c * CHUNK, (c + 1) * CHUNK)
        oc[sl, :] = out_of(qc[sl, :], kc[sl, :], vc[sl, :], gc[sl, :], st_scr[c])

    def out_body(c, carry):
        r = pl.multiple_of(c * CHUNK, CHUNK)
        sl = pl.ds(r, CHUNK)
        ol[sl, :] = out_of(ql[sl, :], kl[sl, :], vl[sl, :], gl[sl, :], st_scr[N_CHUNK_CTX + c])
        return carry

    lax.fori_loop(0, N_CHUNK_LAT, out_body, 0, unroll=RET_UNROLL)


def _retention(qkvgp, log_gamma, gn_g):
    def lat(part):
        return pl.BlockSpec((SEQ, HEAD_DIM), lambda b, h: (b, part * RET_HEADS + h))

    def ctx(part):
        return pl.BlockSpec((CTX_LEN, HEAD_DIM), lambda b, h: (N_LAT // CTX_LEN + b, part * RET_HEADS + h))

    return pl.pallas_call(
        _ret_kernel,
        out_shape=(jax.ShapeDtypeStruct((N_LAT, RET_WIDTH), BF16),
                   jax.ShapeDtypeStruct((N_CTX, RET_WIDTH), BF16)),
        grid=(BATCH, RET_HEADS),
        in_specs=[pl.BlockSpec(memory_space=pltpu.SMEM),
                  lat(0), lat(1), lat(2), lat(3), ctx(0), ctx(1), ctx(2), ctx(3),
                  pl.BlockSpec((1, HEAD_DIM), lambda b, h: (0, h))],
        out_specs=(pl.BlockSpec((SEQ, HEAD_DIM), lambda b, h: (b, h)),
                   pl.BlockSpec((CTX_LEN, HEAD_DIM), lambda b, h: (b, h))),
        scratch_shapes=[pltpu.VMEM((N_CHUNK, 2 * HEAD_DIM, HEAD_DIM), F32),
                        pltpu.VMEM((N_CHUNK, 2 * HEAD_DIM, HEAD_DIM), BF16)],
        compiler_params=pltpu.CompilerParams(vmem_limit_bytes=VMEM_LIMIT),
        name="retention",
    )(log_gamma, *([qkvgp] * 8), gn_g)


POOL_BLK = 256


def _split_dot(m, x):
    hi = x.astype(BF16)
    lo = (x - hi.astype(F32)).astype(BF16)
    return jnp.dot(m, hi, preferred_element_type=F32) + jnp.dot(m, lo, preferred_element_type=F32)


def _window_count(idx, w, n):
    return (jnp.minimum(idx - w // 2 + w, n) - jnp.maximum(idx - w // 2, 0)).astype(F32)


def _pool_kernel(pl_ref, pc_ref, w_ref, scale_ref, ol, oc):
    ti = lax.broadcasted_iota(jnp.int32, (POOL_BLK, POOL_BLK), 0)
    tj = lax.broadcasted_iota(jnp.int32, (POOL_BLK, POOL_BLK), 1)
    tok = lax.broadcasted_iota(jnp.int32, (SEQ, POOL_CH), 0)
    tok_c = lax.broadcasted_iota(jnp.int32, (CTX_LEN, POOL_CH), 0)
    for gi, w in enumerate(POOL_WINDOWS):
        cols = slice(gi * POOL_CH, (gi + 1) * POOL_CH)
        lo_off, hi_off = -(w // 2), w - 1 - w // 2
        wmat = w_ref[gi]
        scale = scale_ref[:, cols]

        u = pl_ref[:, cols].astype(F32)
        pad = jnp.zeros((w // 2 * GRID_W, POOL_CH), F32)
        s = jnp.concatenate([pad, u, pad], axis=0)
        span = 1
        while span < w:
            n = s.shape[0] - span * GRID_W
            s = s[:n] + s[span * GRID_W:]
            span *= 2
        row_mean = s[:SEQ] / _window_count(tok >> GRID_SHIFT, w, GRID_W)
        ci, cj = ti & (GRID_W - 1), tj & (GRID_W - 1)
        band = ((ti >> GRID_SHIFT == tj >> GRID_SHIFT) & (cj >= ci + lo_off) & (cj <= ci + hi_off))
        band = jnp.where(band, 1.0, 0.0).astype(BF16)
        col_cnt = _window_count(tok & (GRID_W - 1), w, GRID_W)
        for blk in range(SEQ // POOL_BLK):
            rs = slice(blk * POOL_BLK, (blk + 1) * POOL_BLK)
            m = _split_dot(band, row_mean[rs]) / col_cnt[rs]
            d = (m - u[rs]).astype(BF16)
            y = jnp.dot(d, wmat, preferred_element_type=F32) * scale
            ol[rs, cols] = y.astype(BF16)

        uc = pc_ref[:, cols].astype(F32)
        band_c = jnp.where((tj >= ti + lo_off) & (tj <= ti + hi_off), 1.0, 0.0).astype(BF16)
        mc = _split_dot(band_c, uc) / _window_count(tok_c, w, CTX_LEN)
        dc = (mc - uc).astype(BF16)
        oc[:, cols] = (jnp.dot(dc, wmat, preferred_element_type=F32) * scale).astype(BF16)


def _pool(qkvgp, pool_w, pool_scale):
    pcol = 4 * RET_WIDTH // POOL_WIDTH
    return pl.pallas_call(
        _pool_kernel,
        out_shape=(jax.ShapeDtypeStruct((N_LAT, POOL_WIDTH), BF16),
                   jax.ShapeDtypeStruct((N_CTX, POOL_WIDTH), BF16)),
        grid=(BATCH,),
        in_specs=[pl.BlockSpec((SEQ, POOL_WIDTH), lambda b: (b, pcol)),
                  pl.BlockSpec((CTX_LEN, POOL_WIDTH), lambda b: (N_LAT // CTX_LEN + b, pcol)),
                  pl.BlockSpec((len(POOL_WINDOWS), POOL_CH, POOL_CH), lambda b: (0, 0, 0)),
                  pl.BlockSpec((1, POOL_WIDTH), lambda b: (0, 0))],
        out_specs=(pl.BlockSpec((SEQ, POOL_WIDTH), lambda b: (b, 0)),
                   pl.BlockSpec((CTX_LEN, POOL_WIDTH), lambda b: (b, 0))),
        compiler_params=pltpu.CompilerParams(vmem_limit_bytes=VMEM_LIMIT),
        name="pool",
    )(qkvgp, qkvgp, pool_w, pool_scale)


def _mix_out(i, rl, rc, pl_, pc, w_ref, xl_ref, xc_ref, gt_ref):
    is_lat = i < LAT_TILES
    ret = jnp.where(is_lat, rl[...], rc[...])
    pool = jnp.where(is_lat, pl_[...], pc[...])
    y = (jnp.dot(ret, w_ref[0:RET_WIDTH, :], preferred_element_type=F32)
         + jnp.dot(pool, w_ref[RET_WIDTH:, :], preferred_element_type=F32))
    return _x_tile(xl_ref, xc_ref) + gt_ref[...] * y


def _outproj_kernel(rl, rc, pl_, pc, w_ref, xl_ref, xc_ref, gt_ref, g_ref, sh_ref, sc_ref, xo_ref, h_ref):
    x = _mix_out(pl.program_id(0), rl, rc, pl_, pc, w_ref, xl_ref, xc_ref, gt_ref)
    xo_ref[...] = x
    h_ref[...] = _rms_mod(x, g_ref[...], sh_ref[...], sc_ref[...]).astype(BF16)


def _outproj_router_kernel(rl, rc, pl_, pc, w_ref, xl_ref, xc_ref, gt_ref, g_ref, sh_ref, sc_ref, wr_ref,
                           xo_ref, h_ref, route_ref, cnt_ref, carry_scr):
    i = pl.program_id(0)
    x = _mix_out(i, rl, rc, pl_, pc, w_ref, xl_ref, xc_ref, gt_ref)
    xo_ref[...] = x
    h = _rms_mod(x, g_ref[...], sh_ref[...], sc_ref[...])
    h_ref[...] = h
    logits = jnp.dot(h.astype(BF16), wr_ref[...], preferred_element_type=F32)
    lane = lax.broadcasted_iota(jnp.int32, logits.shape, 1)
    neg = jnp.float32(-jnp.inf)
    logits = jnp.where(lane < N_EXPERTS, logits, neg)
    m1 = jnp.max(logits, axis=-1, keepdims=True)
    i1 = jnp.min(jnp.where(logits == m1, lane, 128), axis=-1, keepdims=True)
    rest = jnp.where(lane == i1, neg, logits)
    m2 = jnp.max(rest, axis=-1, keepdims=True)
    i2 = jnp.min(jnp.where(rest == m2, lane, 128), axis=-1, keepdims=True)
    e2 = jnp.exp(m2 - m1)
    den = 1.0 + e2

    @pl.when(i == 0)
    def _():
        carry_scr[...] = jnp.zeros_like(carry_scr)

    sel1, sel2 = lane == i1, lane == i2
    picks = jnp.where(sel1 | sel2, 1.0, 0.0)
    ti = lax.broadcasted_iota(jnp.int32, (TM, TM), 0)
    tj = lax.broadcasted_iota(jnp.int32, (TM, TM), 1)
    earlier = jnp.where(tj < ti, 1.0, 0.0).astype(BF16)
    before = jnp.dot(earlier, picks.astype(BF16), preferred_element_type=F32) + carry_scr[...]
    r1 = jnp.sum(jnp.where(sel1, before, 0.0), axis=-1, keepdims=True)
    r2 = jnp.sum(jnp.where(sel2, before, 0.0), axis=-1, keepdims=True)
    carry = carry_scr[...] + jnp.sum(picks, axis=0, keepdims=True)
    carry_scr[...] = carry
    cnt_ref[...] = jnp.broadcast_to(carry, cnt_ref.shape)

    cols = (i1.astype(F32), i2.astype(F32), r1, r2, 1.0 / den, e2 / den)
    route = jnp.zeros(logits.shape, F32)
    for k, v in enumerate(cols):
        route = jnp.where(lane == k, v, route)
    route_ref[...] = route


def _outproj(ret_l, ret_c, pool_l, pool_c, w_out, x_lat, x_ctx, ctx_off, mod5, norm_g, layer, n_tiles,
             router_w=None):
    def lat_spec():
        return pl.BlockSpec((TM, RET_WIDTH), lambda i: (jnp.minimum(i, LAT_TILES - 1), 0))

    def ctx_spec():
        return pl.BlockSpec((TM, RET_WIDTH), lambda i: (jnp.maximum(i - LAT_TILES, 0), 0))

    row = pl.BlockSpec((TM, D_MODEL), lambda i: (i, 0))
    in_specs = [lat_spec(), ctx_spec(), lat_spec(), ctx_spec(),
                pl.BlockSpec((D_MODEL, D_MODEL), lambda i: (0, 0)),
                *_x_specs(ctx_off), _mod_spec(layer, 2),
                pl.BlockSpec((1, D_MODEL), lambda i: (0, 0)),
                _mod_spec(layer, 3), _mod_spec(layer, 4)]
    args = [ret_l, ret_c, pool_l, pool_c, w_out, x_lat, x_ctx, mod5, norm_g, mod5, mod5]
    n_rows = n_tiles * TM
    out_shape = [jax.ShapeDtypeStruct((n_rows, D_MODEL), F32), jax.ShapeDtypeStruct((n_rows, D_MODEL), BF16)]
    out_specs = [row, row]
    scratch = []
    if router_w is None:
        body = _outproj_kernel
    else:
        body = _outproj_router_kernel
        in_specs.append(pl.BlockSpec((D_MODEL, 128), lambda i: (0, 0)))
        args.append(router_w)
        out_shape[1] = jax.ShapeDtypeStruct((n_rows, D_MODEL), F32)
        out_shape += [jax.ShapeDtypeStruct((n_rows, 128), F32), jax.ShapeDtypeStruct((8, 128), F32)]
        out_specs += [pl.BlockSpec((TM, 128), lambda i: (i, 0)), pl.BlockSpec((8, 128), lambda i: (0, 0))]
        scratch = [pltpu.VMEM((1, 128), F32)]
    return pl.pallas_call(
        body, out_shape=tuple(out_shape), grid=(n_tiles,), in_specs=in_specs, out_specs=tuple(out_specs),
        scratch_shapes=scratch,
        compiler_params=pltpu.CompilerParams(vmem_limit_bytes=VMEM_LIMIT),
        name="outproj",
    )(*args)


def _swiglu(h, w13_ref, w2_ref, act_scr):
    for j in range(D_FF // FF_CHUNK):
        u = jnp.dot(h, w13_ref[0, :, j * FF_CHUNK:(j + 1) * FF_CHUNK], preferred_element_type=F32)
        g = jnp.dot(h, w13_ref[0, :, D_FF + j * FF_CHUNK:D_FF + (j + 1) * FF_CHUNK], preferred_element_type=F32)
        act_scr[:, j * FF_CHUNK:(j + 1) * FF_CHUNK] = (_silu(g) * u).astype(BF16)
    return jnp.dot(act_scr[...], w2_ref[0], preferred_element_type=F32)


def _ffn_kernel(h_ref, x_ref, gt_ref, w13_ref, w2_ref, o_ref, act_scr):
    o_ref[...] = x_ref[...] + gt_ref[...] * _swiglu(h_ref[...], w13_ref, w2_ref, act_scr)


def _ffn(h2, x_all, mod5, w13, w2, layer, n_tiles):
    row = pl.BlockSpec((TM, D_MODEL), lambda i: (i, 0))
    return pl.pallas_call(
        _ffn_kernel,
        out_shape=jax.ShapeDtypeStruct((n_tiles * TM, D_MODEL), F32),
        grid=(n_tiles,),
        in_specs=[row, row, _mod_spec(layer, 5),
                  pl.BlockSpec((1, D_MODEL, 2 * D_FF), lambda i: (0, 0, 0)),
                  pl.BlockSpec((1, D_FF, D_MODEL), lambda i: (0, 0, 0))],
        out_specs=row,
        scratch_shapes=[pltpu.VMEM((TM, D_FF), BF16)],
        compiler_params=pltpu.CompilerParams(vmem_limit_bytes=VMEM_LIMIT),
        name="ffn",
    )(h2, x_all, mod5, w13, w2)


def _moe_plan(counts, n_steps):
    cnt = counts.astype(jnp.int32)
    end = jnp.cumsum(cnt)
    start = end - cnt
    first = start // TM
    visits = jnp.where(cnt > 0, (end - 1) // TM - first + 1, 0)
    visit_end = jnp.cumsum(visits)
    visit_start = visit_end - visits
    total = visit_end[-1]
    step = jnp.minimum(jnp.arange(n_steps, dtype=jnp.int32), total - 1)
    eid = jnp.minimum(jnp.sum(step[:, None] >= visit_end[None, :], axis=1), N_EXPERTS - 1).astype(jnp.int32)
    tile = first[eid] + step - visit_start[eid]
    lo = jnp.clip(start[eid] - tile * TM, 0, TM)
    hi = jnp.clip(end[eid] - tile * TM, 0, TM)
    hi = jnp.where(jnp.arange(n_steps) < total, hi, lo)
    return start, tile.astype(jnp.int32), eid, lo.astype(jnp.int32), hi.astype(jnp.int32)


def _row_copy_wait(src, dst, sem):
    pltpu.make_async_copy(src, dst, sem).wait()


def _dispatch_kernel(pos_ref, h_ref, xs_hbm, sem):
    i = pl.program_id(0)
    n = 0
    for r in range(TM):
        for k in range(2):
            p = pos_ref[i, k * TM + r]
            pltpu.make_async_copy(h_ref.at[pl.ds(r, 1)], xs_hbm.at[pl.ds(p, 1)], sem).start(priority=n % 2)
            n += 1
    for k in range(2):
        _row_copy_wait(h_ref, xs_hbm.at[pl.ds(0, TM)], sem)


def _dispatch(pos, h, n_tiles):
    return pl.pallas_call(
        _dispatch_kernel,
        out_shape=jax.ShapeDtypeStruct((2 * n_tiles * TM, D_MODEL), F32),
        grid_spec=pltpu.PrefetchScalarGridSpec(
            num_scalar_prefetch=1,
            grid=(n_tiles,),
            in_specs=[pl.BlockSpec((TM, D_MODEL), lambda i, pos: (i, 0))],
            out_specs=pl.BlockSpec(memory_space=pl.ANY),
            scratch_shapes=[pltpu.SemaphoreType.DMA]),
        name="dispatch",
    )(pos, h)


def _expert_ffn_kernel(tile_ref, eid_ref, lo_ref, hi_ref, x_ref, w13_ref, w2_ref, o_ref, act_scr):
    s = pl.program_id(0)
    lo, hi = lo_ref[s], hi_ref[s]

    @pl.when(hi > lo)
    def _():
        row = lax.broadcasted_iota(jnp.int32, (TM, D_MODEL), 0)
        h = jnp.where((row >= lo) & (row < hi), x_ref[...], 0.0).astype(BF16)
        y = _swiglu(h, w13_ref, w2_ref, act_scr)

        @pl.when(lo == 0)
        def _():
            o_ref[...] = y

        @pl.when(lo > 0)
        def _():
            o_ref[...] += y


def _expert_ffn(plan, xs, w13, w2, moe_layer, n_steps):
    _, tile, eid, lo, hi = plan
    row = pl.BlockSpec((TM, D_MODEL), lambda s, tile, eid, lo, hi: (tile[s], 0))
    return pl.pallas_call(
        _expert_ffn_kernel,
        out_shape=jax.ShapeDtypeStruct(xs.shape, F32),
        grid_spec=pltpu.PrefetchScalarGridSpec(
            num_scalar_prefetch=4,
            grid=(n_steps,),
            in_specs=[row,
                      pl.BlockSpec((None, 1, D_MODEL, 2 * D_FF),
                                   lambda s, tile, eid, lo, hi: (moe_layer, eid[s], 0, 0)),
                      pl.BlockSpec((None, 1, D_FF, D_MODEL),
                                   lambda s, tile, eid, lo, hi: (moe_layer, eid[s], 0, 0))],
            out_specs=row,
            scratch_shapes=[pltpu.VMEM((TM, D_FF), BF16)]),
        compiler_params=pltpu.CompilerParams(vmem_limit_bytes=VMEM_LIMIT),
        name="expert_ffn",
    )(tile, eid, lo, hi, xs, w13, w2)


def _combine_kernel(pos_ref, y_hbm, route_ref, x_ref, gt_ref, g_ref, o_ref, ybuf, sem, *, final_norm):
    i = pl.program_id(0)
    n = 0
    for r in range(TM):
        for k in range(2):
            p = pos_ref[i, k * TM + r]
            pltpu.make_async_copy(y_hbm.at[pl.ds(p, 1)], ybuf.at[k, pl.ds(r, 1)], sem).start(priority=n % 2)
            n += 1
    for k in range(2):
        _row_copy_wait(y_hbm.at[pl.ds(0, TM)], ybuf.at[k], sem)
    w1 = route_ref[:, 4:5]
    w2 = route_ref[:, 5:6]
    x = x_ref[...] + gt_ref[...] * (w1 * ybuf[0] + w2 * ybuf[1])
    if final_norm:
        ms = jnp.mean(x * x, axis=-1, keepdims=True)
        x = x * lax.rsqrt(ms + NORM_EPS) * g_ref[...]
    o_ref[...] = x


def _combine(pos, y, route, x_all, mod5, final_g, layer, n_tiles, final_norm):
    row = pl.BlockSpec((TM, D_MODEL), lambda i, pos: (i, 0))
    return pl.pallas_call(
        functools.partial(_combine_kernel, final_norm=final_norm),
        out_shape=jax.ShapeDtypeStruct((n_tiles * TM, D_MODEL), F32),
        grid_spec=pltpu.PrefetchScalarGridSpec(
            num_scalar_prefetch=1,
            grid=(n_tiles,),
            in_specs=[pl.BlockSpec(memory_space=pl.ANY),
                      pl.BlockSpec((TM, 128), lambda i, pos: (i, 0)), row, _mod_spec(layer, 5),
                      pl.BlockSpec((1, D_MODEL), lambda i, pos: (0, 0))],
            out_specs=row,
            scratch_shapes=[pltpu.VMEM((2, TM, D_MODEL), F32), pltpu.SemaphoreType.DMA]),
        compiler_params=pltpu.CompilerParams(vmem_limit_bytes=VMEM_LIMIT),
        name="combine",
    )(pos, y, route, x_all, mod5, final_g)


def _moe(h, route, counts, x_all, mod5, w13, w2, final_g, layer, n_tiles, final_norm):
    n_steps = 2 * n_tiles + N_EXPERTS
    plan = _moe_plan(counts[0, :N_EXPERTS], n_steps)
    meta = route[:, :4].astype(jnp.int32)
    pos = plan[0][meta[:, 0:2]] + meta[:, 2:4]
    pos = pos.reshape(n_tiles, TM, 2).transpose(0, 2, 1).reshape(n_tiles, 2 * TM)
    xs = _dispatch(pos, h, n_tiles)
    ys = _expert_ffn(plan, xs, w13, w2, layer // 2, n_steps)
    return _combine(pos, ys, route, x_all, mod5, final_g, layer, n_tiles, final_norm)


def kernel(x, c, ctx, c_ctx, w_ada, b_ada, norm1_g, norm2_g, w_in, ret_decay_logit, ret_gn_g, pool_w, pool_scale,
           w_out, ffn_w13, ffn_w2, router_w, moe_w13, moe_w2, final_norm_g):
    assert DEPTH % 2 == 0, "the final norm is fused into the last routed-expert combine"
    c_all = jnp.concatenate([c, c_ctx[None, :], jnp.zeros((MOD_ROWS - BATCH - 1, D_MODEL), F32)], axis=0)
    mod5 = _ada_mod(c_all, w_ada, b_ada).reshape(DEPTH, MOD_ROWS, 6, 1, D_MODEL)
    tables = _rope_tables()
    log_gamma = jax.nn.log_sigmoid(ret_decay_logit.astype(F32))
    router_pad = jnp.pad(router_w, ((0, 0), (0, 0), (0, 128 - N_EXPERTS))).astype(BF16)
    moe_w13_b, moe_w2_b = moe_w13.astype(BF16), moe_w2.astype(BF16)
    final_g = final_norm_g[None, :]

    x_lat, x_ctx, ctx_off = x.reshape(N_LAT, D_MODEL), ctx.reshape(N_CTX, D_MODEL), 0
    for l in range(DEPTH):
        last = l == DEPTH - 1
        n_tiles = LAT_TILES if last else LAT_TILES + CTX_TILES
        qkvgp = _inproj(x_lat, x_ctx, ctx_off, norm1_g[l][None, :], mod5, w_in[l].astype(BF16), tables, l,
                        LAT_TILES + CTX_TILES)
        ret_l, ret_c = _retention(qkvgp, log_gamma[l], ret_gn_g[l][None, :])
        pool_l, pool_c = _pool(qkvgp, pool_w[l].astype(BF16), pool_scale[l][None, :])
        i = l // 2
        if l % 2 == 0:
            x_all, h2 = _outproj(ret_l, ret_c, pool_l, pool_c, w_out[l].astype(BF16), x_lat, x_ctx, ctx_off, mod5,
                                 norm2_g[l][None, :], l, n_tiles)
            x_all = _ffn(h2, x_all, mod5, ffn_w13[i][None].astype(BF16), ffn_w2[i][None].astype(BF16), l, n_tiles)
        else:
            x_all, h2, route, counts = _outproj(ret_l, ret_c, pool_l, pool_c, w_out[l].astype(BF16), x_lat, x_ctx,
                                                ctx_off, mod5, norm2_g[l][None, :], l, n_tiles,
                                                router_w=router_pad[i])
            x_all = _moe(h2, route, counts, x_all, mod5, moe_w13_b, moe_w2_b, final_g, l, n_tiles, last)
        x_lat, x_ctx, ctx_off = x_all, x_all, LAT_TILES
    return x_all.reshape(BATCH, SEQ, D_MODEL)
```

```python
import functools

import jax
import jax.numpy as jnp
from jax import lax
from jax.experimental import pallas as pl
from jax.experimental.pallas import tpu as pltpu

F32 = jnp.float32
BF16 = jnp.bfloat16

D_MODEL = 1024
BATCH = 8
SEQ = 4096
DEPTH = 4
GRID_W = 64
GRID_SHIFT = 6
CTX_LEN = 256
RET_WIDTH = 512
POOL_WIDTH = 512
RET_HEADS = 4
HEAD_DIM = 128
CHUNK = 128
ROPE_BASE = 10000.0
POOL_WINDOWS = (2, 4, 8, 16)
POOL_CH = 128
IN_COLS = 4 * RET_WIDTH + POOL_WIDTH
D_FF = 2816
N_EXPERTS = 8
NORM_EPS = 1e-6
GN_EPS = 1e-5

N_LAT = BATCH * SEQ
N_CTX = BATCH * CTX_LEN
N_ROWS = N_LAT + N_CTX
MOD_ROWS = 16
CTX_MOD_ROW = BATCH

TM = 512
LAT_TILES = N_LAT // TM
CTX_TILES = N_CTX // TM
TILES_PER_BATCH = SEQ // TM
FF_CHUNK = 256
VMEM_LIMIT = 56 * 1024 * 1024


def _mod_row(i):
    return jnp.where(i < LAT_TILES, i // TILES_PER_BATCH, CTX_MOD_ROW)


def _mod_spec(layer, part):
    return pl.BlockSpec((None, None, None, 1, D_MODEL),
                        lambda i, *_: (layer, _mod_row(i), part, 0, 0))


def _silu(v):
    return v * jax.nn.sigmoid(v)


def _x_specs(ctx_block_offset):
    return [pl.BlockSpec((TM, D_MODEL), lambda i, *_: (jnp.minimum(i, LAT_TILES - 1), 0)),
            pl.BlockSpec((TM, D_MODEL), lambda i, *_: (ctx_block_offset + jnp.maximum(i - LAT_TILES, 0), 0))]


def _x_tile(xl_ref, xc_ref):
    return jnp.where(pl.program_id(0) < LAT_TILES, xl_ref[...], xc_ref[...])


def _rms_mod(x, gain, shift, scale):
    ms = jnp.mean(x * x, axis=-1, keepdims=True)
    y = x * lax.rsqrt(ms + NORM_EPS) * gain
    return y * (1.0 + scale) + shift


def _ada_kernel(c_ref, w_ref, b_ref, o_ref):
    s = _silu(c_ref[...])
    o_ref[0] = jnp.dot(s.astype(BF16), w_ref[0].astype(BF16), preferred_element_type=F32) + b_ref[0]


def _ada_mod(c_all, w_ada, b_ada):
    tn = 1536
    return pl.pallas_call(
        _ada_kernel,
        out_shape=jax.ShapeDtypeStruct((DEPTH, MOD_ROWS, 6 * D_MODEL), F32),
        grid=(DEPTH, 6 * D_MODEL // tn),
        in_specs=[pl.BlockSpec((MOD_ROWS, D_MODEL), lambda l, n: (0, 0)),
                  pl.BlockSpec((1, D_MODEL, tn), lambda l, n: (l, 0, n)),
                  pl.BlockSpec((1, 1, tn), lambda l, n: (l, 0, n))],
        out_specs=pl.BlockSpec((1, MOD_ROWS, tn), lambda l, n: (l, 0, n)),
        compiler_params=pltpu.CompilerParams(vmem_limit_bytes=VMEM_LIMIT),
        name="ada_mod",
    )(c_all, w_ada, b_ada.reshape(DEPTH, 1, 6 * D_MODEL))


def _inproj_kernel(xl_ref, xc_ref, g_ref, sh_ref, sc_ref, w_ref, cos_ref, sa_ref, sb_ref, o_ref):
    h = _rms_mod(_x_tile(xl_ref, xc_ref), g_ref[...], sh_ref[...], sc_ref[...]).astype(BF16)
    _project_in(h, w_ref, cos_ref, sa_ref, sb_ref, o_ref)


def _project_in(h, w_ref, cos_ref, sa_ref, sb_ref, o_ref):
    cos, sa, sb = cos_ref[...], sa_ref[...], sb_ref[...]
    k_scale = HEAD_DIM ** -0.5
    for part, mul in ((0, 1.0), (1, k_scale)):
        z = jnp.dot(h, w_ref[:, part * RET_WIDTH:(part + 1) * RET_WIDTH], preferred_element_type=F32)
        for hh in range(RET_HEADS):
            t = z[:, hh * HEAD_DIM:(hh + 1) * HEAD_DIM]
            r = t * cos + pltpu.roll(t, 96, axis=1) * sa + pltpu.roll(t, 32, axis=1) * sb
            if mul != 1.0:
                r = r * mul
            col = part * RET_WIDTH + hh * HEAD_DIM
            o_ref[:, col:col + HEAD_DIM] = r.astype(BF16)
    z = jnp.dot(h, w_ref[:, 2 * RET_WIDTH:], preferred_element_type=F32)
    o_ref[:, 2 * RET_WIDTH:] = z.astype(BF16)


def _rope_tables():
    half = HEAD_DIM // 2
    inv_freq = ROPE_BASE ** (-jnp.arange(0, half, 2, dtype=F32) / half)
    t = jnp.arange(SEQ)
    rows, cols = (t // GRID_W).astype(F32), (t % GRID_W).astype(F32)
    ang_r = rows[:, None] * inv_freq[None, :]
    ang_c = cols[:, None] * inv_freq[None, :]
    zero = jnp.zeros_like(ang_r)
    cos = jnp.concatenate([jnp.cos(ang_r), jnp.cos(ang_r), jnp.cos(ang_c), jnp.cos(ang_c)], axis=1)
    sa = jnp.concatenate([-jnp.sin(ang_r), zero, -jnp.sin(ang_c), zero], axis=1)
    sb = jnp.concatenate([zero, jnp.sin(ang_r), zero, jnp.sin(ang_c)], axis=1)
    ident = jnp.ones((TM, HEAD_DIM), F32)
    none = jnp.zeros((TM, HEAD_DIM), F32)
    return (jnp.concatenate([cos, ident]), jnp.concatenate([sa, none]), jnp.concatenate([sb, none]))


def _inproj(x_lat, x_ctx, ctx_off, norm_g, mod5, w_in, tables, layer, n_tiles):
    tab_spec = pl.BlockSpec((TM, HEAD_DIM),
                            lambda i: (jnp.where(i < LAT_TILES, i % TILES_PER_BATCH, TILES_PER_BATCH), 0))
    return pl.pallas_call(
        _inproj_kernel,
        out_shape=jax.ShapeDtypeStruct((N_ROWS, IN_COLS), BF16),
        grid=(n_tiles,),
        in_specs=_x_specs(ctx_off) + [
            pl.BlockSpec((1, D_MODEL), lambda i: (0, 0)),
            _mod_spec(layer, 0), _mod_spec(layer, 1),
            pl.BlockSpec((D_MODEL, IN_COLS), lambda i: (0, 0)),
            tab_spec, tab_spec, tab_spec],
        out_specs=pl.BlockSpec((TM, IN_COLS), lambda i: (i, 0)),
        compiler_params=pltpu.CompilerParams(vmem_limit_bytes=VMEM_LIMIT),
        name="inproj",
    )(x_lat, x_ctx, norm_g, mod5, mod5, w_in, *tables)


N_CHUNK_LAT = SEQ // CHUNK
N_CHUNK_CTX = CTX_LEN // CHUNK
N_CHUNK = N_CHUNK_LAT + N_CHUNK_CTX
RET_UNROLL = 8


def _ret_kernel(lg_ref, ql, kl, vl, gl, qc, kc, vc, gc, gain_ref, ol, oc, kv_scr, st_scr):
    hd = pl.program_id(1)
    lgf = lg_ref[0, hd]
    lgb = lg_ref[1, hd]
    pi = lax.broadcasted_iota(jnp.int32, (CHUNK, CHUNK), 0).astype(F32)
    pj = lax.broadcasted_iota(jnp.int32, (CHUNK, CHUNK), 1).astype(F32)
    tail_f = jnp.exp(lgf * (CHUNK - 1.0 - pi))
    tail_b = jnp.exp(lgb * pi)
    head_f = jnp.exp(lgf * (pi + 1.0))
    head_b = jnp.exp(lgb * (CHUNK - pi))
    dif = pi - pj
    decay = jnp.where(dif >= 0, jnp.exp(lgf * jnp.maximum(dif, 0.0)), jnp.exp(lgb * jnp.maximum(-dif, 0.0)))
    zeros = jnp.zeros((CHUNK, CHUNK), F32)
    gchunk_f = jnp.exp(zeros + lgf * CHUNK)
    gchunk_b = jnp.exp(zeros + lgb * CHUNK)
    gain = gain_ref[...]

    def kv_of(k, v):
        kf = k.astype(F32)
        kk = jnp.concatenate([(kf * tail_f).astype(BF16), (kf * tail_b).astype(BF16)], axis=1)
        return lax.dot_general(kk, v, (((0,), (0,)), ((), ())), preferred_element_type=F32)

    for c in range(N_CHUNK_CTX):
        kv_scr[c] = kv_of(kc[c * CHUNK:(c + 1) * CHUNK, :], vc[c * CHUNK:(c + 1) * CHUNK, :])

    def kv_body(c, carry):
        r = pl.multiple_of(c * CHUNK, CHUNK)
        kv_scr[N_CHUNK_CTX + c] = kv_of(kl[pl.ds(r, CHUNK), :], vl[pl.ds(r, CHUNK), :])
        return carry

    lax.fori_loop(0, N_CHUNK_LAT, kv_body, 0, unroll=RET_UNROLL)

    def fwd_body(c, s):
        st_scr[c, 0:HEAD_DIM, :] = s.astype(BF16)
        return gchunk_f * s + kv_scr[c, 0:HEAD_DIM, :]

    lax.fori_loop(0, N_CHUNK, fwd_body, zeros)

    def bwd_step(c, s):
        st_scr[c, HEAD_DIM:2 * HEAD_DIM, :] = s.astype(BF16)
        return gchunk_b * s + kv_scr[c, HEAD_DIM:2 * HEAD_DIM, :]

    s = zeros
    for c in reversed(range(N_CHUNK_CTX)):
        s = bwd_step(c, s)
    lax.fori_loop(0, N_CHUNK_LAT, lambda t, s: bwd_step(N_CHUNK - 1 - t, s), s)

    def out_of(q, k, v, g, st):
        sc = lax.dot_general(q, k, (((1,), (1,)), ((), ())), preferred_element_type=F32)
        intra = jnp.dot((sc * decay).astype(BF16), v, preferred_element_type=F32)
        qf = q.astype(F32)
        qq = jnp.concatenate([(qf * head_f).astype(BF16), (qf * head_b).astype(BF16)], axis=1)
        o = intra + jnp.dot(qq, st, preferred_element_type=F32)
        mu = jnp.mean(o, axis=-1, keepdims=True)
        var = jnp.mean(jnp.square(o - mu), axis=-1, keepdims=True)
        on = (o - mu) * lax.rsqrt(var + GN_EPS)
        return (on * gain * _silu(g.astype(F32))).astype(BF16)

    for c in range(N_CHUNK_CTX):
        sl = slice(c * CHUNK, (c + 1) * CHUNK)
        oc[sl, :] = out_of(qc[sl, :], kc[sl, :], vc[sl, :], gc[sl, :], st_scr[c])

    def out_body(c, carry):
        r = pl.multiple_of(c * CHUNK, CHUNK)
        sl = pl.ds(r, CHUNK)
        ol[sl, :] = out_of(ql[sl, :], kl[sl, :], vl[sl, :], gl[sl, :], st_scr[N_CHUNK_CTX + c])
        return carry

    lax.fori_loop(0, N_CHUNK_LAT, out_body, 0, unroll=RET_UNROLL)


def _retention(qkvgp, log_gamma, gn_g):
    def lat(part):
        return pl.BlockSpec((SEQ, HEAD_DIM), lambda b, h: (b, part * RET_HEADS + h))

    def ctx(part):
        return pl.BlockSpec((CTX_LEN, HEAD_DIM), lambda b, h: (N_LAT // CTX_LEN + b, part * RET_HEADS + h))

    return pl.pallas_call(
        _ret_kernel,
        out_shape=(jax.ShapeDtypeStruct((N_LAT, RET_WIDTH), BF16),
                   jax.ShapeDtypeStruct((N_CTX, RET_WIDTH), BF16)),
        grid=(BATCH, RET_HEADS),
        in_specs=[pl.BlockSpec(memory_space=pltpu.SMEM),
                  lat(0), lat(1), lat(2), lat(3), ctx(0), ctx(1), ctx(2), ctx(3),
                  pl.BlockSpec((1, HEAD_DIM), lambda b, h: (0, h))],
        out_specs=(pl.BlockSpec((SEQ, HEAD_DIM), lambda b, h: (b, h)),
                   pl.BlockSpec((CTX_LEN, HEAD_DIM), lambda b, h: (b, h))),
        scratch_shapes=[pltpu.VMEM((N_CHUNK, 2 * HEAD_DIM, HEAD_DIM), F32),
                        pltpu.VMEM((N_CHUNK, 2 * HEAD_DIM, HEAD_DIM), BF16)],
        compiler_params=pltpu.CompilerParams(vmem_limit_bytes=VMEM_LIMIT),
        name="retention",
    )(log_gamma, *([qkvgp] * 8), gn_g)


POOL_BLK = 256


def _split_dot(m, x):
    hi = x.astype(BF16)
    lo = (x - hi.astype(F32)).astype(BF16)
    both = jnp.dot(m, jnp.concatenate([hi, lo], axis=1), preferred_element_type=F32)
    return both[:, :x.shape[1]] + both[:, x.shape[1]:]


def _window_count(idx, w, n):
    return (jnp.minimum(idx - w // 2 + w, n) - jnp.maximum(idx - w // 2, 0)).astype(F32)


def _row_window_mean(s, w):
    size = [min(r - w // 2 + w, GRID_W) - max(r - w // 2, 0) for r in range(GRID_W)]
    parts, r0 = [], 0
    for r in range(1, GRID_W + 1):
        if r == GRID_W or size[r] != size[r0]:
            parts.append(s[r0 * GRID_W:r * GRID_W] * (1.0 / size[r0]))
            r0 = r
    return jnp.concatenate(parts, axis=0)


def _pool_kernel(pl_ref, pc_ref, w_ref, scale_ref, ol, oc):
    ti = lax.broadcasted_iota(jnp.int32, (POOL_BLK, POOL_BLK), 0)
    tj = lax.broadcasted_iota(jnp.int32, (POOL_BLK, POOL_BLK), 1)
    tok_b = lax.broadcasted_iota(jnp.int32, (POOL_BLK, POOL_CH), 0)
    tok_c = lax.broadcasted_iota(jnp.int32, (CTX_LEN, POOL_CH), 0)
    for gi, w in enumerate(POOL_WINDOWS):
        cols = slice(gi * POOL_CH, (gi + 1) * POOL_CH)
        lo_off, hi_off = -(w // 2), w - 1 - w // 2
        wmat = w_ref[gi]
        scale = scale_ref[:, cols]

        u = pl_ref[:, cols].astype(F32)
        pad = jnp.zeros((w // 2 * GRID_W, POOL_CH), F32)
        s = jnp.concatenate([pad, u, pad], axis=0)
        span = 1
        while span < w:
            n = s.shape[0] - span * GRID_W
            s = s[:n] + s[span * GRID_W:]
            span *= 2
        row_mean = _row_window_mean(s[:SEQ], w)
        ci, cj = ti & (GRID_W - 1), tj & (GRID_W - 1)
        band = ((ti >> GRID_SHIFT == tj >> GRID_SHIFT) & (cj >= ci + lo_off) & (cj <= ci + hi_off))
        band = jnp.where(band, 1.0, 0.0).astype(BF16)
        col_inv = 1.0 / _window_count(tok_b & (GRID_W - 1), w, GRID_W)
        for blk in range(SEQ // POOL_BLK):
            rs = slice(blk * POOL_BLK, (blk + 1) * POOL_BLK)
            m = _split_dot(band, row_mean[rs]) * col_inv
            d = (m - u[rs]).astype(BF16)
            y = jnp.dot(d, wmat, preferred_element_type=F32) * scale
            ol[rs, cols] = y.astype(BF16)

        uc = pc_ref[:, cols].astype(F32)
        band_c = jnp.where((tj >= ti + lo_off) & (tj <= ti + hi_off), 1.0, 0.0).astype(BF16)
        mc = _split_dot(band_c, uc) / _window_count(tok_c, w, CTX_LEN)
        dc = (mc - uc).astype(BF16)
        oc[:, cols] = (jnp.dot(dc, wmat, preferred_element_type=F32) * scale).astype(BF16)


def _pool(qkvgp, pool_w, pool_scale):
    pcol = 4 * RET_WIDTH // POOL_WIDTH
    return pl.pallas_call(
        _pool_kernel,
        out_shape=(jax.ShapeDtypeStruct((N_LAT, POOL_WIDTH), BF16),
                   jax.ShapeDtypeStruct((N_CTX, POOL_WIDTH), BF16)),
        grid=(BATCH,),
        in_specs=[pl.BlockSpec((SEQ, POOL_WIDTH), lambda b: (b, pcol)),
                  pl.BlockSpec((CTX_LEN, POOL_WIDTH), lambda b: (N_LAT // CTX_LEN + b, pcol)),
                  pl.BlockSpec((len(POOL_WINDOWS), POOL_CH, POOL_CH), lambda b: (0, 0, 0)),
                  pl.BlockSpec((1, POOL_WIDTH), lambda b: (0, 0))],
        out_specs=(pl.BlockSpec((SEQ, POOL_WIDTH), lambda b: (b, 0)),
                   pl.BlockSpec((CTX_LEN, POOL_WIDTH), lambda b: (b, 0))),
        compiler_params=pltpu.CompilerParams(vmem_limit_bytes=VMEM_LIMIT),
        name="pool",
    )(qkvgp, qkvgp, pool_w, pool_scale)


def _mix_out(i, rl, rc, pl_, pc, w_ref, xl_ref, xc_ref, gt_ref):
    is_lat = i < LAT_TILES
    ret = jnp.where(is_lat, rl[...], rc[...])
    pool = jnp.where(is_lat, pl_[...], pc[...])
    y = (jnp.dot(ret, w_ref[0:RET_WIDTH, :], preferred_element_type=F32)
         + jnp.dot(pool, w_ref[RET_WIDTH:, :], preferred_element_type=F32))
    return _x_tile(xl_ref, xc_ref) + gt_ref[...] * y


def _outproj_router_kernel(rl, rc, pl_, pc, w_ref, xl_ref, xc_ref, gt_ref, g_ref, sh_ref, sc_ref, wr_ref,
                           xo_ref, h_ref, route_ref, cnt_ref, carry_scr):
    i = pl.program_id(0)
    x = _mix_out(i, rl, rc, pl_, pc, w_ref, xl_ref, xc_ref, gt_ref)
    xo_ref[...] = x
    h = _rms_mod(x, g_ref[...], sh_ref[...], sc_ref[...])
    h_ref[...] = h
    logits = jnp.dot(h.astype(BF16), wr_ref[...], preferred_element_type=F32)
    lane = lax.broadcasted_iota(jnp.int32, logits.shape, 1)
    neg = jnp.float32(-jnp.inf)
    logits = jnp.where(lane < N_EXPERTS, logits, neg)
    m1 = jnp.max(logits, axis=-1, keepdims=True)
    i1 = jnp.min(jnp.where(logits == m1, lane, 128), axis=-1, keepdims=True)
    rest = jnp.where(lane == i1, neg, logits)
    m2 = jnp.max(rest, axis=-1, keepdims=True)
    i2 = jnp.min(jnp.where(rest == m2, lane, 128), axis=-1, keepdims=True)
    e2 = jnp.exp(m2 - m1)
    den = 1.0 + e2

    @pl.when(i == 0)
    def _():
        carry_scr[...] = jnp.zeros_like(carry_scr)

    sel1, sel2 = lane == i1, lane == i2
    picks = jnp.where(sel1 | sel2, 1.0, 0.0)
    ti = lax.broadcasted_iota(jnp.int32, (TM, TM), 0)
    tj = lax.broadcasted_iota(jnp.int32, (TM, TM), 1)
    earlier = jnp.where(tj < ti, 1.0, 0.0).astype(BF16)
    before = jnp.dot(earlier, picks.astype(BF16), preferred_element_type=F32) + carry_scr[...]
    r1 = jnp.sum(jnp.where(sel1, before, 0.0), axis=-1, keepdims=True)
    r2 = jnp.sum(jnp.where(sel2, before, 0.0), axis=-1, keepdims=True)
    carry = carry_scr[...] + jnp.sum(picks, axis=0, keepdims=True)
    carry_scr[...] = carry
    cnt_ref[...] = jnp.broadcast_to(carry, cnt_ref.shape)

    cols = (i1.astype(F32), i2.astype(F32), r1, r2, 1.0 / den, e2 / den)
    route = jnp.zeros(logits.shape, F32)
    for k, v in enumerate(cols):
        route = jnp.where(lane == k, v, route)
    route_ref[...] = route


def _mixer_specs(ctx_off, layer):
    def lat_spec():
        return pl.BlockSpec((TM, RET_WIDTH), lambda i: (jnp.minimum(i, LAT_TILES - 1), 0))

    def ctx_spec():
        return pl.BlockSpec((TM, RET_WIDTH), lambda i: (jnp.maximum(i - LAT_TILES, 0), 0))

    return [lat_spec(), ctx_spec(), lat_spec(), ctx_spec(),
            pl.BlockSpec((D_MODEL, D_MODEL), lambda i: (0, 0), pipeline_mode=pl.Buffered(1)),
            *_x_specs(ctx_off), _mod_spec(layer, 2),
            pl.BlockSpec((1, D_MODEL), lambda i: (0, 0)),
            _mod_spec(layer, 3), _mod_spec(layer, 4)]


def _outproj_router(ret_l, ret_c, pool_l, pool_c, w_out, x_lat, x_ctx, ctx_off, mod5, norm_g, router_w, layer,
                    n_tiles):
    row = pl.BlockSpec((TM, D_MODEL), lambda i: (i, 0))
    n_rows = n_tiles * TM
    return pl.pallas_call(
        _outproj_router_kernel,
        out_shape=(jax.ShapeDtypeStruct((n_rows, D_MODEL), F32), jax.ShapeDtypeStruct((n_rows, D_MODEL), F32),
                   jax.ShapeDtypeStruct((n_rows, 128), F32), jax.ShapeDtypeStruct((8, 128), F32)),
        grid=(n_tiles,),
        in_specs=_mixer_specs(ctx_off, layer) + [pl.BlockSpec((D_MODEL, 128), lambda i: (0, 0))],
        out_specs=(row, row, pl.BlockSpec((TM, 128), lambda i: (i, 0)), pl.BlockSpec((8, 128), lambda i: (0, 0))),
        scratch_shapes=[pltpu.VMEM((1, 128), F32)],
        compiler_params=pltpu.CompilerParams(dimension_semantics=("arbitrary",), vmem_limit_bytes=VMEM_LIMIT),
        name="outproj",
    )(ret_l, ret_c, pool_l, pool_c, w_out, x_lat, x_ctx, mod5, norm_g, mod5, mod5, router_w)


def _swiglu(h, w13_ref, w2_ref, act_scr):
    for j in range(D_FF // FF_CHUNK):
        u = jnp.dot(h, w13_ref[0, :, j * FF_CHUNK:(j + 1) * FF_CHUNK], preferred_element_type=F32)
        g = jnp.dot(h, w13_ref[0, :, D_FF + j * FF_CHUNK:D_FF + (j + 1) * FF_CHUNK], preferred_element_type=F32)
        act_scr[:, j * FF_CHUNK:(j + 1) * FF_CHUNK] = (_silu(g) * u).astype(BF16)
    return jnp.dot(act_scr[...], w2_ref[0], preferred_element_type=F32)


def _outproj_ffn_kernel(rl, rc, pl_, pc, w_ref, xl_ref, xc_ref, gt1_ref, g_ref, sh_ref, sc_ref, gt2_ref,
                        w13_ref, w2_ref, o_ref, act_scr):
    x = _mix_out(pl.program_id(0), rl, rc, pl_, pc, w_ref, xl_ref, xc_ref, gt1_ref)
    h = _rms_mod(x, g_ref[...], sh_ref[...], sc_ref[...]).astype(BF16)
    o_ref[...] = x + gt2_ref[...] * _swiglu(h, w13_ref, w2_ref, act_scr)


def _outproj_ffn(ret_l, ret_c, pool_l, pool_c, w_out, x_lat, x_ctx, ctx_off, mod5, norm_g, w13, w2, layer, n_tiles):
    resident = pl.Buffered(1)
    return pl.pallas_call(
        _outproj_ffn_kernel,
        out_shape=jax.ShapeDtypeStruct((n_tiles * TM, D_MODEL), F32),
        grid=(n_tiles,),
        in_specs=_mixer_specs(ctx_off, layer) + [
            _mod_spec(layer, 5),
            pl.BlockSpec((1, D_MODEL, 2 * D_FF), lambda i: (0, 0, 0), pipeline_mode=resident),
            pl.BlockSpec((1, D_FF, D_MODEL), lambda i: (0, 0, 0), pipeline_mode=resident)],
        out_specs=pl.BlockSpec((TM, D_MODEL), lambda i: (i, 0)),
        scratch_shapes=[pltpu.VMEM((TM, D_FF), BF16)],
        compiler_params=pltpu.CompilerParams(vmem_limit_bytes=VMEM_LIMIT),
        name="outproj_ffn",
    )(ret_l, ret_c, pool_l, pool_c, w_out, x_lat, x_ctx, mod5, norm_g, mod5, mod5, mod5, w13, w2)


def _moe_plan(counts, n_steps):
    cnt = counts.astype(jnp.int32)
    end = jnp.cumsum(cnt)
    start = end - cnt
    first = start // TM
    visits = jnp.where(cnt > 0, (end - 1) // TM - first + 1, 0)
    visit_end = jnp.cumsum(visits)
    visit_start = visit_end - visits
    total = visit_end[-1]
    step = jnp.minimum(jnp.arange(n_steps, dtype=jnp.int32), total - 1)
    eid = jnp.minimum(jnp.sum(step[:, None] >= visit_end[None, :], axis=1), N_EXPERTS - 1).astype(jnp.int32)
    tile = first[eid] + step - visit_start[eid]
    lo = jnp.clip(start[eid] - tile * TM, 0, TM)
    hi = jnp.clip(end[eid] - tile * TM, 0, TM)
    hi = jnp.where(jnp.arange(n_steps) < total, hi, lo)
    return start, tile.astype(jnp.int32), eid, lo.astype(jnp.int32), hi.astype(jnp.int32)


def _row_copy_wait(src, dst, sem):
    pltpu.make_async_copy(src, dst, sem).wait()


def _dispatch_kernel(pos_ref, h_ref, xs_hbm, sem):
    i = pl.program_id(0)
    n = 0
    for r in range(TM):
        for k in range(2):
            p = pos_ref[i, k * TM + r]
            pltpu.make_async_copy(h_ref.at[pl.ds(r, 1)], xs_hbm.at[pl.ds(p, 1)], sem).start(priority=n % 2)
            n += 1
    for k in range(2):
        _row_copy_wait(h_ref, xs_hbm.at[pl.ds(0, TM)], sem)


def _dispatch(pos, h, n_tiles):
    return pl.pallas_call(
        _dispatch_kernel,
        out_shape=jax.ShapeDtypeStruct((2 * n_tiles * TM, D_MODEL), F32),
        grid_spec=pltpu.PrefetchScalarGridSpec(
            num_scalar_prefetch=1,
            grid=(n_tiles,),
            in_specs=[pl.BlockSpec((TM, D_MODEL), lambda i, pos: (i, 0))],
            out_specs=pl.BlockSpec(memory_space=pl.ANY),
            scratch_shapes=[pltpu.SemaphoreType.DMA]),
        name="dispatch",
    )(pos, h)


def _expert_ffn_kernel(tile_ref, eid_ref, lo_ref, hi_ref, x_ref, w13_ref, w2_ref, o_ref, act_scr):
    s = pl.program_id(0)
    lo, hi = lo_ref[s], hi_ref[s]

    @pl.when(hi > lo)
    def _():
        row = lax.broadcasted_iota(jnp.int32, (TM, D_MODEL), 0)
        h = jnp.where((row >= lo) & (row < hi), x_ref[...], 0.0).astype(BF16)
        y = _swiglu(h, w13_ref, w2_ref, act_scr)

        @pl.when(lo == 0)
        def _():
            o_ref[...] = y

        @pl.when(lo > 0)
        def _():
            o_ref[...] += y


def _expert_ffn(plan, xs, w13, w2, moe_layer, n_steps):
    _, tile, eid, lo, hi = plan
    row = pl.BlockSpec((TM, D_MODEL), lambda s, tile, eid, lo, hi: (tile[s], 0))
    return pl.pallas_call(
        _expert_ffn_kernel,
        out_shape=jax.ShapeDtypeStruct(xs.shape, F32),
        grid_spec=pltpu.PrefetchScalarGridSpec(
            num_scalar_prefetch=4,
            grid=(n_steps,),
            in_specs=[row,
                      pl.BlockSpec((None, 1, D_MODEL, 2 * D_FF),
                                   lambda s, tile, eid, lo, hi: (moe_layer, eid[s], 0, 0)),
                      pl.BlockSpec((None, 1, D_FF, D_MODEL),
                                   lambda s, tile, eid, lo, hi: (moe_layer, eid[s], 0, 0))],
            out_specs=row,
            scratch_shapes=[pltpu.VMEM((TM, D_FF), BF16)]),
        compiler_params=pltpu.CompilerParams(vmem_limit_bytes=VMEM_LIMIT),
        name="expert_ffn",
    )(tile, eid, lo, hi, xs, w13, w2)


def _combine_kernel(pos_ref, y_hbm, route_ref, x_ref, gt_ref, g_ref, o_ref, ybuf, sem, *, n_tiles, final_norm):
    i = pl.program_id(0)
    slot = i % 2

    def request(tile, into):
        n = 0
        for r in range(TM):
            for k in range(2):
                p = pos_ref[tile, k * TM + r]
                pltpu.make_async_copy(y_hbm.at[pl.ds(p, 1)], ybuf.at[into, k, pl.ds(r, 1)],
                                      sem.at[into]).start(priority=n % 2)
                n += 1

    @pl.when(i == 0)
    def _():
        request(0, 0)

    @pl.when(i + 1 < n_tiles)
    def _():
        request(i + 1, 1 - slot)

    for k in range(2):
        _row_copy_wait(y_hbm.at[pl.ds(0, TM)], ybuf.at[slot, k], sem.at[slot])
    w1 = route_ref[:, 4:5]
    w2 = route_ref[:, 5:6]
    x = x_ref[...] + gt_ref[...] * (w1 * ybuf[slot, 0] + w2 * ybuf[slot, 1])
    if final_norm:
        ms = jnp.mean(x * x, axis=-1, keepdims=True)
        x = x * lax.rsqrt(ms + NORM_EPS) * g_ref[...]
    o_ref[...] = x


def _combine(pos, y, route, x_all, mod5, final_g, layer, n_tiles, final_norm):
    row = pl.BlockSpec((TM, D_MODEL), lambda i, pos: (i, 0))
    return pl.pallas_call(
        functools.partial(_combine_kernel, n_tiles=n_tiles, final_norm=final_norm),
        out_shape=jax.ShapeDtypeStruct((n_tiles * TM, D_MODEL), F32),
        grid_spec=pltpu.PrefetchScalarGridSpec(
            num_scalar_prefetch=1,
            grid=(n_tiles,),
            in_specs=[pl.BlockSpec(memory_space=pl.ANY),
                      pl.BlockSpec((TM, 128), lambda i, pos: (i, 0)), row, _mod_spec(layer, 5),
                      pl.BlockSpec((1, D_MODEL), lambda i, pos: (0, 0))],
            out_specs=row,
            scratch_shapes=[pltpu.VMEM((2, 2, TM, D_MODEL), F32), pltpu.SemaphoreType.DMA((2,))]),
        compiler_params=pltpu.CompilerParams(dimension_semantics=("arbitrary",), vmem_limit_bytes=VMEM_LIMIT),
        name="combine",
    )(pos, y, route, x_all, mod5, final_g)


def _moe(h, route, counts, x_all, mod5, w13, w2, final_g, layer, n_tiles, final_norm):
    n_steps = 2 * n_tiles + N_EXPERTS
    plan = _moe_plan(counts[0, :N_EXPERTS], n_steps)
    meta = route[:, :4].astype(jnp.int32)
    pos = plan[0][meta[:, 0:2]] + meta[:, 2:4]
    pos = pos.reshape(n_tiles, TM, 2).transpose(0, 2, 1).reshape(n_tiles, 2 * TM)
    xs = _dispatch(pos, h, n_tiles)
    ys = _expert_ffn(plan, xs, w13, w2, layer // 2, n_steps)
    return _combine(pos, ys, route, x_all, mod5, final_g, layer, n_tiles, final_norm)


def kernel(x, c, ctx, c_ctx, w_ada, b_ada, norm1_g, norm2_g, w_in, ret_decay_logit, ret_gn_g, pool_w, pool_scale,
           w_out, ffn_w13, ffn_w2, router_w, moe_w13, moe_w2, final_norm_g):
    assert DEPTH % 2 == 0, "the final norm is fused into the last routed-expert combine"
    c_all = jnp.concatenate([c, c_ctx[None, :], jnp.zeros((MOD_ROWS - BATCH - 1, D_MODEL), F32)], axis=0)
    mod5 = _ada_mod(c_all, w_ada, b_ada).reshape(DEPTH, MOD_ROWS, 6, 1, D_MODEL)
    tables = _rope_tables()
    log_gamma = jax.nn.log_sigmoid(ret_decay_logit.astype(F32))
    router_pad = jnp.pad(router_w, ((0, 0), (0, 0), (0, 128 - N_EXPERTS))).astype(BF16)
    moe_w13_b, moe_w2_b = moe_w13.astype(BF16), moe_w2.astype(BF16)
    final_g = final_norm_g[None, :]

    x_lat, x_ctx, ctx_off = x.reshape(N_LAT, D_MODEL), ctx.reshape(N_CTX, D_MODEL), 0
    for l in range(DEPTH):
        last = l == DEPTH - 1
        n_tiles = LAT_TILES if last else LAT_TILES + CTX_TILES
        qkvgp = _inproj(x_lat, x_ctx, ctx_off, norm1_g[l][None, :], mod5, w_in[l].astype(BF16), tables, l,
                        LAT_TILES + CTX_TILES)
        ret_l, ret_c = _retention(qkvgp, log_gamma[l], ret_gn_g[l][None, :])
        pool_l, pool_c = _pool(qkvgp, pool_w[l].astype(BF16), pool_scale[l][None, :])
        i = l // 2
        if l % 2 == 0:
            x_all = _outproj_ffn(ret_l, ret_c, pool_l, pool_c, w_out[l].astype(BF16), x_lat, x_ctx, ctx_off, mod5,
                                 norm2_g[l][None, :], ffn_w13[i][None].astype(BF16), ffn_w2[i][None].astype(BF16),
                                 l, n_tiles)
        else:
            x_all, h2, route, counts = _outproj_router(ret_l, ret_c, pool_l, pool_c, w_out[l].astype(BF16), x_lat,
                                                       x_ctx, ctx_off, mod5, norm2_g[l][None, :], router_pad[i],
                                                       l, n_tiles)
            x_all = _moe(h2, route, counts, x_all, mod5, moe_w13_b, moe_w2_b, final_g, l, n_tiles, last)
        x_lat, x_ctx, ctx_off = x_all, x_all, LAT_TILES
    return x_all.reshape(BATCH, SEQ, D_MODEL)
```

```python
import functools

import jax
import jax.numpy as jnp
from jax import lax
from jax.experimental import pallas as pl
from jax.experimental.pallas import tpu as pltpu

F32 = jnp.float32
BF16 = jnp.bfloat16

D_MODEL = 1024
BATCH = 8
SEQ = 4096
DEPTH = 4
GRID_W = 64
GRID_SHIFT = 6
CTX_LEN = 256
RET_WIDTH = 512
POOL_WIDTH = 512
RET_HEADS = 4
HEAD_DIM = 128
CHUNK = 128
ROPE_BASE = 10000.0
POOL_WINDOWS = (2, 4, 8, 16)
POOL_CH = 128
IN_COLS = 4 * RET_WIDTH + POOL_WIDTH
D_FF = 2816
N_EXPERTS = 8
NORM_EPS = 1e-6
GN_EPS = 1e-5

N_LAT = BATCH * SEQ
N_CTX = BATCH * CTX_LEN
N_ROWS = N_LAT + N_CTX
MOD_ROWS = 16
CTX_MOD_ROW = BATCH

TM = 512
LAT_TILES = N_LAT // TM
CTX_TILES = N_CTX // TM
TILES_PER_BATCH = SEQ // TM
FF_CHUNK = 256
LANES = 128
ROW_TILE = (D_MODEL // LANES, LANES)
VMEM_LIMIT = 56 * 1024 * 1024


def _mod_row(i):
    return jnp.where(i < LAT_TILES, i // TILES_PER_BATCH, CTX_MOD_ROW)


def _mod_spec(layer, part):
    return pl.BlockSpec((None, None, None, 1, D_MODEL),
                        lambda i, *_: (layer, _mod_row(i), part, 0, 0))


def _silu(v):
    return v * jax.nn.sigmoid(v)


def _x_specs(ctx_block_offset):
    return [pl.BlockSpec((TM, D_MODEL), lambda i, *_: (jnp.minimum(i, LAT_TILES - 1), 0)),
            pl.BlockSpec((TM, D_MODEL), lambda i, *_: (ctx_block_offset + jnp.maximum(i - LAT_TILES, 0), 0))]


def _x_tile(xl_ref, xc_ref):
    return jnp.where(pl.program_id(0) < LAT_TILES, xl_ref[...], xc_ref[...])


def _rms_mod(x, gain, shift, scale):
    ms = jnp.mean(x * x, axis=-1, keepdims=True)
    y = x * lax.rsqrt(ms + NORM_EPS) * gain
    return y * (1.0 + scale) + shift


def _ada_kernel(c_ref, w_ref, b_ref, o_ref):
    s = _silu(c_ref[...])
    o_ref[0] = jnp.dot(s.astype(BF16), w_ref[0].astype(BF16), preferred_element_type=F32) + b_ref[0]


def _ada_mod(c_all, w_ada, b_ada):
    tn = 1536
    return pl.pallas_call(
        _ada_kernel,
        out_shape=jax.ShapeDtypeStruct((DEPTH, MOD_ROWS, 6 * D_MODEL), F32),
        grid=(DEPTH, 6 * D_MODEL // tn),
        in_specs=[pl.BlockSpec((MOD_ROWS, D_MODEL), lambda l, n: (0, 0)),
                  pl.BlockSpec((1, D_MODEL, tn), lambda l, n: (l, 0, n)),
                  pl.BlockSpec((1, 1, tn), lambda l, n: (l, 0, n))],
        out_specs=pl.BlockSpec((1, MOD_ROWS, tn), lambda l, n: (l, 0, n)),
        compiler_params=pltpu.CompilerParams(vmem_limit_bytes=VMEM_LIMIT),
        name="ada_mod",
    )(c_all, w_ada, b_ada.reshape(DEPTH, 1, 6 * D_MODEL))


def _inproj_kernel(xl_ref, xc_ref, g_ref, sh_ref, sc_ref, w_ref, cos_ref, sa_ref, sb_ref, o_ref):
    h = _rms_mod(_x_tile(xl_ref, xc_ref), g_ref[...], sh_ref[...], sc_ref[...]).astype(BF16)
    _project_in(h, w_ref, cos_ref, sa_ref, sb_ref, o_ref)


def _project_in(h, w_ref, cos_ref, sa_ref, sb_ref, o_ref):
    cos, sa, sb = cos_ref[...], sa_ref[...], sb_ref[...]
    k_scale = HEAD_DIM ** -0.5
    for part, mul in ((0, 1.0), (1, k_scale)):
        z = jnp.dot(h, w_ref[:, part * RET_WIDTH:(part + 1) * RET_WIDTH], preferred_element_type=F32)
        for hh in range(RET_HEADS):
            t = z[:, hh * HEAD_DIM:(hh + 1) * HEAD_DIM]
            r = t * cos + pltpu.roll(t, 96, axis=1) * sa + pltpu.roll(t, 32, axis=1) * sb
            if mul != 1.0:
                r = r * mul
            col = part * RET_WIDTH + hh * HEAD_DIM
            o_ref[:, col:col + HEAD_DIM] = r.astype(BF16)
    z = jnp.dot(h, w_ref[:, 2 * RET_WIDTH:], preferred_element_type=F32)
    o_ref[:, 2 * RET_WIDTH:] = z.astype(BF16)


def _rope_tables():
    half = HEAD_DIM // 2
    inv_freq = ROPE_BASE ** (-jnp.arange(0, half, 2, dtype=F32) / half)
    t = jnp.arange(SEQ)
    rows, cols = (t // GRID_W).astype(F32), (t % GRID_W).astype(F32)
    ang_r = rows[:, None] * inv_freq[None, :]
    ang_c = cols[:, None] * inv_freq[None, :]
    zero = jnp.zeros_like(ang_r)
    cos = jnp.concatenate([jnp.cos(ang_r), jnp.cos(ang_r), jnp.cos(ang_c), jnp.cos(ang_c)], axis=1)
    sa = jnp.concatenate([-jnp.sin(ang_r), zero, -jnp.sin(ang_c), zero], axis=1)
    sb = jnp.concatenate([zero, jnp.sin(ang_r), zero, jnp.sin(ang_c)], axis=1)
    ident = jnp.ones((TM, HEAD_DIM), F32)
    none = jnp.zeros((TM, HEAD_DIM), F32)
    return (jnp.concatenate([cos, ident]), jnp.concatenate([sa, none]), jnp.concatenate([sb, none]))


def _inproj(x_lat, x_ctx, ctx_off, norm_g, mod5, w_in, tables, layer, n_tiles):
    tab_spec = pl.BlockSpec((TM, HEAD_DIM),
                            lambda i: (jnp.where(i < LAT_TILES, i % TILES_PER_BATCH, TILES_PER_BATCH), 0))
    return pl.pallas_call(
        _inproj_kernel,
        out_shape=jax.ShapeDtypeStruct((N_ROWS, IN_COLS), BF16),
        grid=(n_tiles,),
        in_specs=_x_specs(ctx_off) + [
            pl.BlockSpec((1, D_MODEL), lambda i: (0, 0)),
            _mod_spec(layer, 0), _mod_spec(layer, 1),
            pl.BlockSpec((D_MODEL, IN_COLS), lambda i: (0, 0)),
            tab_spec, tab_spec, tab_spec],
        out_specs=pl.BlockSpec((TM, IN_COLS), lambda i: (i, 0)),
        compiler_params=pltpu.CompilerParams(vmem_limit_bytes=VMEM_LIMIT),
        name="inproj",
    )(x_lat, x_ctx, norm_g, mod5, mod5, w_in, *tables)


N_CHUNK_LAT = SEQ // CHUNK
N_CHUNK_CTX = CTX_LEN // CHUNK
N_CHUNK = N_CHUNK_LAT + N_CHUNK_CTX
RET_UNROLL = 8


def _ret_kernel(lg_ref, ql, kl, vl, gl, qc, kc, vc, gc, gain_ref, ol, oc, kv_scr, st_scr):
    hd = pl.program_id(1)
    lgf = lg_ref[0, hd]
    lgb = lg_ref[1, hd]
    pi = lax.broadcasted_iota(jnp.int32, (CHUNK, CHUNK), 0).astype(F32)
    pj = lax.broadcasted_iota(jnp.int32, (CHUNK, CHUNK), 1).astype(F32)
    tail_f = jnp.exp(lgf * (CHUNK - 1.0 - pi))
    tail_b = jnp.exp(lgb * pi)
    head_f = jnp.exp(lgf * (pi + 1.0))
    head_b = jnp.exp(lgb * (CHUNK - pi))
    dif = pi - pj
    decay = jnp.where(dif >= 0, jnp.exp(lgf * jnp.maximum(dif, 0.0)), jnp.exp(lgb * jnp.maximum(-dif, 0.0)))
    zeros = jnp.zeros((CHUNK, CHUNK), F32)
    gchunk_f = jnp.exp(zeros + lgf * CHUNK)
    gchunk_b = jnp.exp(zeros + lgb * CHUNK)
    gain = gain_ref[...]

    def kv_of(k, v):
        kf = k.astype(F32)
        kk = jnp.concatenate([(kf * tail_f).astype(BF16), (kf * tail_b).astype(BF16)], axis=1)
        return lax.dot_general(kk, v, (((0,), (0,)), ((), ())), preferred_element_type=F32)

    for c in range(N_CHUNK_CTX):
        kv_scr[c] = kv_of(kc[c * CHUNK:(c + 1) * CHUNK, :], vc[c * CHUNK:(c + 1) * CHUNK, :])

    def kv_body(c, carry):
        r = pl.multiple_of(c * CHUNK, CHUNK)
        kv_scr[N_CHUNK_CTX + c] = kv_of(kl[pl.ds(r, CHUNK), :], vl[pl.ds(r, CHUNK), :])
        return carry

    lax.fori_loop(0, N_CHUNK_LAT, kv_body, 0, unroll=RET_UNROLL)

    def fwd_body(c, s):
        st_scr[c, 0:HEAD_DIM, :] = s.astype(BF16)
        return gchunk_f * s + kv_scr[c, 0:HEAD_DIM, :]

    lax.fori_loop(0, N_CHUNK, fwd_body, zeros)

    def bwd_step(c, s):
        st_scr[c, HEAD_DIM:2 * HEAD_DIM, :] = s.astype(BF16)
        return gchunk_b * s + kv_scr[c, HEAD_DIM:2 * HEAD_DIM, :]

    s = zeros
    for c in reversed(range(N_CHUNK_CTX)):
        s = bwd_step(c, s)
    lax.fori_loop(0, N_CHUNK_LAT, lambda t, s: bwd_step(N_CHUNK - 1 - t, s), s)

    def out_of(q, k, v, g, st):
        sc = lax.dot_general(q, k, (((1,), (1,)), ((), ())), preferred_element_type=F32)
        intra = jnp.dot((sc * decay).astype(BF16), v, preferred_element_type=F32)
        qf = q.astype(F32)
        qq = jnp.concatenate([(qf * head_f).astype(BF16), (qf * head_b).astype(BF16)], axis=1)
        o = intra + jnp.dot(qq, st, preferred_element_type=F32)
        mu = jnp.mean(o, axis=-1, keepdims=True)
        var = jnp.mean(jnp.square(o - mu), axis=-1, keepdims=True)
        on = (o - mu) * lax.rsqrt(var + GN_EPS)
        return (on * gain * _silu(g.astype(F32))).astype(BF16)

    for c in range(N_CHUNK_CTX):
        sl = slice(c * CHUNK, (c + 1) * CHUNK)
        oc[sl, :] = out_of(qc[sl, :], kc[sl, :], vc[sl, :], gc[sl, :], st_scr[c])

    def out_body(c, carry):
        r = pl.multiple_of(c * CHUNK, CHUNK)
        sl = pl.ds(r, CHUNK)
        ol[sl, :] = out_of(ql[sl, :], kl[sl, :], vl[sl, :], gl[sl, :], st_scr[N_CHUNK_CTX + c])
        return carry

    lax.fori_loop(0, N_CHUNK_LAT, out_body, 0, unroll=RET_UNROLL)


def _retention(qkvgp, log_gamma, gn_g):
    def lat(part):
        return pl.BlockSpec((SEQ, HEAD_DIM), lambda b, h: (b, part * RET_HEADS + h))

    def ctx(part):
        return pl.BlockSpec((CTX_LEN, HEAD_DIM), lambda b, h: (N_LAT // CTX_LEN + b, part * RET_HEADS + h))

    return pl.pallas_call(
        _ret_kernel,
        out_shape=(jax.ShapeDtypeStruct((N_LAT, RET_WIDTH), BF16),
                   jax.ShapeDtypeStruct((N_CTX, RET_WIDTH), BF16)),
        grid=(BATCH, RET_HEADS),
        in_specs=[pl.BlockSpec(memory_space=pltpu.SMEM),
                  lat(0), lat(1), lat(2), lat(3), ctx(0), ctx(1), ctx(2), ctx(3),
                  pl.BlockSpec((1, HEAD_DIM), lambda b, h: (0, h))],
        out_specs=(pl.BlockSpec((SEQ, HEAD_DIM), lambda b, h: (b, h)),
                   pl.BlockSpec((CTX_LEN, HEAD_DIM), lambda b, h: (b, h))),
        scratch_shapes=[pltpu.VMEM((N_CHUNK, 2 * HEAD_DIM, HEAD_DIM), F32),
                        pltpu.VMEM((N_CHUNK, 2 * HEAD_DIM, HEAD_DIM), BF16)],
        compiler_params=pltpu.CompilerParams(vmem_limit_bytes=VMEM_LIMIT),
        name="retention",
    )(log_gamma, *([qkvgp] * 8), gn_g)


POOL_BLK = 256


def _split_dot(m, x):
    hi = x.astype(BF16)
    lo = (x - hi.astype(F32)).astype(BF16)
    both = jnp.dot(m, jnp.concatenate([hi, lo], axis=1), preferred_element_type=F32)
    return both[:, :x.shape[1]] + both[:, x.shape[1]:]


def _window_count(idx, w, n):
    return (jnp.minimum(idx - w // 2 + w, n) - jnp.maximum(idx - w // 2, 0)).astype(F32)


def _row_window_mean(s, w):
    size = [min(r - w // 2 + w, GRID_W) - max(r - w // 2, 0) for r in range(GRID_W)]
    parts, r0 = [], 0
    for r in range(1, GRID_W + 1):
        if r == GRID_W or size[r] != size[r0]:
            parts.append(s[r0 * GRID_W:r * GRID_W] * (1.0 / size[r0]))
            r0 = r
    return jnp.concatenate(parts, axis=0)


def _pool_kernel(pl_ref, pc_ref, w_ref, scale_ref, ol, oc):
    ti = lax.broadcasted_iota(jnp.int32, (POOL_BLK, POOL_BLK), 0)
    tj = lax.broadcasted_iota(jnp.int32, (POOL_BLK, POOL_BLK), 1)
    tok_b = lax.broadcasted_iota(jnp.int32, (POOL_BLK, POOL_CH), 0)
    tok_c = lax.broadcasted_iota(jnp.int32, (CTX_LEN, POOL_CH), 0)
    for gi, w in enumerate(POOL_WINDOWS):
        cols = slice(gi * POOL_CH, (gi + 1) * POOL_CH)
        lo_off, hi_off = -(w // 2), w - 1 - w // 2
        wmat = w_ref[gi]
        scale = scale_ref[:, cols]

        u = pl_ref[:, cols].astype(F32)
        pad = jnp.zeros((w // 2 * GRID_W, POOL_CH), F32)
        s = jnp.concatenate([pad, u, pad], axis=0)
        span = 1
        while span < w:
            n = s.shape[0] - span * GRID_W
            s = s[:n] + s[span * GRID_W:]
            span *= 2
        row_mean = _row_window_mean(s[:SEQ], w)
        ci, cj = ti & (GRID_W - 1), tj & (GRID_W - 1)
        band = ((ti >> GRID_SHIFT == tj >> GRID_SHIFT) & (cj >= ci + lo_off) & (cj <= ci + hi_off))
        band = jnp.where(band, 1.0, 0.0).astype(BF16)
        col_inv = 1.0 / _window_count(tok_b & (GRID_W - 1), w, GRID_W)
        for blk in range(SEQ // POOL_BLK):
            rs = slice(blk * POOL_BLK, (blk + 1) * POOL_BLK)
            m = _split_dot(band, row_mean[rs]) * col_inv
            d = (m - u[rs]).astype(BF16)
            y = jnp.dot(d, wmat, preferred_element_type=F32) * scale
            ol[rs, cols] = y.astype(BF16)

        uc = pc_ref[:, cols].astype(F32)
        band_c = jnp.where((tj >= ti + lo_off) & (tj <= ti + hi_off), 1.0, 0.0).astype(BF16)
        mc = _split_dot(band_c, uc) / _window_count(tok_c, w, CTX_LEN)
        dc = (mc - uc).astype(BF16)
        oc[:, cols] = (jnp.dot(dc, wmat, preferred_element_type=F32) * scale).astype(BF16)


def _pool(qkvgp, pool_w, pool_scale):
    pcol = 4 * RET_WIDTH // POOL_WIDTH
    return pl.pallas_call(
        _pool_kernel,
        out_shape=(jax.ShapeDtypeStruct((N_LAT, POOL_WIDTH), BF16),
                   jax.ShapeDtypeStruct((N_CTX, POOL_WIDTH), BF16)),
        grid=(BATCH,),
        in_specs=[pl.BlockSpec((SEQ, POOL_WIDTH), lambda b: (b, pcol)),
                  pl.BlockSpec((CTX_LEN, POOL_WIDTH), lambda b: (N_LAT // CTX_LEN + b, pcol)),
                  pl.BlockSpec((len(POOL_WINDOWS), POOL_CH, POOL_CH), lambda b: (0, 0, 0)),
                  pl.BlockSpec((1, POOL_WIDTH), lambda b: (0, 0))],
        out_specs=(pl.BlockSpec((SEQ, POOL_WIDTH), lambda b: (b, 0)),
                   pl.BlockSpec((CTX_LEN, POOL_WIDTH), lambda b: (b, 0))),
        compiler_params=pltpu.CompilerParams(vmem_limit_bytes=VMEM_LIMIT),
        name="pool",
    )(qkvgp, qkvgp, pool_w, pool_scale)


def _mix_out(i, rl, rc, pl_, pc, w_ref, xl_ref, xc_ref, gt_ref):
    is_lat = i < LAT_TILES
    ret = jnp.where(is_lat, rl[...], rc[...])
    pool = jnp.where(is_lat, pl_[...], pc[...])
    y = (jnp.dot(ret, w_ref[0:RET_WIDTH, :], preferred_element_type=F32)
         + jnp.dot(pool, w_ref[RET_WIDTH:, :], preferred_element_type=F32))
    return _x_tile(xl_ref, xc_ref) + gt_ref[...] * y


def _outproj_router_kernel(rl, rc, pl_, pc, w_ref, xl_ref, xc_ref, gt_ref, g_ref, sh_ref, sc_ref, wr_ref,
                           xo_ref, h_ref, route_ref, cnt_ref, carry_scr):
    i = pl.program_id(0)
    x = _mix_out(i, rl, rc, pl_, pc, w_ref, xl_ref, xc_ref, gt_ref)
    xo_ref[...] = x
    h = _rms_mod(x, g_ref[...], sh_ref[...], sc_ref[...])
    h_ref[...] = h.reshape(TM, *ROW_TILE)
    logits = jnp.dot(h.astype(BF16), wr_ref[...], preferred_element_type=F32)
    lane = lax.broadcasted_iota(jnp.int32, logits.shape, 1)
    neg = jnp.float32(-jnp.inf)
    logits = jnp.where(lane < N_EXPERTS, logits, neg)
    m1 = jnp.max(logits, axis=-1, keepdims=True)
    i1 = jnp.min(jnp.where(logits == m1, lane, 128), axis=-1, keepdims=True)
    rest = jnp.where(lane == i1, neg, logits)
    m2 = jnp.max(rest, axis=-1, keepdims=True)
    i2 = jnp.min(jnp.where(rest == m2, lane, 128), axis=-1, keepdims=True)
    e2 = jnp.exp(m2 - m1)
    den = 1.0 + e2

    @pl.when(i == 0)
    def _():
        carry_scr[...] = jnp.zeros_like(carry_scr)

    sel1, sel2 = lane == i1, lane == i2
    picks = jnp.where(sel1 | sel2, 1.0, 0.0)
    ti = lax.broadcasted_iota(jnp.int32, (TM, TM), 0)
    tj = lax.broadcasted_iota(jnp.int32, (TM, TM), 1)
    earlier = jnp.where(tj < ti, 1.0, 0.0).astype(BF16)
    before = jnp.dot(earlier, picks.astype(BF16), preferred_element_type=F32) + carry_scr[...]
    r1 = jnp.sum(jnp.where(sel1, before, 0.0), axis=-1, keepdims=True)
    r2 = jnp.sum(jnp.where(sel2, before, 0.0), axis=-1, keepdims=True)
    carry = carry_scr[...] + jnp.sum(picks, axis=0, keepdims=True)
    carry_scr[...] = carry
    cnt_ref[...] = jnp.broadcast_to(carry, cnt_ref.shape)

    cols = (i1.astype(F32), i2.astype(F32), r1, r2, 1.0 / den, e2 / den)
    route = jnp.zeros(logits.shape, F32)
    for k, v in enumerate(cols):
        route = jnp.where(lane == k, v, route)
    route_ref[...] = route


def _mixer_specs(ctx_off, layer):
    def lat_spec():
        return pl.BlockSpec((TM, RET_WIDTH), lambda i: (jnp.minimum(i, LAT_TILES - 1), 0))

    def ctx_spec():
        return pl.BlockSpec((TM, RET_WIDTH), lambda i: (jnp.maximum(i - LAT_TILES, 0), 0))

    return [lat_spec(), ctx_spec(), lat_spec(), ctx_spec(),
            pl.BlockSpec((D_MODEL, D_MODEL), lambda i: (0, 0), pipeline_mode=pl.Buffered(1)),
            *_x_specs(ctx_off), _mod_spec(layer, 2),
            pl.BlockSpec((1, D_MODEL), lambda i: (0, 0)),
            _mod_spec(layer, 3), _mod_spec(layer, 4)]


def _outproj_router(ret_l, ret_c, pool_l, pool_c, w_out, x_lat, x_ctx, ctx_off, mod5, norm_g, router_w, layer,
                    n_tiles):
    row = pl.BlockSpec((TM, D_MODEL), lambda i: (i, 0))
    n_rows = n_tiles * TM
    return pl.pallas_call(
        _outproj_router_kernel,
        out_shape=(jax.ShapeDtypeStruct((n_rows, D_MODEL), F32), jax.ShapeDtypeStruct((n_rows, *ROW_TILE), F32),
                   jax.ShapeDtypeStruct((n_rows, 128), F32), jax.ShapeDtypeStruct((8, 128), F32)),
        grid=(n_tiles,),
        in_specs=_mixer_specs(ctx_off, layer) + [pl.BlockSpec((D_MODEL, 128), lambda i: (0, 0))],
        out_specs=(row, pl.BlockSpec((TM, *ROW_TILE), lambda i: (i, 0, 0)),
                   pl.BlockSpec((TM, 128), lambda i: (i, 0)), pl.BlockSpec((8, 128), lambda i: (0, 0))),
        scratch_shapes=[pltpu.VMEM((1, 128), F32)],
        compiler_params=pltpu.CompilerParams(dimension_semantics=("arbitrary",), vmem_limit_bytes=VMEM_LIMIT),
        name="outproj",
    )(ret_l, ret_c, pool_l, pool_c, w_out, x_lat, x_ctx, mod5, norm_g, mod5, mod5, router_w)


def _swiglu(h, w13_ref, w2_ref, act_scr):
    for j in range(D_FF // FF_CHUNK):
        u = jnp.dot(h, w13_ref[0, :, j * FF_CHUNK:(j + 1) * FF_CHUNK], preferred_element_type=F32)
        g = jnp.dot(h, w13_ref[0, :, D_FF + j * FF_CHUNK:D_FF + (j + 1) * FF_CHUNK], preferred_element_type=F32)
        act_scr[:, j * FF_CHUNK:(j + 1) * FF_CHUNK] = (_silu(g) * u).astype(BF16)
    return jnp.dot(act_scr[...], w2_ref[0], preferred_element_type=F32)


def _outproj_ffn_kernel(rl, rc, pl_, pc, w_ref, xl_ref, xc_ref, gt1_ref, g_ref, sh_ref, sc_ref, gt2_ref,
                        w13_ref, w2_ref, o_ref, act_scr):
    x = _mix_out(pl.program_id(0), rl, rc, pl_, pc, w_ref, xl_ref, xc_ref, gt1_ref)
    h = _rms_mod(x, g_ref[...], sh_ref[...], sc_ref[...]).astype(BF16)
    o_ref[...] = x + gt2_ref[...] * _swiglu(h, w13_ref, w2_ref, act_scr)


def _outproj_ffn(ret_l, ret_c, pool_l, pool_c, w_out, x_lat, x_ctx, ctx_off, mod5, norm_g, w13, w2, layer, n_tiles):
    resident = pl.Buffered(1)
    return pl.pallas_call(
        _outproj_ffn_kernel,
        out_shape=jax.ShapeDtypeStruct((n_tiles * TM, D_MODEL), F32),
        grid=(n_tiles,),
        in_specs=_mixer_specs(ctx_off, layer) + [
            _mod_spec(layer, 5),
            pl.BlockSpec((1, D_MODEL, 2 * D_FF), lambda i: (0, 0, 0), pipeline_mode=resident),
            pl.BlockSpec((1, D_FF, D_MODEL), lambda i: (0, 0, 0), pipeline_mode=resident)],
        out_specs=pl.BlockSpec((TM, D_MODEL), lambda i: (i, 0)),
        scratch_shapes=[pltpu.VMEM((TM, D_FF), BF16)],
        compiler_params=pltpu.CompilerParams(vmem_limit_bytes=VMEM_LIMIT),
        name="outproj_ffn",
    )(ret_l, ret_c, pool_l, pool_c, w_out, x_lat, x_ctx, mod5, norm_g, mod5, mod5, mod5, w13, w2)


def _moe_plan(counts, n_steps):
    cnt = counts.astype(jnp.int32)
    end = jnp.cumsum(cnt)
    start = end - cnt
    first = start // TM
    visits = jnp.where(cnt > 0, (end - 1) // TM - first + 1, 0)
    visit_end = jnp.cumsum(visits)
    visit_start = visit_end - visits
    total = visit_end[-1]
    step = jnp.minimum(jnp.arange(n_steps, dtype=jnp.int32), total - 1)
    eid = jnp.minimum(jnp.sum(step[:, None] >= visit_end[None, :], axis=1), N_EXPERTS - 1).astype(jnp.int32)
    tile = first[eid] + step - visit_start[eid]
    lo = jnp.clip(start[eid] - tile * TM, 0, TM)
    hi = jnp.clip(end[eid] - tile * TM, 0, TM)
    hi = jnp.where(jnp.arange(n_steps) < total, hi, lo)
    return start, tile.astype(jnp.int32), eid, lo.astype(jnp.int32), hi.astype(jnp.int32)


def _row_copy_wait(src, dst, sem):
    pltpu.make_async_copy(src, dst, sem).wait()


def _dispatch_kernel(pos_ref, h_ref, xs_hbm, sem):
    i = pl.program_id(0)
    n = 0
    for r in range(TM):
        for k in range(2):
            p = pos_ref[i, k * TM + r]
            pltpu.make_async_copy(h_ref.at[r], xs_hbm.at[p], sem).start(priority=n % 2)
            n += 1
    for k in range(2):
        _row_copy_wait(h_ref, xs_hbm.at[pl.ds(0, TM)], sem)


def _dispatch(pos, h, n_tiles):
    return pl.pallas_call(
        _dispatch_kernel,
        out_shape=jax.ShapeDtypeStruct((2 * n_tiles * TM, *ROW_TILE), F32),
        grid_spec=pltpu.PrefetchScalarGridSpec(
            num_scalar_prefetch=1,
            grid=(n_tiles,),
            in_specs=[pl.BlockSpec((TM, *ROW_TILE), lambda i, pos: (i, 0, 0))],
            out_specs=pl.BlockSpec(memory_space=pl.ANY),
            scratch_shapes=[pltpu.SemaphoreType.DMA]),
        name="dispatch",
    )(pos, h)


def _expert_ffn_kernel(tile_ref, eid_ref, lo_ref, hi_ref, x_ref, w13_ref, w2_ref, o_ref, act_scr):
    s = pl.program_id(0)
    lo, hi = lo_ref[s], hi_ref[s]

    @pl.when(hi > lo)
    def _():
        row = lax.broadcasted_iota(jnp.int32, (TM, D_MODEL), 0)
        x = x_ref[...].reshape(TM, D_MODEL)
        h = jnp.where((row >= lo) & (row < hi), x, 0.0).astype(BF16)
        y = _swiglu(h, w13_ref, w2_ref, act_scr).reshape(TM, *ROW_TILE)

        @pl.when(lo == 0)
        def _():
            o_ref[...] = y

        @pl.when(lo > 0)
        def _():
            o_ref[...] += y


def _expert_ffn(plan, xs, w13, w2, moe_layer, n_steps):
    _, tile, eid, lo, hi = plan
    row = pl.BlockSpec((TM, *ROW_TILE), lambda s, tile, eid, lo, hi: (tile[s], 0, 0))
    return pl.pallas_call(
        _expert_ffn_kernel,
        out_shape=jax.ShapeDtypeStruct(xs.shape, F32),
        grid_spec=pltpu.PrefetchScalarGridSpec(
            num_scalar_prefetch=4,
            grid=(n_steps,),
            in_specs=[row,
                      pl.BlockSpec((None, 1, D_MODEL, 2 * D_FF),
                                   lambda s, tile, eid, lo, hi: (moe_layer, eid[s], 0, 0)),
                      pl.BlockSpec((None, 1, D_FF, D_MODEL),
                                   lambda s, tile, eid, lo, hi: (moe_layer, eid[s], 0, 0))],
            out_specs=row,
            scratch_shapes=[pltpu.VMEM((TM, D_FF), BF16)]),
        compiler_params=pltpu.CompilerParams(vmem_limit_bytes=VMEM_LIMIT),
        name="expert_ffn",
    )(tile, eid, lo, hi, xs, w13, w2)


def _combine_kernel(pos_ref, y_hbm, route_ref, x_ref, gt_ref, g_ref, o_ref, ybuf, sem, *, n_tiles, final_norm):
    i = pl.program_id(0)
    slot = i % 2

    def request(tile, into):
        n = 0
        for r in range(TM):
            for k in range(2):
                p = pos_ref[tile, k * TM + r]
                pltpu.make_async_copy(y_hbm.at[p], ybuf.at[into, k, r], sem.at[into]).start(priority=n % 2)
                n += 1

    @pl.when(i == 0)
    def _():
        request(0, 0)

    @pl.when(i + 1 < n_tiles)
    def _():
        request(i + 1, 1 - slot)

    for k in range(2):
        _row_copy_wait(y_hbm.at[pl.ds(0, TM)], ybuf.at[slot, k], sem.at[slot])
    w1 = route_ref[:, 4:5]
    w2 = route_ref[:, 5:6]
    y1 = ybuf[slot, 0].reshape(TM, D_MODEL)
    y2 = ybuf[slot, 1].reshape(TM, D_MODEL)
    x = x_ref[...] + gt_ref[...] * (w1 * y1 + w2 * y2)
    if final_norm:
        ms = jnp.mean(x * x, axis=-1, keepdims=True)
        x = x * lax.rsqrt(ms + NORM_EPS) * g_ref[...]
    o_ref[...] = x


def _combine(pos, y, route, x_all, mod5, final_g, layer, n_tiles, final_norm):
    row = pl.BlockSpec((TM, D_MODEL), lambda i, pos: (i, 0))
    return pl.pallas_call(
        functools.partial(_combine_kernel, n_tiles=n_tiles, final_norm=final_norm),
        out_shape=jax.ShapeDtypeStruct((n_tiles * TM, D_MODEL), F32),
        grid_spec=pltpu.PrefetchScalarGridSpec(
            num_scalar_prefetch=1,
            grid=(n_tiles,),
            in_specs=[pl.BlockSpec(memory_space=pl.ANY),
                      pl.BlockSpec((TM, 128), lambda i, pos: (i, 0)), row, _mod_spec(layer, 5),
                      pl.BlockSpec((1, D_MODEL), lambda i, pos: (0, 0))],
            out_specs=row,
            scratch_shapes=[pltpu.VMEM((2, 2, TM, *ROW_TILE), F32), pltpu.SemaphoreType.DMA((2,))]),
        compiler_params=pltpu.CompilerParams(dimension_semantics=("arbitrary",), vmem_limit_bytes=VMEM_LIMIT),
        name="combine",
    )(pos, y, route, x_all, mod5, final_g)


def _moe(h, route, counts, x_all, mod5, w13, w2, final_g, layer, n_tiles, final_norm):
    n_steps = 2 * n_tiles + N_EXPERTS
    plan = _moe_plan(counts[0, :N_EXPERTS], n_steps)
    meta = route[:, :4].astype(jnp.int32)
    pos = plan[0][meta[:, 0:2]] + meta[:, 2:4]
    pos = pos.reshape(n_tiles, TM, 2).transpose(0, 2, 1).reshape(n_tiles, 2 * TM)
    xs = _dispatch(pos, h, n_tiles)
    ys = _expert_ffn(plan, xs, w13, w2, layer // 2, n_steps)
    return _combine(pos, ys, route, x_all, mod5, final_g, layer, n_tiles, final_norm)


def kernel(x, c, ctx, c_ctx, w_ada, b_ada, norm1_g, norm2_g, w_in, ret_decay_logit, ret_gn_g, pool_w, pool_scale,
           w_out, ffn_w13, ffn_w2, router_w, moe_w13, moe_w2, final_norm_g):
    assert DEPTH % 2 == 0, "the final norm is fused into the last routed-expert combine"
    c_all = jnp.concatenate([c, c_ctx[None, :], jnp.zeros((MOD_ROWS - BATCH - 1, D_MODEL), F32)], axis=0)
    mod5 = _ada_mod(c_all, w_ada, b_ada).reshape(DEPTH, MOD_ROWS, 6, 1, D_MODEL)
    tables = _rope_tables()
    log_gamma = jax.nn.log_sigmoid(ret_decay_logit.astype(F32))
    router_pad = jnp.pad(router_w, ((0, 0), (0, 0), (0, 128 - N_EXPERTS))).astype(BF16)
    moe_w13_b, moe_w2_b = moe_w13.astype(BF16), moe_w2.astype(BF16)
    final_g = final_norm_g[None, :]

    x_lat, x_ctx, ctx_off = x.reshape(N_LAT, D_MODEL), ctx.reshape(N_CTX, D_MODEL), 0
    for l in range(DEPTH):
        last = l == DEPTH - 1
        n_tiles = LAT_TILES if last else LAT_TILES + CTX_TILES
        qkvgp = _inproj(x_lat, x_ctx, ctx_off, norm1_g[l][None, :], mod5, w_in[l].astype(BF16), tables, l,
                        LAT_TILES + CTX_TILES)
        ret_l, ret_c = _retention(qkvgp, log_gamma[l], ret_gn_g[l][None, :])
        pool_l, pool_c = _pool(qkvgp, pool_w[l].astype(BF16), pool_scale[l][None, :])
        i = l // 2
        if l % 2 == 0:
            x_all = _outproj_ffn(ret_l, ret_c, pool_l, pool_c, w_out[l].astype(BF16), x_lat, x_ctx, ctx_off, mod5,
                                 norm2_g[l][None, :], ffn_w13[i][None].astype(BF16), ffn_w2[i][None].astype(BF16),
                                 l, n_tiles)
        else:
            x_all, h2, route, counts = _outproj_router(ret_l, ret_c, pool_l, pool_c, w_out[l].astype(BF16), x_lat,
                                                       x_ctx, ctx_off, mod5, norm2_g[l][None, :], router_pad[i],
                                                       l, n_tiles)
            x_all = _moe(h2, route, counts, x_all, mod5, moe_w13_b, moe_w2_b, final_g, l, n_tiles, last)
        x_lat, x_ctx, ctx_off = x_all, x_all, LAT_TILES
    return x_all.reshape(BATCH, SEQ, D_MODEL)
```

```python
import functools

import jax
import jax.numpy as jnp
from jax import lax
from jax.experimental import pallas as pl
from jax.experimental.pallas import tpu as pltpu

F32 = jnp.float32
BF16 = jnp.bfloat16

D_MODEL = 1024
BATCH = 8
SEQ = 4096
DEPTH = 4
GRID_W = 64
GRID_SHIFT = 6
CTX_LEN = 256
RET_WIDTH = 512
POOL_WIDTH = 512
RET_HEADS = 4
HEAD_DIM = 128
CHUNK = 128
ROPE_BASE = 10000.0
POOL_WINDOWS = (2, 4, 8, 16)
POOL_CH = 128
IN_COLS = 4 * RET_WIDTH + POOL_WIDTH
D_FF = 2816
N_EXPERTS = 8
NORM_EPS = 1e-6
GN_EPS = 1e-5

N_LAT = BATCH * SEQ
N_CTX = BATCH * CTX_LEN
N_ROWS = N_LAT + N_CTX
MOD_ROWS = 16
CTX_MOD_ROW = BATCH

TM = 512
LAT_TILES = N_LAT // TM
CTX_TILES = N_CTX // TM
TILES_PER_BATCH = SEQ // TM
FF_CHUNK = 256
LANES = 128
ROW_TILE = (D_MODEL // LANES, LANES)
VMEM_LIMIT = 56 * 1024 * 1024


def _mod_row(i):
    return jnp.where(i < LAT_TILES, i // TILES_PER_BATCH, CTX_MOD_ROW)


def _mod_spec(layer, part):
    return pl.BlockSpec((None, None, None, 1, D_MODEL),
                        lambda i, *_: (layer, _mod_row(i), part, 0, 0))


def _silu(v):
    return v * jax.nn.sigmoid(v)


def _x_specs(ctx_block_offset):
    return [pl.BlockSpec((TM, D_MODEL), lambda i, *_: (jnp.minimum(i, LAT_TILES - 1), 0)),
            pl.BlockSpec((TM, D_MODEL), lambda i, *_: (ctx_block_offset + jnp.maximum(i - LAT_TILES, 0), 0))]


def _x_tile(xl_ref, xc_ref):
    return jnp.where(pl.program_id(0) < LAT_TILES, xl_ref[...], xc_ref[...])


def _rms_mod(x, gain, shift, scale):
    ms = jnp.mean(x * x, axis=-1, keepdims=True)
    y = x * lax.rsqrt(ms + NORM_EPS) * gain
    return y * (1.0 + scale) + shift


def _ada_kernel(c_ref, w_ref, b_ref, o_ref):
    s = _silu(c_ref[...])
    o_ref[0] = jnp.dot(s.astype(BF16), w_ref[0].astype(BF16), preferred_element_type=F32) + b_ref[0]


def _ada_mod(c_all, w_ada, b_ada):
    tn = 1536
    return pl.pallas_call(
        _ada_kernel,
        out_shape=jax.ShapeDtypeStruct((DEPTH, MOD_ROWS, 6 * D_MODEL), F32),
        grid=(DEPTH, 6 * D_MODEL // tn),
        in_specs=[pl.BlockSpec((MOD_ROWS, D_MODEL), lambda l, n: (0, 0)),
                  pl.BlockSpec((1, D_MODEL, tn), lambda l, n: (l, 0, n)),
                  pl.BlockSpec((1, 1, tn), lambda l, n: (l, 0, n))],
        out_specs=pl.BlockSpec((1, MOD_ROWS, tn), lambda l, n: (l, 0, n)),
        compiler_params=pltpu.CompilerParams(vmem_limit_bytes=VMEM_LIMIT),
        name="ada_mod",
    )(c_all, w_ada, b_ada.reshape(DEPTH, 1, 6 * D_MODEL))


def _inproj_kernel(xl_ref, xc_ref, g_ref, sh_ref, sc_ref, w_ref, cos_ref, sa_ref, sb_ref, o_ref):
    h = _rms_mod(_x_tile(xl_ref, xc_ref), g_ref[...], sh_ref[...], sc_ref[...]).astype(BF16)
    _project_in(h, w_ref, cos_ref, sa_ref, sb_ref, o_ref)


def _project_in(h, w_ref, cos_ref, sa_ref, sb_ref, o_ref):
    cos, sa, sb = cos_ref[...], sa_ref[...], sb_ref[...]
    k_scale = HEAD_DIM ** -0.5
    for part, mul in ((0, 1.0), (1, k_scale)):
        z = jnp.dot(h, w_ref[:, part * RET_WIDTH:(part + 1) * RET_WIDTH], preferred_element_type=F32)
        for hh in range(RET_HEADS):
            t = z[:, hh * HEAD_DIM:(hh + 1) * HEAD_DIM]
            r = t * cos + pltpu.roll(t, 96, axis=1) * sa + pltpu.roll(t, 32, axis=1) * sb
            if mul != 1.0:
                r = r * mul
            col = part * RET_WIDTH + hh * HEAD_DIM
            o_ref[:, col:col + HEAD_DIM] = r.astype(BF16)
    z = jnp.dot(h, w_ref[:, 2 * RET_WIDTH:], preferred_element_type=F32)
    o_ref[:, 2 * RET_WIDTH:] = z.astype(BF16)


def _rope_tables():
    half = HEAD_DIM // 2
    inv_freq = ROPE_BASE ** (-jnp.arange(0, half, 2, dtype=F32) / half)
    t = jnp.arange(SEQ)
    rows, cols = (t // GRID_W).astype(F32), (t % GRID_W).astype(F32)
    ang_r = rows[:, None] * inv_freq[None, :]
    ang_c = cols[:, None] * inv_freq[None, :]
    zero = jnp.zeros_like(ang_r)
    cos = jnp.concatenate([jnp.cos(ang_r), jnp.cos(ang_r), jnp.cos(ang_c), jnp.cos(ang_c)], axis=1)
    sa = jnp.concatenate([-jnp.sin(ang_r), zero, -jnp.sin(ang_c), zero], axis=1)
    sb = jnp.concatenate([zero, jnp.sin(ang_r), zero, jnp.sin(ang_c)], axis=1)
    ident = jnp.ones((TM, HEAD_DIM), F32)
    none = jnp.zeros((TM, HEAD_DIM), F32)
    return (jnp.concatenate([cos, ident]), jnp.concatenate([sa, none]), jnp.concatenate([sb, none]))


def _inproj(x_lat, x_ctx, ctx_off, norm_g, mod5, w_in, tables, layer, n_tiles):
    tab_spec = pl.BlockSpec((TM, HEAD_DIM),
                            lambda i: (jnp.where(i < LAT_TILES, i % TILES_PER_BATCH, TILES_PER_BATCH), 0))
    return pl.pallas_call(
        _inproj_kernel,
        out_shape=jax.ShapeDtypeStruct((N_ROWS, IN_COLS), BF16),
        grid=(n_tiles,),
        in_specs=_x_specs(ctx_off) + [
            pl.BlockSpec((1, D_MODEL), lambda i: (0, 0)),
            _mod_spec(layer, 0), _mod_spec(layer, 1),
            pl.BlockSpec((D_MODEL, IN_COLS), lambda i: (0, 0)),
            tab_spec, tab_spec, tab_spec],
        out_specs=pl.BlockSpec((TM, IN_COLS), lambda i: (i, 0)),
        compiler_params=pltpu.CompilerParams(vmem_limit_bytes=VMEM_LIMIT),
        name="inproj",
    )(x_lat, x_ctx, norm_g, mod5, mod5, w_in, *tables)


N_CHUNK_LAT = SEQ // CHUNK
N_CHUNK_CTX = CTX_LEN // CHUNK
N_CHUNK = N_CHUNK_LAT + N_CHUNK_CTX
RET_UNROLL = 32


def _ret_kernel(lg_ref, ql, kl, vl, gl, qc, kc, vc, gc, gain_ref, ol, oc, kv_scr, st_scr):
    hd = pl.program_id(1)
    lgf = lg_ref[0, hd]
    lgb = lg_ref[1, hd]
    pi = lax.broadcasted_iota(jnp.int32, (CHUNK, CHUNK), 0).astype(F32)
    pj = lax.broadcasted_iota(jnp.int32, (CHUNK, CHUNK), 1).astype(F32)
    tail_f = jnp.exp(lgf * (CHUNK - 1.0 - pi))
    tail_b = jnp.exp(lgb * pi)
    head_f = jnp.exp(lgf * (pi + 1.0))
    head_b = jnp.exp(lgb * (CHUNK - pi))
    dif = pi - pj
    decay = jnp.where(dif >= 0, jnp.exp(lgf * jnp.maximum(dif, 0.0)), jnp.exp(lgb * jnp.maximum(-dif, 0.0)))
    zeros = jnp.zeros((CHUNK, CHUNK), F32)
    gchunk_f = jnp.exp(zeros + lgf * CHUNK)
    gchunk_b = jnp.exp(zeros + lgb * CHUNK)
    gain = gain_ref[...]

    def kv_of(k, v):
        kf = k.astype(F32)
        kk = jnp.concatenate([(kf * tail_f).astype(BF16), (kf * tail_b).astype(BF16)], axis=1)
        return lax.dot_general(kk, v, (((0,), (0,)), ((), ())), preferred_element_type=F32)

    for c in range(N_CHUNK_CTX):
        kv_scr[c] = kv_of(kc[c * CHUNK:(c + 1) * CHUNK, :], vc[c * CHUNK:(c + 1) * CHUNK, :])

    def kv_body(c, carry):
        r = pl.multiple_of(c * CHUNK, CHUNK)
        kv_scr[N_CHUNK_CTX + c] = kv_of(kl[pl.ds(r, CHUNK), :], vl[pl.ds(r, CHUNK), :])
        return carry

    lax.fori_loop(0, N_CHUNK_LAT, kv_body, 0, unroll=RET_UNROLL)

    def fwd_body(c, s):
        st_scr[c, 0:HEAD_DIM, :] = s.astype(BF16)
        return gchunk_f * s + kv_scr[c, 0:HEAD_DIM, :]

    lax.fori_loop(0, N_CHUNK, fwd_body, zeros)

    def bwd_step(c, s):
        st_scr[c, HEAD_DIM:2 * HEAD_DIM, :] = s.astype(BF16)
        return gchunk_b * s + kv_scr[c, HEAD_DIM:2 * HEAD_DIM, :]

    s = zeros
    for c in reversed(range(N_CHUNK_CTX)):
        s = bwd_step(c, s)
    lax.fori_loop(0, N_CHUNK_LAT, lambda t, s: bwd_step(N_CHUNK - 1 - t, s), s)

    def out_of(q, k, v, g, st):
        sc = lax.dot_general(q, k, (((1,), (1,)), ((), ())), preferred_element_type=F32)
        intra = jnp.dot((sc * decay).astype(BF16), v, preferred_element_type=F32)
        qf = q.astype(F32)
        qq = jnp.concatenate([(qf * head_f).astype(BF16), (qf * head_b).astype(BF16)], axis=1)
        o = intra + jnp.dot(qq, st, preferred_element_type=F32)
        mu = jnp.mean(o, axis=-1, keepdims=True)
        var = jnp.mean(jnp.square(o - mu), axis=-1, keepdims=True)
        on = (o - mu) * lax.rsqrt(var + GN_EPS)
        return (on * gain * _silu(g.astype(F32))).astype(BF16)

    for c in range(N_CHUNK_CTX):
        sl = slice(c * CHUNK, (c + 1) * CHUNK)
        oc[sl, :] = out_of(qc[sl, :], kc[sl, :], vc[sl, :], gc[sl, :], st_scr[c])

    def out_body(c, carry):
        r = pl.multiple_of(c * CHUNK, CHUNK)
        sl = pl.ds(r, CHUNK)
        ol[sl, :] = out_of(ql[sl, :], kl[sl, :], vl[sl, :], gl[sl, :], st_scr[N_CHUNK_CTX + c])
        return carry

    lax.fori_loop(0, N_CHUNK_LAT, out_body, 0, unroll=RET_UNROLL)


def _retention(qkvgp, log_gamma, gn_g):
    def lat(part):
        return pl.BlockSpec((SEQ, HEAD_DIM), lambda b, h: (b, part * RET_HEADS + h))

    def ctx(part):
        return pl.BlockSpec((CTX_LEN, HEAD_DIM), lambda b, h: (N_LAT // CTX_LEN + b, part * RET_HEADS + h))

    return pl.pallas_call(
        _ret_kernel,
        out_shape=(jax.ShapeDtypeStruct((N_LAT, RET_WIDTH), BF16),
                   jax.ShapeDtypeStruct((N_CTX, RET_WIDTH), BF16)),
        grid=(BATCH, RET_HEADS),
        in_specs=[pl.BlockSpec(memory_space=pltpu.SMEM),
                  lat(0), lat(1), lat(2), lat(3), ctx(0), ctx(1), ctx(2), ctx(3),
                  pl.BlockSpec((1, HEAD_DIM), lambda b, h: (0, h))],
        out_specs=(pl.BlockSpec((SEQ, HEAD_DIM), lambda b, h: (b, h)),
                   pl.BlockSpec((CTX_LEN, HEAD_DIM), lambda b, h: (b, h))),
        scratch_shapes=[pltpu.VMEM((N_CHUNK, 2 * HEAD_DIM, HEAD_DIM), F32),
                        pltpu.VMEM((N_CHUNK, 2 * HEAD_DIM, HEAD_DIM), BF16)],
        compiler_params=pltpu.CompilerParams(vmem_limit_bytes=VMEM_LIMIT),
        name="retention",
    )(log_gamma, *([qkvgp] * 8), gn_g)


POOL_BLK = 256


def _window_count(idx, w, n):
    return (jnp.minimum(idx - w // 2 + w, n) - jnp.maximum(idx - w // 2, 0)).astype(F32)


def _row_window_mean(s, w):
    size = [min(r - w // 2 + w, GRID_W) - max(r - w // 2, 0) for r in range(GRID_W)]
    parts, r0 = [], 0
    for r in range(1, GRID_W + 1):
        if r == GRID_W or size[r] != size[r0]:
            parts.append(s[r0 * GRID_W:r * GRID_W] * (1.0 / size[r0]))
            r0 = r
    return jnp.concatenate(parts, axis=0)


def _pool_kernel(pl_ref, pc_ref, w_ref, scale_ref, ol, oc):
    ti = lax.broadcasted_iota(jnp.int32, (POOL_BLK, POOL_BLK), 0)
    tj = lax.broadcasted_iota(jnp.int32, (POOL_BLK, POOL_BLK), 1)
    tok_b = lax.broadcasted_iota(jnp.int32, (POOL_BLK, POOL_CH), 0)
    tok_c = lax.broadcasted_iota(jnp.int32, (CTX_LEN, POOL_CH), 0)
    for gi, w in enumerate(POOL_WINDOWS):
        cols = slice(gi * POOL_CH, (gi + 1) * POOL_CH)
        lo_off, hi_off = -(w // 2), w - 1 - w // 2
        wmat = w_ref[gi]
        scale = scale_ref[:, cols]

        ci, cj = ti & (GRID_W - 1), tj & (GRID_W - 1)
        band = ((ti >> GRID_SHIFT == tj >> GRID_SHIFT) & (cj >= ci + lo_off) & (cj <= ci + hi_off))
        band = jnp.where(band, 1.0, 0.0).astype(BF16)
        col_inv = 1.0 / _window_count(tok_b & (GRID_W - 1), w, GRID_W)
        col_mean = [jnp.dot(band, pl_ref[blk * POOL_BLK:(blk + 1) * POOL_BLK, cols],
                            preferred_element_type=F32) * col_inv for blk in range(SEQ // POOL_BLK)]
        pad = jnp.zeros((w // 2 * GRID_W, POOL_CH), F32)
        s = jnp.concatenate([pad] + col_mean + [pad], axis=0)
        span = 1
        while span < w:
            n = s.shape[0] - span * GRID_W
            s = s[:n] + s[span * GRID_W:]
            span *= 2
        box_mean = _row_window_mean(s[:SEQ], w)
        for blk in range(SEQ // POOL_BLK):
            rs = slice(blk * POOL_BLK, (blk + 1) * POOL_BLK)
            d = (box_mean[rs] - pl_ref[rs, cols].astype(F32)).astype(BF16)
            y = jnp.dot(d, wmat, preferred_element_type=F32) * scale
            ol[rs, cols] = y.astype(BF16)

        band_c = jnp.where((tj >= ti + lo_off) & (tj <= ti + hi_off), 1.0, 0.0).astype(BF16)
        mc = jnp.dot(band_c, pc_ref[:, cols], preferred_element_type=F32) / _window_count(tok_c, w, CTX_LEN)
        uc = pc_ref[:, cols].astype(F32)
        dc = (mc - uc).astype(BF16)
        oc[:, cols] = (jnp.dot(dc, wmat, preferred_element_type=F32) * scale).astype(BF16)


def _pool(qkvgp, pool_w, pool_scale):
    pcol = 4 * RET_WIDTH // POOL_WIDTH
    return pl.pallas_call(
        _pool_kernel,
        out_shape=(jax.ShapeDtypeStruct((N_LAT, POOL_WIDTH), BF16),
                   jax.ShapeDtypeStruct((N_CTX, POOL_WIDTH), BF16)),
        grid=(BATCH,),
        in_specs=[pl.BlockSpec((SEQ, POOL_WIDTH), lambda b: (b, pcol)),
                  pl.BlockSpec((CTX_LEN, POOL_WIDTH), lambda b: (N_LAT // CTX_LEN + b, pcol)),
                  pl.BlockSpec((len(POOL_WINDOWS), POOL_CH, POOL_CH), lambda b: (0, 0, 0)),
                  pl.BlockSpec((1, POOL_WIDTH), lambda b: (0, 0))],
        out_specs=(pl.BlockSpec((SEQ, POOL_WIDTH), lambda b: (b, 0)),
                   pl.BlockSpec((CTX_LEN, POOL_WIDTH), lambda b: (b, 0))),
        compiler_params=pltpu.CompilerParams(vmem_limit_bytes=VMEM_LIMIT),
        name="pool",
    )(qkvgp, qkvgp, pool_w, pool_scale)


def _mix_out(i, rl, rc, pl_, pc, w_ref, xl_ref, xc_ref, gt_ref):
    is_lat = i < LAT_TILES
    ret = jnp.where(is_lat, rl[...], rc[...])
    pool = jnp.where(is_lat, pl_[...], pc[...])
    y = (jnp.dot(ret, w_ref[0:RET_WIDTH, :], preferred_element_type=F32)
         + jnp.dot(pool, w_ref[RET_WIDTH:, :], preferred_element_type=F32))
    return _x_tile(xl_ref, xc_ref) + gt_ref[...] * y


def _outproj_router_kernel(rl, rc, pl_, pc, w_ref, xl_ref, xc_ref, gt_ref, g_ref, sh_ref, sc_ref, wr_ref,
                           xo_ref, h_ref, route_ref, cnt_ref, carry_scr):
    i = pl.program_id(0)
    x = _mix_out(i, rl, rc, pl_, pc, w_ref, xl_ref, xc_ref, gt_ref)
    xo_ref[...] = x
    h = _rms_mod(x, g_ref[...], sh_ref[...], sc_ref[...])
    h_ref[...] = h.reshape(TM, *ROW_TILE)
    logits = jnp.dot(h.astype(BF16), wr_ref[...], preferred_element_type=F32)
    lane = lax.broadcasted_iota(jnp.int32, logits.shape, 1)
    neg = jnp.float32(-jnp.inf)
    logits = jnp.where(lane < N_EXPERTS, logits, neg)
    m1 = jnp.max(logits, axis=-1, keepdims=True)
    i1 = jnp.min(jnp.where(logits == m1, lane, 128), axis=-1, keepdims=True)
    rest = jnp.where(lane == i1, neg, logits)
    m2 = jnp.max(rest, axis=-1, keepdims=True)
    i2 = jnp.min(jnp.where(rest == m2, lane, 128), axis=-1, keepdims=True)
    e2 = jnp.exp(m2 - m1)
    den = 1.0 + e2

    @pl.when(i == 0)
    def _():
        carry_scr[...] = jnp.zeros_like(carry_scr)

    sel1, sel2 = lane == i1, lane == i2
    picks = jnp.where(sel1 | sel2, 1.0, 0.0)
    ti = lax.broadcasted_iota(jnp.int32, (TM, TM), 0)
    tj = lax.broadcasted_iota(jnp.int32, (TM, TM), 1)
    earlier = jnp.where(tj < ti, 1.0, 0.0).astype(BF16)
    before = jnp.dot(earlier, picks.astype(BF16), preferred_element_type=F32) + carry_scr[...]
    r1 = jnp.sum(jnp.where(sel1, before, 0.0), axis=-1, keepdims=True)
    r2 = jnp.sum(jnp.where(sel2, before, 0.0), axis=-1, keepdims=True)
    carry = carry_scr[...] + jnp.sum(picks, axis=0, keepdims=True)
    carry_scr[...] = carry
    cnt_ref[...] = jnp.broadcast_to(carry, cnt_ref.shape)

    cols = (i1.astype(F32), i2.astype(F32), r1, r2, 1.0 / den, e2 / den)
    route = jnp.zeros(logits.shape, F32)
    for k, v in enumerate(cols):
        route = jnp.where(lane == k, v, route)
    route_ref[...] = route


def _mixer_specs(ctx_off, layer):
    def lat_spec():
        return pl.BlockSpec((TM, RET_WIDTH), lambda i: (jnp.minimum(i, LAT_TILES - 1), 0))

    def ctx_spec():
        return pl.BlockSpec((TM, RET_WIDTH), lambda i: (jnp.maximum(i - LAT_TILES, 0), 0))

    return [lat_spec(), ctx_spec(), lat_spec(), ctx_spec(),
            pl.BlockSpec((D_MODEL, D_MODEL), lambda i: (0, 0), pipeline_mode=pl.Buffered(1)),
            *_x_specs(ctx_off), _mod_spec(layer, 2),
            pl.BlockSpec((1, D_MODEL), lambda i: (0, 0)),
            _mod_spec(layer, 3), _mod_spec(layer, 4)]


def _outproj_router(ret_l, ret_c, pool_l, pool_c, w_out, x_lat, x_ctx, ctx_off, mod5, norm_g, router_w, layer,
                    n_tiles):
    row = pl.BlockSpec((TM, D_MODEL), lambda i: (i, 0))
    n_rows = n_tiles * TM
    return pl.pallas_call(
        _outproj_router_kernel,
        out_shape=(jax.ShapeDtypeStruct((n_rows, D_MODEL), F32), jax.ShapeDtypeStruct((n_rows, *ROW_TILE), F32),
                   jax.ShapeDtypeStruct((n_rows, 128), F32), jax.ShapeDtypeStruct((8, 128), F32)),
        grid=(n_tiles,),
        in_specs=_mixer_specs(ctx_off, layer) + [pl.BlockSpec((D_MODEL, 128), lambda i: (0, 0))],
        out_specs=(row, pl.BlockSpec((TM, *ROW_TILE), lambda i: (i, 0, 0)),
                   pl.BlockSpec((TM, 128), lambda i: (i, 0)), pl.BlockSpec((8, 128), lambda i: (0, 0))),
        scratch_shapes=[pltpu.VMEM((1, 128), F32)],
        compiler_params=pltpu.CompilerParams(dimension_semantics=("arbitrary",), vmem_limit_bytes=VMEM_LIMIT),
        name="outproj",
    )(ret_l, ret_c, pool_l, pool_c, w_out, x_lat, x_ctx, mod5, norm_g, mod5, mod5, router_w)


def _swiglu(h, w13_ref, w2_ref, act_scr):
    for j in range(D_FF // FF_CHUNK):
        u = jnp.dot(h, w13_ref[0, :, j * FF_CHUNK:(j + 1) * FF_CHUNK], preferred_element_type=F32)
        g = jnp.dot(h, w13_ref[0, :, D_FF + j * FF_CHUNK:D_FF + (j + 1) * FF_CHUNK], preferred_element_type=F32)
        act_scr[:, j * FF_CHUNK:(j + 1) * FF_CHUNK] = (_silu(g) * u).astype(BF16)
    return jnp.dot(act_scr[...], w2_ref[0], preferred_element_type=F32)


def _outproj_ffn_kernel(rl, rc, pl_, pc, w_ref, xl_ref, xc_ref, gt1_ref, g_ref, sh_ref, sc_ref, gt2_ref,
                        w13_ref, w2_ref, o_ref, act_scr):
    x = _mix_out(pl.program_id(0), rl, rc, pl_, pc, w_ref, xl_ref, xc_ref, gt1_ref)
    h = _rms_mod(x, g_ref[...], sh_ref[...], sc_ref[...]).astype(BF16)
    o_ref[...] = x + gt2_ref[...] * _swiglu(h, w13_ref, w2_ref, act_scr)


def _outproj_ffn(ret_l, ret_c, pool_l, pool_c, w_out, x_lat, x_ctx, ctx_off, mod5, norm_g, w13, w2, layer, n_tiles):
    resident = pl.Buffered(1)
    return pl.pallas_call(
        _outproj_ffn_kernel,
        out_shape=jax.ShapeDtypeStruct((n_tiles * TM, D_MODEL), F32),
        grid=(n_tiles,),
        in_specs=_mixer_specs(ctx_off, layer) + [
            _mod_spec(layer, 5),
            pl.BlockSpec((1, D_MODEL, 2 * D_FF), lambda i: (0, 0, 0), pipeline_mode=resident),
            pl.BlockSpec((1, D_FF, D_MODEL), lambda i: (0, 0, 0), pipeline_mode=resident)],
        out_specs=pl.BlockSpec((TM, D_MODEL), lambda i: (i, 0)),
        scratch_shapes=[pltpu.VMEM((TM, D_FF), BF16)],
        compiler_params=pltpu.CompilerParams(vmem_limit_bytes=VMEM_LIMIT),
        name="outproj_ffn",
    )(ret_l, ret_c, pool_l, pool_c, w_out, x_lat, x_ctx, mod5, norm_g, mod5, mod5, mod5, w13, w2)


def _moe_plan(counts, n_steps):
    cnt = counts.astype(jnp.int32)
    end = jnp.cumsum(cnt)
    start = end - cnt
    first = start // TM
    visits = jnp.where(cnt > 0, (end - 1) // TM - first + 1, 0)
    visit_end = jnp.cumsum(visits)
    visit_start = visit_end - visits
    total = visit_end[-1]
    step = jnp.minimum(jnp.arange(n_steps, dtype=jnp.int32), total - 1)
    eid = jnp.minimum(jnp.sum(step[:, None] >= visit_end[None, :], axis=1), N_EXPERTS - 1).astype(jnp.int32)
    tile = first[eid] + step - visit_start[eid]
    lo = jnp.clip(start[eid] - tile * TM, 0, TM)
    hi = jnp.clip(end[eid] - tile * TM, 0, TM)
    hi = jnp.where(jnp.arange(n_steps) < total, hi, lo)
    return start, tile.astype(jnp.int32), eid, lo.astype(jnp.int32), hi.astype(jnp.int32)


def _row_copy_wait(src, dst, sem):
    pltpu.make_async_copy(src, dst, sem).wait()


def _dispatch_kernel(pos_ref, h_ref, xs_hbm, sem):
    i = pl.program_id(0)
    n = 0
    for r in range(TM):
        for k in range(2):
            p = pos_ref[i, k * TM + r]
            pltpu.make_async_copy(h_ref.at[r], xs_hbm.at[p], sem).start(priority=n % 2)
            n += 1
    for k in range(2):
        _row_copy_wait(h_ref, xs_hbm.at[pl.ds(0, TM)], sem)


def _dispatch(pos, h, n_tiles):
    return pl.pallas_call(
        _dispatch_kernel,
        out_shape=jax.ShapeDtypeStruct((2 * n_tiles * TM, *ROW_TILE), F32),
        grid_spec=pltpu.PrefetchScalarGridSpec(
            num_scalar_prefetch=1,
            grid=(n_tiles,),
            in_specs=[pl.BlockSpec((TM, *ROW_TILE), lambda i, pos: (i, 0, 0))],
            out_specs=pl.BlockSpec(memory_space=pl.ANY),
            scratch_shapes=[pltpu.SemaphoreType.DMA]),
        name="dispatch",
    )(pos, h)


def _expert_ffn_kernel(tile_ref, eid_ref, lo_ref, hi_ref, x_ref, w13_ref, w2_ref, o_ref, act_scr):
    s = pl.program_id(0)
    lo, hi = lo_ref[s], hi_ref[s]

    @pl.when(hi > lo)
    def _():
        row = lax.broadcasted_iota(jnp.int32, (TM, D_MODEL), 0)
        x = x_ref[...].reshape(TM, D_MODEL)
        h = jnp.where((row >= lo) & (row < hi), x, 0.0).astype(BF16)
        y = _swiglu(h, w13_ref, w2_ref, act_scr).reshape(TM, *ROW_TILE)

        @pl.when(lo == 0)
        def _():
            o_ref[...] = y

        @pl.when(lo > 0)
        def _():
            o_ref[...] += y


def _expert_ffn(plan, xs, w13, w2, moe_layer, n_steps):
    _, tile, eid, lo, hi = plan
    row = pl.BlockSpec((TM, *ROW_TILE), lambda s, tile, eid, lo, hi: (tile[s], 0, 0))
    return pl.pallas_call(
        _expert_ffn_kernel,
        out_shape=jax.ShapeDtypeStruct(xs.shape, F32),
        grid_spec=pltpu.PrefetchScalarGridSpec(
            num_scalar_prefetch=4,
            grid=(n_steps,),
            in_specs=[row,
                      pl.BlockSpec((None, 1, D_MODEL, 2 * D_FF),
                                   lambda s, tile, eid, lo, hi: (moe_layer, eid[s], 0, 0)),
                      pl.BlockSpec((None, 1, D_FF, D_MODEL),
                                   lambda s, tile, eid, lo, hi: (moe_layer, eid[s], 0, 0))],
            out_specs=row,
            scratch_shapes=[pltpu.VMEM((TM, D_FF), BF16)]),
        compiler_params=pltpu.CompilerParams(vmem_limit_bytes=VMEM_LIMIT),
        name="expert_ffn",
    )(tile, eid, lo, hi, xs, w13, w2)


def _combine_kernel(pos_ref, y_hbm, route_ref, x_ref, gt_ref, g_ref, o_ref, ybuf, sem, *, n_tiles, final_norm):
    i = pl.program_id(0)
    slot = i % 2

    def request(tile, into):
        n = 0
        for r in range(TM):
            for k in range(2):
                p = pos_ref[tile, k * TM + r]
                pltpu.make_async_copy(y_hbm.at[p], ybuf.at[into, k, r], sem.at[into]).start(priority=n % 2)
                n += 1

    @pl.when(i == 0)
    def _():
        request(0, 0)

    @pl.when(i + 1 < n_tiles)
    def _():
        request(i + 1, 1 - slot)

    for k in range(2):
        _row_copy_wait(y_hbm.at[pl.ds(0, TM)], ybuf.at[slot, k], sem.at[slot])
    w1 = route_ref[:, 4:5]
    w2 = route_ref[:, 5:6]
    y1 = ybuf[slot, 0].reshape(TM, D_MODEL)
    y2 = ybuf[slot, 1].reshape(TM, D_MODEL)
    x = x_ref[...] + gt_ref[...] * (w1 * y1 + w2 * y2)
    if final_norm:
        ms = jnp.mean(x * x, axis=-1, keepdims=True)
        x = x * lax.rsqrt(ms + NORM_EPS) * g_ref[...]
    o_ref[...] = x


def _combine(pos, y, route, x_all, mod5, final_g, layer, n_tiles, final_norm):
    row = pl.BlockSpec((TM, D_MODEL), lambda i, pos: (i, 0))
    return pl.pallas_call(
        functools.partial(_combine_kernel, n_tiles=n_tiles, final_norm=final_norm),
        out_shape=jax.ShapeDtypeStruct((n_tiles * TM, D_MODEL), F32),
        grid_spec=pltpu.PrefetchScalarGridSpec(
            num_scalar_prefetch=1,
            grid=(n_tiles,),
            in_specs=[pl.BlockSpec(memory_space=pl.ANY),
                      pl.BlockSpec((TM, 128), lambda i, pos: (i, 0)), row, _mod_spec(layer, 5),
                      pl.BlockSpec((1, D_MODEL), lambda i, pos: (0, 0))],
            out_specs=row,
            scratch_shapes=[pltpu.VMEM((2, 2, TM, *ROW_TILE), F32), pltpu.SemaphoreType.DMA((2,))]),
        compiler_params=pltpu.CompilerParams(dimension_semantics=("arbitrary",), vmem_limit_bytes=VMEM_LIMIT),
        name="combine",
    )(pos, y, route, x_all, mod5, final_g)


def _moe(h, route, counts, x_all, mod5, w13, w2, final_g, layer, n_tiles, final_norm):
    n_steps = 2 * n_tiles + N_EXPERTS
    plan = _moe_plan(counts[0, :N_EXPERTS], n_steps)
    meta = route[:, :4].astype(jnp.int32)
    pos = plan[0][meta[:, 0:2]] + meta[:, 2:4]
    pos = pos.reshape(n_tiles, TM, 2).transpose(0, 2, 1).reshape(n_tiles, 2 * TM)
    xs = _dispatch(pos, h, n_tiles)
    ys = _expert_ffn(plan, xs, w13, w2, layer // 2, n_steps)
    return _combine(pos, ys, route, x_all, mod5, final_g, layer, n_tiles, final_norm)


def kernel(x, c, ctx, c_ctx, w_ada, b_ada, norm1_g, norm2_g, w_in, ret_decay_logit, ret_gn_g, pool_w, pool_scale,
           w_out, ffn_w13, ffn_w2, router_w, moe_w13, moe_w2, final_norm_g):
    assert DEPTH % 2 == 0, "the final norm is fused into the last routed-expert combine"
    c_all = jnp.concatenate([c, c_ctx[None, :], jnp.zeros((MOD_ROWS - BATCH - 1, D_MODEL), F32)], axis=0)
    mod5 = _ada_mod(c_all, w_ada, b_ada).reshape(DEPTH, MOD_ROWS, 6, 1, D_MODEL)
    tables = _rope_tables()
    log_gamma = jax.nn.log_sigmoid(ret_decay_logit.astype(F32))
    router_pad = jnp.pad(router_w, ((0, 0), (0, 0), (0, 128 - N_EXPERTS))).astype(BF16)
    moe_w13_b, moe_w2_b = moe_w13.astype(BF16), moe_w2.astype(BF16)
    final_g = final_norm_g[None, :]

    x_lat, x_ctx, ctx_off = x.reshape(N_LAT, D_MODEL), ctx.reshape(N_CTX, D_MODEL), 0
    for l in range(DEPTH):
        last = l == DEPTH - 1
        n_tiles = LAT_TILES if last else LAT_TILES + CTX_TILES
        qkvgp = _inproj(x_lat, x_ctx, ctx_off, norm1_g[l][None, :], mod5, w_in[l].astype(BF16), tables, l,
                        LAT_TILES + CTX_TILES)
        ret_l, ret_c = _retention(qkvgp, log_gamma[l], ret_gn_g[l][None, :])
        pool_l, pool_c = _pool(qkvgp, pool_w[l].astype(BF16), pool_scale[l][None, :])
        i = l // 2
        if l % 2 == 0:
            x_all = _outproj_ffn(ret_l, ret_c, pool_l, pool_c, w_out[l].astype(BF16), x_lat, x_ctx, ctx_off, mod5,
                                 norm2_g[l][None, :], ffn_w13[i][None].astype(BF16), ffn_w2[i][None].astype(BF16),
                                 l, n_tiles)
        else:
            x_all, h2, route, counts = _outproj_router(ret_l, ret_c, pool_l, pool_c, w_out[l].astype(BF16), x_lat,
                                                       x_ctx, ctx_off, mod5, norm2_g[l][None, :], router_pad[i],
                                                       l, n_tiles)
            x_all = _moe(h2, route, counts, x_all, mod5, moe_w13_b, moe_w2_b, final_g, l, n_tiles, last)
        x_lat, x_ctx, ctx_off = x_all, x_all, LAT_TILES
    return x_all.reshape(BATCH, SEQ, D_MODEL)
```

```python
import functools

import jax
import jax.numpy as jnp
from jax import lax
from jax.experimental import pallas as pl
from jax.experimental.pallas import tpu as pltpu

F32 = jnp.float32
BF16 = jnp.bfloat16

D_MODEL = 1024
BATCH = 8
SEQ = 4096
DEPTH = 4
GRID_W = 64
GRID_SHIFT = 6
CTX_LEN = 256
RET_WIDTH = 512
POOL_WIDTH = 512
RET_HEADS = 4
HEAD_DIM = 128
CHUNK = 128
ROPE_BASE = 10000.0
POOL_WINDOWS = (2, 4, 8, 16)
POOL_CH = 128
IN_COLS = 4 * RET_WIDTH + POOL_WIDTH
D_FF = 2816
N_EXPERTS = 8
NORM_EPS = 1e-6
GN_EPS = 1e-5

N_LAT = BATCH * SEQ
N_CTX = BATCH * CTX_LEN
N_ROWS = N_LAT + N_CTX
MOD_ROWS = 16
CTX_MOD_ROW = BATCH

TM = 512
LAT_TILES = N_LAT // TM
CTX_TILES = N_CTX // TM
TILES_PER_BATCH = SEQ // TM
FF_CHUNK = 256
LANES = 128
ROW_TILE = (D_MODEL // LANES, LANES)
VMEM_LIMIT = 56 * 1024 * 1024


def _mod_row(i):
    return jnp.where(i < LAT_TILES, i // TILES_PER_BATCH, CTX_MOD_ROW)


def _mod_spec(layer, part):
    return pl.BlockSpec((None, None, None, 1, D_MODEL),
                        lambda i, *_: (layer, _mod_row(i), part, 0, 0))


def _silu(v):
    return v * jax.nn.sigmoid(v)


def _x_specs(ctx_block_offset):
    return [pl.BlockSpec((TM, D_MODEL), lambda i, *_: (jnp.minimum(i, LAT_TILES - 1), 0)),
            pl.BlockSpec((TM, D_MODEL), lambda i, *_: (ctx_block_offset + jnp.maximum(i - LAT_TILES, 0), 0))]


def _x_tile(xl_ref, xc_ref):
    return jnp.where(pl.program_id(0) < LAT_TILES, xl_ref[...], xc_ref[...])


def _cast_specs(rows, cols, layer_index):
    blk = rows // LAT_TILES
    step = lambda i: jnp.minimum(i, LAT_TILES - 1)
    return (pl.BlockSpec((blk, cols), lambda i: (layer_index * LAT_TILES + step(i), 0)),
            pl.BlockSpec((blk, cols), lambda i: (step(i), 0)))


def _rms_mod(x, gain, shift, scale):
    ms = jnp.mean(x * x, axis=-1, keepdims=True)
    y = x * lax.rsqrt(ms + NORM_EPS) * gain
    return y * (1.0 + scale) + shift


def _ada_kernel(c_ref, w_ref, b_ref, o_ref):
    s = _silu(c_ref[...])
    o_ref[0] = jnp.dot(s.astype(BF16), w_ref[0].astype(BF16), preferred_element_type=F32) + b_ref[0]


def _ada_mod(c_all, w_ada, b_ada):
    tn = 1536
    return pl.pallas_call(
        _ada_kernel,
        out_shape=jax.ShapeDtypeStruct((DEPTH, MOD_ROWS, 6 * D_MODEL), F32),
        grid=(DEPTH, 6 * D_MODEL // tn),
        in_specs=[pl.BlockSpec((MOD_ROWS, D_MODEL), lambda l, n: (0, 0)),
                  pl.BlockSpec((1, D_MODEL, tn), lambda l, n: (l, 0, n)),
                  pl.BlockSpec((1, 1, tn), lambda l, n: (l, 0, n))],
        out_specs=pl.BlockSpec((1, MOD_ROWS, tn), lambda l, n: (l, 0, n)),
        compiler_params=pltpu.CompilerParams(vmem_limit_bytes=VMEM_LIMIT),
        name="ada_mod",
    )(c_all, w_ada, b_ada.reshape(DEPTH, 1, 6 * D_MODEL))


def _inproj_kernel(xl_ref, xc_ref, g_ref, sh_ref, sc_ref, w_ref, cos_ref, sa_ref, sb_ref, o_ref):
    h = _rms_mod(_x_tile(xl_ref, xc_ref), g_ref[...], sh_ref[...], sc_ref[...]).astype(BF16)
    _project_in(h, w_ref, cos_ref, sa_ref, sb_ref, o_ref)


def _inproj_cast_kernel(xl_ref, xc_ref, g_ref, sh_ref, sc_ref, w_ref, cos_ref, sa_ref, sb_ref, wf_ref, o_ref, wb_ref):
    _inproj_kernel(xl_ref, xc_ref, g_ref, sh_ref, sc_ref, w_ref, cos_ref, sa_ref, sb_ref, o_ref)
    wb_ref[...] = wf_ref[...].astype(BF16)


def _project_in(h, w_ref, cos_ref, sa_ref, sb_ref, o_ref):
    cos, sa, sb = cos_ref[...], sa_ref[...], sb_ref[...]
    k_scale = HEAD_DIM ** -0.5
    for part, mul in ((0, 1.0), (1, k_scale)):
        z = jnp.dot(h, w_ref[:, part * RET_WIDTH:(part + 1) * RET_WIDTH], preferred_element_type=F32)
        for hh in range(RET_HEADS):
            t = z[:, hh * HEAD_DIM:(hh + 1) * HEAD_DIM]
            r = t * cos + pltpu.roll(t, 96, axis=1) * sa + pltpu.roll(t, 32, axis=1) * sb
            if mul != 1.0:
                r = r * mul
            col = part * RET_WIDTH + hh * HEAD_DIM
            o_ref[:, col:col + HEAD_DIM] = r.astype(BF16)
    z = jnp.dot(h, w_ref[:, 2 * RET_WIDTH:], preferred_element_type=F32)
    o_ref[:, 2 * RET_WIDTH:] = z.astype(BF16)


def _rope_tables():
    half = HEAD_DIM // 2
    inv_freq = ROPE_BASE ** (-jnp.arange(0, half, 2, dtype=F32) / half)
    t = jnp.arange(SEQ)
    rows, cols = (t // GRID_W).astype(F32), (t % GRID_W).astype(F32)
    ang_r = rows[:, None] * inv_freq[None, :]
    ang_c = cols[:, None] * inv_freq[None, :]
    zero = jnp.zeros_like(ang_r)
    cos = jnp.concatenate([jnp.cos(ang_r), jnp.cos(ang_r), jnp.cos(ang_c), jnp.cos(ang_c)], axis=1)
    sa = jnp.concatenate([-jnp.sin(ang_r), zero, -jnp.sin(ang_c), zero], axis=1)
    sb = jnp.concatenate([zero, jnp.sin(ang_r), zero, jnp.sin(ang_c)], axis=1)
    ident = jnp.ones((TM, HEAD_DIM), F32)
    none = jnp.zeros((TM, HEAD_DIM), F32)
    return (jnp.concatenate([cos, ident]), jnp.concatenate([sa, none]), jnp.concatenate([sb, none]))


def _inproj(x_lat, x_ctx, ctx_off, norm_g, mod5, w_in, tables, layer, n_tiles, cast=None):
    tab_spec = pl.BlockSpec((TM, HEAD_DIM),
                            lambda i: (jnp.where(i < LAT_TILES, i % TILES_PER_BATCH, TILES_PER_BATCH), 0))
    body = _inproj_kernel
    in_specs = _x_specs(ctx_off) + [
        pl.BlockSpec((1, D_MODEL), lambda i: (0, 0)),
        _mod_spec(layer, 0), _mod_spec(layer, 1),
        pl.BlockSpec((D_MODEL, IN_COLS), lambda i: (0, 0)),
        tab_spec, tab_spec, tab_spec]
    args = [x_lat, x_ctx, norm_g, mod5, mod5, w_in, *tables]
    out_shape = [jax.ShapeDtypeStruct((N_ROWS, IN_COLS), BF16)]
    out_specs = [pl.BlockSpec((TM, IN_COLS), lambda i: (i, 0))]
    if cast is not None:
        stacked, index = cast
        rows, cols = stacked.shape[0] // (DEPTH // 2), stacked.shape[1]
        src_spec, dst_spec = _cast_specs(rows, cols, index)
        body = _inproj_cast_kernel
        in_specs.append(src_spec)
        args.append(stacked)
        out_shape.append(jax.ShapeDtypeStruct((rows, cols), BF16))
        out_specs.append(dst_spec)
    out = pl.pallas_call(
        body, out_shape=tuple(out_shape), grid=(n_tiles,), in_specs=in_specs, out_specs=tuple(out_specs),
        compiler_params=pltpu.CompilerParams(vmem_limit_bytes=VMEM_LIMIT),
        name="inproj",
    )(*args)
    return out if cast is not None else out[0]


N_CHUNK_LAT = SEQ // CHUNK
N_CHUNK_CTX = CTX_LEN // CHUNK
N_CHUNK = N_CHUNK_LAT + N_CHUNK_CTX
RET_UNROLL = 32


def _ret_kernel(lg_ref, ql, kl, vl, gl, qc, kc, vc, gc, gain_ref, ol, oc, kv_scr, st_scr):
    hd = pl.program_id(1)
    lgf = lg_ref[0, hd]
    lgb = lg_ref[1, hd]
    pi = lax.broadcasted_iota(jnp.int32, (CHUNK, CHUNK), 0).astype(F32)
    pj = lax.broadcasted_iota(jnp.int32, (CHUNK, CHUNK), 1).astype(F32)
    tail_f = jnp.exp(lgf * (CHUNK - 1.0 - pi))
    tail_b = jnp.exp(lgb * pi)
    head_f = jnp.exp(lgf * (pi + 1.0))
    head_b = jnp.exp(lgb * (CHUNK - pi))
    dif = pi - pj
    decay = jnp.where(dif >= 0, jnp.exp(lgf * jnp.maximum(dif, 0.0)), jnp.exp(lgb * jnp.maximum(-dif, 0.0)))
    zeros = jnp.zeros((CHUNK, CHUNK), F32)
    gchunk_f = jnp.exp(zeros + lgf * CHUNK)
    gchunk_b = jnp.exp(zeros + lgb * CHUNK)
    gain = gain_ref[...]

    def kv_of(k, v):
        kf = k.astype(F32)
        kk = jnp.concatenate([(kf * tail_f).astype(BF16), (kf * tail_b).astype(BF16)], axis=1)
        return lax.dot_general(kk, v, (((0,), (0,)), ((), ())), preferred_element_type=F32)

    for c in range(N_CHUNK_CTX):
        kv_scr[c] = kv_of(kc[c * CHUNK:(c + 1) * CHUNK, :], vc[c * CHUNK:(c + 1) * CHUNK, :])

    def kv_body(c, carry):
        r = pl.multiple_of(c * CHUNK, CHUNK)
        kv_scr[N_CHUNK_CTX + c] = kv_of(kl[pl.ds(r, CHUNK), :], vl[pl.ds(r, CHUNK), :])
        return carry

    lax.fori_loop(0, N_CHUNK_LAT, kv_body, 0, unroll=RET_UNROLL)

    def fwd_body(c, s):
        st_scr[c, 0:HEAD_DIM, :] = s.astype(BF16)
        return gchunk_f * s + kv_scr[c, 0:HEAD_DIM, :]

    lax.fori_loop(0, N_CHUNK, fwd_body, zeros)

    def bwd_step(c, s):
        st_scr[c, HEAD_DIM:2 * HEAD_DIM, :] = s.astype(BF16)
        return gchunk_b * s + kv_scr[c, HEAD_DIM:2 * HEAD_DIM, :]

    s = zeros
    for c in reversed(range(N_CHUNK_CTX)):
        s = bwd_step(c, s)
    lax.fori_loop(0, N_CHUNK_LAT, lambda t, s: bwd_step(N_CHUNK - 1 - t, s), s)

    def out_of(q, k, v, g, st):
        sc = lax.dot_general(q, k, (((1,), (1,)), ((), ())), preferred_element_type=F32)
        intra = jnp.dot((sc * decay).astype(BF16), v, preferred_element_type=F32)
        qf = q.astype(F32)
        qq = jnp.concatenate([(qf * head_f).astype(BF16), (qf * head_b).astype(BF16)], axis=1)
        o = intra + jnp.dot(qq, st, preferred_element_type=F32)
        mu = jnp.mean(o, axis=-1, keepdims=True)
        var = jnp.mean(jnp.square(o - mu), axis=-1, keepdims=True)
        on = (o - mu) * lax.rsqrt(var + GN_EPS)
        return (on * gain * _silu(g.astype(F32))).astype(BF16)

    for c in range(N_CHUNK_CTX):
        sl = slice(c * CHUNK, (c + 1) * CHUNK)
        oc[sl, :] = out_of(qc[sl, :], kc[sl, :], vc[sl, :], gc[sl, :], st_scr[c])

    def out_body(c, carry):
        r = pl.multiple_of(c * CHUNK, CHUNK)
        sl = pl.ds(r, CHUNK)
        ol[sl, :] = out_of(ql[sl, :], kl[sl, :], vl[sl, :], gl[sl, :], st_scr[N_CHUNK_CTX + c])
        return carry

    lax.fori_loop(0, N_CHUNK_LAT, out_body, 0, unroll=RET_UNROLL)


def _retention(qkvgp, log_gamma, gn_g):
    def lat(part):
        return pl.BlockSpec((SEQ, HEAD_DIM), lambda b, h: (b, part * RET_HEADS + h))

    def ctx(part):
        return pl.BlockSpec((CTX_LEN, HEAD_DIM), lambda b, h: (N_LAT // CTX_LEN + b, part * RET_HEADS + h))

    return pl.pallas_call(
        _ret_kernel,
        out_shape=(jax.ShapeDtypeStruct((N_LAT, RET_WIDTH), BF16),
                   jax.ShapeDtypeStruct((N_CTX, RET_WIDTH), BF16)),
        grid=(BATCH, RET_HEADS),
        in_specs=[pl.BlockSpec(memory_space=pltpu.SMEM),
                  lat(0), lat(1), lat(2), lat(3), ctx(0), ctx(1), ctx(2), ctx(3),
                  pl.BlockSpec((1, HEAD_DIM), lambda b, h: (0, h))],
        out_specs=(pl.BlockSpec((SEQ, HEAD_DIM), lambda b, h: (b, h)),
                   pl.BlockSpec((CTX_LEN, HEAD_DIM), lambda b, h: (b, h))),
        scratch_shapes=[pltpu.VMEM((N_CHUNK, 2 * HEAD_DIM, HEAD_DIM), F32),
                        pltpu.VMEM((N_CHUNK, 2 * HEAD_DIM, HEAD_DIM), BF16)],
        compiler_params=pltpu.CompilerParams(vmem_limit_bytes=VMEM_LIMIT),
        name="retention",
    )(log_gamma, *([qkvgp] * 8), gn_g)


POOL_BLK = 256


def _window_count(idx, w, n):
    return (jnp.minimum(idx - w // 2 + w, n) - jnp.maximum(idx - w // 2, 0)).astype(F32)


def _row_window_mean(s, w):
    size = [min(r - w // 2 + w, GRID_W) - max(r - w // 2, 0) for r in range(GRID_W)]
    parts, r0 = [], 0
    for r in range(1, GRID_W + 1):
        if r == GRID_W or size[r] != size[r0]:
            parts.append(s[r0 * GRID_W:r * GRID_W] * (1.0 / size[r0]))
            r0 = r
    return jnp.concatenate(parts, axis=0)


def _pool_kernel(pl_ref, pc_ref, w_ref, scale_ref, ol, oc):
    ti = lax.broadcasted_iota(jnp.int32, (POOL_BLK, POOL_BLK), 0)
    tj = lax.broadcasted_iota(jnp.int32, (POOL_BLK, POOL_BLK), 1)
    tok_b = lax.broadcasted_iota(jnp.int32, (POOL_BLK, POOL_CH), 0)
    tok_c = lax.broadcasted_iota(jnp.int32, (CTX_LEN, POOL_CH), 0)
    for gi, w in enumerate(POOL_WINDOWS):
        cols = slice(gi * POOL_CH, (gi + 1) * POOL_CH)
        lo_off, hi_off = -(w // 2), w - 1 - w // 2
        wmat = w_ref[gi]
        scale = scale_ref[:, cols]

        ci, cj = ti & (GRID_W - 1), tj & (GRID_W - 1)
        band = ((ti >> GRID_SHIFT == tj >> GRID_SHIFT) & (cj >= ci + lo_off) & (cj <= ci + hi_off))
        band = jnp.where(band, 1.0, 0.0).astype(BF16)
        col_inv = 1.0 / _window_count(tok_b & (GRID_W - 1), w, GRID_W)
        col_mean = [jnp.dot(band, pl_ref[blk * POOL_BLK:(blk + 1) * POOL_BLK, cols],
                            preferred_element_type=F32) * col_inv for blk in range(SEQ // POOL_BLK)]
        pad = jnp.zeros((w // 2 * GRID_W, POOL_CH), F32)
        s = jnp.concatenate([pad] + col_mean + [pad], axis=0)
        span = 1
        while span < w:
            n = s.shape[0] - span * GRID_W
            s = s[:n] + s[span * GRID_W:]
            span *= 2
        box_mean = _row_window_mean(s[:SEQ], w)
        for blk in range(SEQ // POOL_BLK):
            rs = slice(blk * POOL_BLK, (blk + 1) * POOL_BLK)
            d = (box_mean[rs] - pl_ref[rs, cols].astype(F32)).astype(BF16)
            y = jnp.dot(d, wmat, preferred_element_type=F32) * scale
            ol[rs, cols] = y.astype(BF16)

        band_c = jnp.where((tj >= ti + lo_off) & (tj <= ti + hi_off), 1.0, 0.0).astype(BF16)
        mc = jnp.dot(band_c, pc_ref[:, cols], preferred_element_type=F32) / _window_count(tok_c, w, CTX_LEN)
        uc = pc_ref[:, cols].astype(F32)
        dc = (mc - uc).astype(BF16)
        oc[:, cols] = (jnp.dot(dc, wmat, preferred_element_type=F32) * scale).astype(BF16)


def _pool(qkvgp, pool_w, pool_scale):
    pcol = 4 * RET_WIDTH // POOL_WIDTH
    return pl.pallas_call(
        _pool_kernel,
        out_shape=(jax.ShapeDtypeStruct((N_LAT, POOL_WIDTH), BF16),
                   jax.ShapeDtypeStruct((N_CTX, POOL_WIDTH), BF16)),
        grid=(BATCH,),
        in_specs=[pl.BlockSpec((SEQ, POOL_WIDTH), lambda b: (b, pcol)),
                  pl.BlockSpec((CTX_LEN, POOL_WIDTH), lambda b: (N_LAT // CTX_LEN + b, pcol)),
                  pl.BlockSpec((len(POOL_WINDOWS), POOL_CH, POOL_CH), lambda b: (0, 0, 0)),
                  pl.BlockSpec((1, POOL_WIDTH), lambda b: (0, 0))],
        out_specs=(pl.BlockSpec((SEQ, POOL_WIDTH), lambda b: (b, 0)),
                   pl.BlockSpec((CTX_LEN, POOL_WIDTH), lambda b: (b, 0))),
        compiler_params=pltpu.CompilerParams(vmem_limit_bytes=VMEM_LIMIT),
        name="pool",
    )(qkvgp, qkvgp, pool_w, pool_scale)


def _mix_out(i, rl, rc, pl_, pc, w_ref, xl_ref, xc_ref, gt_ref):
    is_lat = i < LAT_TILES
    ret = jnp.where(is_lat, rl[...], rc[...])
    pool = jnp.where(is_lat, pl_[...], pc[...])
    y = (jnp.dot(ret, w_ref[0:RET_WIDTH, :], preferred_element_type=F32)
         + jnp.dot(pool, w_ref[RET_WIDTH:, :], preferred_element_type=F32))
    return _x_tile(xl_ref, xc_ref) + gt_ref[...] * y


def _outproj_router_kernel(rl, rc, pl_, pc, w_ref, xl_ref, xc_ref, gt_ref, g_ref, sh_ref, sc_ref, wr_ref,
                           xo_ref, h_ref, route_ref, cnt_ref, carry_scr):
    i = pl.program_id(0)
    x = _mix_out(i, rl, rc, pl_, pc, w_ref, xl_ref, xc_ref, gt_ref)
    xo_ref[...] = x
    h = _rms_mod(x, g_ref[...], sh_ref[...], sc_ref[...])
    h_ref[...] = h.reshape(TM, *ROW_TILE)
    logits = jnp.dot(h.astype(BF16), wr_ref[...], preferred_element_type=F32)
    lane = lax.broadcasted_iota(jnp.int32, logits.shape, 1)
    neg = jnp.float32(-jnp.inf)
    logits = jnp.where(lane < N_EXPERTS, logits, neg)
    m1 = jnp.max(logits, axis=-1, keepdims=True)
    i1 = jnp.min(jnp.where(logits == m1, lane, 128), axis=-1, keepdims=True)
    rest = jnp.where(lane == i1, neg, logits)
    m2 = jnp.max(rest, axis=-1, keepdims=True)
    i2 = jnp.min(jnp.where(rest == m2, lane, 128), axis=-1, keepdims=True)
    e2 = jnp.exp(m2 - m1)
    den = 1.0 + e2

    @pl.when(i == 0)
    def _():
        carry_scr[...] = jnp.zeros_like(carry_scr)

    sel1, sel2 = lane == i1, lane == i2
    picks = jnp.where(sel1 | sel2, 1.0, 0.0)
    ti = lax.broadcasted_iota(jnp.int32, (TM, TM), 0)
    tj = lax.broadcasted_iota(jnp.int32, (TM, TM), 1)
    earlier = jnp.where(tj < ti, 1.0, 0.0).astype(BF16)
    before = jnp.dot(earlier, picks.astype(BF16), preferred_element_type=F32) + carry_scr[...]
    r1 = jnp.sum(jnp.where(sel1, before, 0.0), axis=-1, keepdims=True)
    r2 = jnp.sum(jnp.where(sel2, before, 0.0), axis=-1, keepdims=True)
    carry = carry_scr[...] + jnp.sum(picks, axis=0, keepdims=True)
    carry_scr[...] = carry
    cnt_ref[...] = jnp.broadcast_to(carry, cnt_ref.shape)

    cols = (i1.astype(F32), i2.astype(F32), r1, r2, 1.0 / den, e2 / den)
    route = jnp.zeros(logits.shape, F32)
    for k, v in enumerate(cols):
        route = jnp.where(lane == k, v, route)
    route_ref[...] = route


def _mixer_specs(ctx_off, layer):
    def lat_spec():
        return pl.BlockSpec((TM, RET_WIDTH), lambda i: (jnp.minimum(i, LAT_TILES - 1), 0))

    def ctx_spec():
        return pl.BlockSpec((TM, RET_WIDTH), lambda i: (jnp.maximum(i - LAT_TILES, 0), 0))

    return [lat_spec(), ctx_spec(), lat_spec(), ctx_spec(),
            pl.BlockSpec((D_MODEL, D_MODEL), lambda i: (0, 0), pipeline_mode=pl.Buffered(1)),
            *_x_specs(ctx_off), _mod_spec(layer, 2),
            pl.BlockSpec((1, D_MODEL), lambda i: (0, 0)),
            _mod_spec(layer, 3), _mod_spec(layer, 4)]


def _outproj_router(ret_l, ret_c, pool_l, pool_c, w_out, x_lat, x_ctx, ctx_off, mod5, norm_g, router_w, layer,
                    n_tiles):
    row = pl.BlockSpec((TM, D_MODEL), lambda i: (i, 0))
    n_rows = n_tiles * TM
    return pl.pallas_call(
        _outproj_router_kernel,
        out_shape=(jax.ShapeDtypeStruct((n_rows, D_MODEL), F32), jax.ShapeDtypeStruct((n_rows, *ROW_TILE), F32),
                   jax.ShapeDtypeStruct((n_rows, 128), F32), jax.ShapeDtypeStruct((8, 128), F32)),
        grid=(n_tiles,),
        in_specs=_mixer_specs(ctx_off, layer) + [pl.BlockSpec((D_MODEL, 128), lambda i: (0, 0))],
        out_specs=(row, pl.BlockSpec((TM, *ROW_TILE), lambda i: (i, 0, 0)),
                   pl.BlockSpec((TM, 128), lambda i: (i, 0)), pl.BlockSpec((8, 128), lambda i: (0, 0))),
        scratch_shapes=[pltpu.VMEM((1, 128), F32)],
        compiler_params=pltpu.CompilerParams(dimension_semantics=("arbitrary",), vmem_limit_bytes=VMEM_LIMIT),
        name="outproj",
    )(ret_l, ret_c, pool_l, pool_c, w_out, x_lat, x_ctx, mod5, norm_g, mod5, mod5, router_w)


def _swiglu(h, w13_ref, w2_ref, act_scr):
    for j in range(D_FF // FF_CHUNK):
        u = jnp.dot(h, w13_ref[0, :, j * FF_CHUNK:(j + 1) * FF_CHUNK], preferred_element_type=F32)
        g = jnp.dot(h, w13_ref[0, :, D_FF + j * FF_CHUNK:D_FF + (j + 1) * FF_CHUNK], preferred_element_type=F32)
        act_scr[:, j * FF_CHUNK:(j + 1) * FF_CHUNK] = (_silu(g) * u).astype(BF16)
    return jnp.dot(act_scr[...], w2_ref[0], preferred_element_type=F32)


def _outproj_ffn_kernel(rl, rc, pl_, pc, w_ref, xl_ref, xc_ref, gt1_ref, g_ref, sh_ref, sc_ref, gt2_ref,
                        w13_ref, w2_ref, wf_ref, o_ref, wb_ref, act_scr):
    x = _mix_out(pl.program_id(0), rl, rc, pl_, pc, w_ref, xl_ref, xc_ref, gt1_ref)
    h = _rms_mod(x, g_ref[...], sh_ref[...], sc_ref[...]).astype(BF16)
    o_ref[...] = x + gt2_ref[...] * _swiglu(h, w13_ref, w2_ref, act_scr)
    wb_ref[...] = wf_ref[...].astype(BF16)


def _outproj_ffn(ret_l, ret_c, pool_l, pool_c, w_out, x_lat, x_ctx, ctx_off, mod5, norm_g, w13, w2, cast, layer,
                 n_tiles):
    resident = pl.Buffered(1)
    stacked, index = cast
    rows, cols = stacked.shape[0] // (DEPTH // 2), stacked.shape[1]
    src_spec, dst_spec = _cast_specs(rows, cols, index)
    return pl.pallas_call(
        _outproj_ffn_kernel,
        out_shape=(jax.ShapeDtypeStruct((n_tiles * TM, D_MODEL), F32), jax.ShapeDtypeStruct((rows, cols), BF16)),
        grid=(n_tiles,),
        in_specs=_mixer_specs(ctx_off, layer) + [
            _mod_spec(layer, 5),
            pl.BlockSpec((1, D_MODEL, 2 * D_FF), lambda i: (0, 0, 0), pipeline_mode=resident),
            pl.BlockSpec((1, D_FF, D_MODEL), lambda i: (0, 0, 0), pipeline_mode=resident),
            src_spec],
        out_specs=(pl.BlockSpec((TM, D_MODEL), lambda i: (i, 0)), dst_spec),
        scratch_shapes=[pltpu.VMEM((TM, D_FF), BF16)],
        compiler_params=pltpu.CompilerParams(vmem_limit_bytes=VMEM_LIMIT),
        name="outproj_ffn",
    )(ret_l, ret_c, pool_l, pool_c, w_out, x_lat, x_ctx, mod5, norm_g, mod5, mod5, mod5, w13, w2, stacked)


def _moe_plan(counts, n_steps):
    cnt = counts.astype(jnp.int32)
    end = jnp.cumsum(cnt)
    start = end - cnt
    first = start // TM
    visits = jnp.where(cnt > 0, (end - 1) // TM - first + 1, 0)
    visit_end = jnp.cumsum(visits)
    visit_start = visit_end - visits
    total = visit_end[-1]
    step = jnp.minimum(jnp.arange(n_steps, dtype=jnp.int32), total - 1)
    eid = jnp.minimum(jnp.sum(step[:, None] >= visit_end[None, :], axis=1), N_EXPERTS - 1).astype(jnp.int32)
    tile = first[eid] + step - visit_start[eid]
    lo = jnp.clip(start[eid] - tile * TM, 0, TM)
    hi = jnp.clip(end[eid] - tile * TM, 0, TM)
    hi = jnp.where(jnp.arange(n_steps) < total, hi, lo)
    return start, tile.astype(jnp.int32), eid, lo.astype(jnp.int32), hi.astype(jnp.int32)


def _row_copy_wait(src, dst, sem):
    pltpu.make_async_copy(src, dst, sem).wait()


def _dispatch_kernel(pos_ref, h_ref, xs_hbm, sem):
    i = pl.program_id(0)
    n = 0
    for r in range(TM):
        for k in range(2):
            p = pos_ref[i, k * TM + r]
            pltpu.make_async_copy(h_ref.at[r], xs_hbm.at[p], sem).start(priority=n % 2)
            n += 1
    for k in range(2):
        _row_copy_wait(h_ref, xs_hbm.at[pl.ds(0, TM)], sem)


def _dispatch(pos, h, n_tiles):
    return pl.pallas_call(
        _dispatch_kernel,
        out_shape=jax.ShapeDtypeStruct((2 * n_tiles * TM, *ROW_TILE), F32),
        grid_spec=pltpu.PrefetchScalarGridSpec(
            num_scalar_prefetch=1,
            grid=(n_tiles,),
            in_specs=[pl.BlockSpec((TM, *ROW_TILE), lambda i, pos: (i, 0, 0))],
            out_specs=pl.BlockSpec(memory_space=pl.ANY),
            scratch_shapes=[pltpu.SemaphoreType.DMA]),
        name="dispatch",
    )(pos, h)


def _expert_ffn_kernel(tile_ref, eid_ref, lo_ref, hi_ref, x_ref, w13_ref, w2_ref, o_ref, act_scr):
    s = pl.program_id(0)
    lo, hi = lo_ref[s], hi_ref[s]

    @pl.when(hi > lo)
    def _():
        row = lax.broadcasted_iota(jnp.int32, (TM, D_MODEL), 0)
        x = x_ref[...].reshape(TM, D_MODEL)
        h = jnp.where((row >= lo) & (row < hi), x, 0.0).astype(BF16)
        y = _swiglu(h, w13_ref, w2_ref, act_scr).reshape(TM, *ROW_TILE)

        @pl.when(lo == 0)
        def _():
            o_ref[...] = y

        @pl.when(lo > 0)
        def _():
            o_ref[...] += y


def _expert_ffn(plan, xs, w13, w2, n_steps):
    _, tile, eid, lo, hi = plan
    row = pl.BlockSpec((TM, *ROW_TILE), lambda s, tile, eid, lo, hi: (tile[s], 0, 0))
    return pl.pallas_call(
        _expert_ffn_kernel,
        out_shape=jax.ShapeDtypeStruct(xs.shape, F32),
        grid_spec=pltpu.PrefetchScalarGridSpec(
            num_scalar_prefetch=4,
            grid=(n_steps,),
            in_specs=[row,
                      pl.BlockSpec((1, D_MODEL, 2 * D_FF), lambda s, tile, eid, lo, hi: (eid[s], 0, 0)),
                      pl.BlockSpec((1, D_FF, D_MODEL), lambda s, tile, eid, lo, hi: (eid[s], 0, 0))],
            out_specs=row,
            scratch_shapes=[pltpu.VMEM((TM, D_FF), BF16)]),
        compiler_params=pltpu.CompilerParams(vmem_limit_bytes=VMEM_LIMIT),
        name="expert_ffn",
    )(tile, eid, lo, hi, xs, w13, w2)


def _combine_kernel(pos_ref, y_hbm, route_ref, x_ref, gt_ref, g_ref, o_ref, ybuf, sem, *, n_tiles, final_norm):
    i = pl.program_id(0)
    slot = i % 2

    def request(tile, into):
        n = 0
        for r in range(TM):
            for k in range(2):
                p = pos_ref[tile, k * TM + r]
                pltpu.make_async_copy(y_hbm.at[p], ybuf.at[into, k, r], sem.at[into]).start(priority=n % 2)
                n += 1

    @pl.when(i == 0)
    def _():
        request(0, 0)

    @pl.when(i + 1 < n_tiles)
    def _():
        request(i + 1, 1 - slot)

    for k in range(2):
        _row_copy_wait(y_hbm.at[pl.ds(0, TM)], ybuf.at[slot, k], sem.at[slot])
    w1 = route_ref[:, 4:5]
    w2 = route_ref[:, 5:6]
    y1 = ybuf[slot, 0].reshape(TM, D_MODEL)
    y2 = ybuf[slot, 1].reshape(TM, D_MODEL)
    x = x_ref[...] + gt_ref[...] * (w1 * y1 + w2 * y2)
    if final_norm:
        ms = jnp.mean(x * x, axis=-1, keepdims=True)
        x = x * lax.rsqrt(ms + NORM_EPS) * g_ref[...]
    o_ref[...] = x


def _combine(pos, y, route, x_all, mod5, final_g, layer, n_tiles, final_norm):
    row = pl.BlockSpec((TM, D_MODEL), lambda i, pos: (i, 0))
    return pl.pallas_call(
        functools.partial(_combine_kernel, n_tiles=n_tiles, final_norm=final_norm),
        out_shape=jax.ShapeDtypeStruct((n_tiles * TM, D_MODEL), F32),
        grid_spec=pltpu.PrefetchScalarGridSpec(
            num_scalar_prefetch=1,
            grid=(n_tiles,),
            in_specs=[pl.BlockSpec(memory_space=pl.ANY),
                      pl.BlockSpec((TM, 128), lambda i, pos: (i, 0)), row, _mod_spec(layer, 5),
                      pl.BlockSpec((1, D_MODEL), lambda i, pos: (0, 0))],
            out_specs=row,
            scratch_shapes=[pltpu.VMEM((2, 2, TM, *ROW_TILE), F32), pltpu.SemaphoreType.DMA((2,))]),
        compiler_params=pltpu.CompilerParams(dimension_semantics=("arbitrary",), vmem_limit_bytes=VMEM_LIMIT),
        name="combine",
    )(pos, y, route, x_all, mod5, final_g)


def _moe(h, route, counts, x_all, mod5, w13, w2, final_g, layer, n_tiles, final_norm):
    n_steps = 2 * n_tiles + N_EXPERTS
    plan = _moe_plan(counts[0, :N_EXPERTS], n_steps)
    meta = route[:, :4].astype(jnp.int32)
    pos = plan[0][meta[:, 0:2]] + meta[:, 2:4]
    pos = pos.reshape(n_tiles, TM, 2).transpose(0, 2, 1).reshape(n_tiles, 2 * TM)
    xs = _dispatch(pos, h, n_tiles)
    ys = _expert_ffn(plan, xs, w13, w2, n_steps)
    return _combine(pos, ys, route, x_all, mod5, final_g, layer, n_tiles, final_norm)


def kernel(x, c, ctx, c_ctx, w_ada, b_ada, norm1_g, norm2_g, w_in, ret_decay_logit, ret_gn_g, pool_w, pool_scale,
           w_out, ffn_w13, ffn_w2, router_w, moe_w13, moe_w2, final_norm_g):
    assert DEPTH % 2 == 0, "the final norm is fused into the last routed-expert combine"
    c_all = jnp.concatenate([c, c_ctx[None, :], jnp.zeros((MOD_ROWS - BATCH - 1, D_MODEL), F32)], axis=0)
    mod5 = _ada_mod(c_all, w_ada, b_ada).reshape(DEPTH, MOD_ROWS, 6, 1, D_MODEL)
    tables = _rope_tables()
    log_gamma = jax.nn.log_sigmoid(ret_decay_logit.astype(F32))
    router_pad = jnp.pad(router_w, ((0, 0), (0, 0), (0, 128 - N_EXPERTS))).astype(BF16)
    moe_w13_rows = moe_w13.reshape(-1, 2 * D_FF)
    moe_w2_rows = moe_w2.reshape(-1, D_MODEL)
    final_g = final_norm_g[None, :]

    x_lat, x_ctx, ctx_off = x.reshape(N_LAT, D_MODEL), ctx.reshape(N_CTX, D_MODEL), 0
    for l in range(DEPTH):
        last = l == DEPTH - 1
        n_tiles = LAT_TILES if last else LAT_TILES + CTX_TILES
        i = l // 2
        if l % 2 == 0:
            qkvgp, w2_b = _inproj(x_lat, x_ctx, ctx_off, norm1_g[l][None, :], mod5, w_in[l].astype(BF16), tables, l,
                                  LAT_TILES + CTX_TILES, cast=(moe_w2_rows, i))
        else:
            qkvgp = _inproj(x_lat, x_ctx, ctx_off, norm1_g[l][None, :], mod5, w_in[l].astype(BF16), tables, l,
                            LAT_TILES + CTX_TILES)
        ret_l, ret_c = _retention(qkvgp, log_gamma[l], ret_gn_g[l][None, :])
        pool_l, pool_c = _pool(qkvgp, pool_w[l].astype(BF16), pool_scale[l][None, :])
        if l % 2 == 0:
            x_all, w13_b = _outproj_ffn(ret_l, ret_c, pool_l, pool_c, w_out[l].astype(BF16), x_lat, x_ctx, ctx_off,
                                        mod5, norm2_g[l][None, :], ffn_w13[i][None].astype(BF16),
                                        ffn_w2[i][None].astype(BF16), (moe_w13_rows, i), l, n_tiles)
        else:
            x_all, h2, route, counts = _outproj_router(ret_l, ret_c, pool_l, pool_c, w_out[l].astype(BF16), x_lat,
                                                       x_ctx, ctx_off, mod5, norm2_g[l][None, :], router_pad[i],
                                                       l, n_tiles)
            x_all = _moe(h2, route, counts, x_all, mod5, w13_b.reshape(N_EXPERTS, D_MODEL, 2 * D_FF),
                         w2_b.reshape(N_EXPERTS, D_FF, D_MODEL), final_g, l, n_tiles, last)
        x_lat, x_ctx, ctx_off = x_all, x_all, LAT_TILES
    return x_all.reshape(BATCH, SEQ, D_MODEL)
```

```python
import functools

import jax
import jax.numpy as jnp
from jax import lax
from jax.experimental import pallas as pl
from jax.experimental.pallas import tpu as pltpu

F32 = jnp.float32
BF16 = jnp.bfloat16

D_MODEL = 1024
BATCH = 8
SEQ = 4096
DEPTH = 4
GRID_W = 64
GRID_SHIFT = 6
CTX_LEN = 256
RET_WIDTH = 512
POOL_WIDTH = 512
RET_HEADS = 4
HEAD_DIM = 128
CHUNK = 128
ROPE_BASE = 10000.0
POOL_WINDOWS = (2, 4, 8, 16)
POOL_CH = 128
IN_COLS = 4 * RET_WIDTH + POOL_WIDTH
D_FF = 2816
N_EXPERTS = 8
NORM_EPS = 1e-6
GN_EPS = 1e-5

N_LAT = BATCH * SEQ
N_CTX = BATCH * CTX_LEN
N_ROWS = N_LAT + N_CTX
MOD_ROWS = 16
CTX_MOD_ROW = BATCH

TM = 512
LAT_TILES = N_LAT // TM
CTX_TILES = N_CTX // TM
TILES_PER_BATCH = SEQ // TM
FF_CHUNK = 256
LANES = 128
SUBLANES = 8
ROW_TILE = (D_MODEL // LANES, LANES)
ROUTE_EXPERT, ROUTE_RANK, ROUTE_GATE = (0, 1), (2, 3), (4, 5)
VMEM_LIMIT = 56 * 1024 * 1024


def _mod_row(i):
    return jnp.where(i < LAT_TILES, i // TILES_PER_BATCH, CTX_MOD_ROW)


def _mod_spec(layer, part):
    return pl.BlockSpec((None, None, None, 1, D_MODEL),
                        lambda i, *_: (layer, _mod_row(i), part, 0, 0))


def _silu(v):
    return v * jax.nn.sigmoid(v)


def _x_specs(ctx_block_offset):
    return [pl.BlockSpec((TM, D_MODEL), lambda i, *_: (jnp.minimum(i, LAT_TILES - 1), 0)),
            pl.BlockSpec((TM, D_MODEL), lambda i, *_: (ctx_block_offset + jnp.maximum(i - LAT_TILES, 0), 0))]


def _x_tile(xl_ref, xc_ref):
    return jnp.where(pl.program_id(0) < LAT_TILES, xl_ref[...], xc_ref[...])


def _cast_specs(rows, cols, layer_index):
    blk = rows // LAT_TILES

    def step(i):
        return jnp.minimum(i, LAT_TILES - 1)

    return (pl.BlockSpec((blk, cols), lambda i: (layer_index * LAT_TILES + step(i), 0)),
            pl.BlockSpec((blk, cols), lambda i: (step(i), 0)))


def _rms_mod(x, gain, shift, scale):
    ms = jnp.mean(x * x, axis=-1, keepdims=True)
    y = x * lax.rsqrt(ms + NORM_EPS) * gain
    return y * (1.0 + scale) + shift


def _ada_kernel(c_ref, w_ref, b_ref, o_ref):
    s = _silu(c_ref[...])
    o_ref[0] = jnp.dot(s.astype(BF16), w_ref[0].astype(BF16), preferred_element_type=F32) + b_ref[0]


def _ada_mod(c_all, w_ada, b_ada):
    tn = 1536
    return pl.pallas_call(
        _ada_kernel,
        out_shape=jax.ShapeDtypeStruct((DEPTH, MOD_ROWS, 6 * D_MODEL), F32),
        grid=(DEPTH, 6 * D_MODEL // tn),
        in_specs=[pl.BlockSpec((MOD_ROWS, D_MODEL), lambda l, n: (0, 0)),
                  pl.BlockSpec((1, D_MODEL, tn), lambda l, n: (l, 0, n)),
                  pl.BlockSpec((1, 1, tn), lambda l, n: (l, 0, n))],
        out_specs=pl.BlockSpec((1, MOD_ROWS, tn), lambda l, n: (l, 0, n)),
        compiler_params=pltpu.CompilerParams(vmem_limit_bytes=VMEM_LIMIT),
        name="ada_mod",
    )(c_all, w_ada, b_ada.reshape(DEPTH, 1, 6 * D_MODEL))


def _inproj_kernel(xl_ref, xc_ref, g_ref, sh_ref, sc_ref, w_ref, cos_ref, sa_ref, sb_ref, o_ref):
    h = _rms_mod(_x_tile(xl_ref, xc_ref), g_ref[...], sh_ref[...], sc_ref[...]).astype(BF16)
    _project_in(h, w_ref, cos_ref, sa_ref, sb_ref, o_ref)


def _inproj_cast_kernel(xl_ref, xc_ref, g_ref, sh_ref, sc_ref, w_ref, cos_ref, sa_ref, sb_ref, wf_ref, o_ref, wb_ref):
    _inproj_kernel(xl_ref, xc_ref, g_ref, sh_ref, sc_ref, w_ref, cos_ref, sa_ref, sb_ref, o_ref)
    wb_ref[...] = wf_ref[...].astype(BF16)


def _project_in(h, w_ref, cos_ref, sa_ref, sb_ref, o_ref):
    cos, sa, sb = cos_ref[...], sa_ref[...], sb_ref[...]
    k_scale = HEAD_DIM ** -0.5
    for part, mul in ((0, 1.0), (1, k_scale)):
        z = jnp.dot(h, w_ref[:, part * RET_WIDTH:(part + 1) * RET_WIDTH], preferred_element_type=F32)
        for hh in range(RET_HEADS):
            t = z[:, hh * HEAD_DIM:(hh + 1) * HEAD_DIM]
            r = t * cos + pltpu.roll(t, 96, axis=1) * sa + pltpu.roll(t, 32, axis=1) * sb
            if mul != 1.0:
                r = r * mul
            col = part * RET_WIDTH + hh * HEAD_DIM
            o_ref[:, col:col + HEAD_DIM] = r.astype(BF16)
    z = jnp.dot(h, w_ref[:, 2 * RET_WIDTH:], preferred_element_type=F32)
    o_ref[:, 2 * RET_WIDTH:] = z.astype(BF16)


def _rope_tables():
    half = HEAD_DIM // 2
    inv_freq = ROPE_BASE ** (-jnp.arange(0, half, 2, dtype=F32) / half)
    t = jnp.arange(SEQ)
    rows, cols = (t // GRID_W).astype(F32), (t % GRID_W).astype(F32)
    ang_r = rows[:, None] * inv_freq[None, :]
    ang_c = cols[:, None] * inv_freq[None, :]
    zero = jnp.zeros_like(ang_r)
    cos = jnp.concatenate([jnp.cos(ang_r), jnp.cos(ang_r), jnp.cos(ang_c), jnp.cos(ang_c)], axis=1)
    sa = jnp.concatenate([-jnp.sin(ang_r), zero, -jnp.sin(ang_c), zero], axis=1)
    sb = jnp.concatenate([zero, jnp.sin(ang_r), zero, jnp.sin(ang_c)], axis=1)
    ident = jnp.ones((TM, HEAD_DIM), F32)
    none = jnp.zeros((TM, HEAD_DIM), F32)
    return (jnp.concatenate([cos, ident]), jnp.concatenate([sa, none]), jnp.concatenate([sb, none]))


def _inproj(x_lat, x_ctx, ctx_off, norm_g, mod5, w_in, tables, layer, n_tiles, cast=None):
    tab_spec = pl.BlockSpec((TM, HEAD_DIM),
                            lambda i: (jnp.where(i < LAT_TILES, i % TILES_PER_BATCH, TILES_PER_BATCH), 0))
    body = _inproj_kernel
    in_specs = _x_specs(ctx_off) + [
        pl.BlockSpec((1, D_MODEL), lambda i: (0, 0)),
        _mod_spec(layer, 0), _mod_spec(layer, 1),
        pl.BlockSpec((D_MODEL, IN_COLS), lambda i: (0, 0)),
        tab_spec, tab_spec, tab_spec]
    args = [x_lat, x_ctx, norm_g, mod5, mod5, w_in, *tables]
    out_shape = [jax.ShapeDtypeStruct((N_ROWS, IN_COLS), BF16)]
    out_specs = [pl.BlockSpec((TM, IN_COLS), lambda i: (i, 0))]
    if cast is not None:
        stacked, index = cast
        rows, cols = stacked.shape[0] // (DEPTH // 2), stacked.shape[1]
        src_spec, dst_spec = _cast_specs(rows, cols, index)
        body = _inproj_cast_kernel
        in_specs.append(src_spec)
        args.append(stacked)
        out_shape.append(jax.ShapeDtypeStruct((rows, cols), BF16))
        out_specs.append(dst_spec)
    out = pl.pallas_call(
        body, out_shape=tuple(out_shape), grid=(n_tiles,), in_specs=in_specs, out_specs=tuple(out_specs),
        compiler_params=pltpu.CompilerParams(vmem_limit_bytes=VMEM_LIMIT),
        name="inproj",
    )(*args)
    return out if cast is not None else out[0]


N_CHUNK_LAT = SEQ // CHUNK
N_CHUNK_CTX = CTX_LEN // CHUNK
N_CHUNK = N_CHUNK_LAT + N_CHUNK_CTX
RET_UNROLL = 32


def _ret_kernel(lg_ref, ql, kl, vl, gl, qc, kc, vc, gc, gain_ref, ol, oc, kv_scr, st_scr):
    hd = pl.program_id(1)
    lgf = lg_ref[0, hd]
    lgb = lg_ref[1, hd]
    pi = lax.broadcasted_iota(jnp.int32, (CHUNK, CHUNK), 0).astype(F32)
    pj = lax.broadcasted_iota(jnp.int32, (CHUNK, CHUNK), 1).astype(F32)
    tail_f = jnp.exp(lgf * (CHUNK - 1.0 - pi))
    tail_b = jnp.exp(lgb * pi)
    head_f = jnp.exp(lgf * (pi + 1.0))
    head_b = jnp.exp(lgb * (CHUNK - pi))
    dif = pi - pj
    decay = jnp.where(dif >= 0, jnp.exp(lgf * jnp.maximum(dif, 0.0)), jnp.exp(lgb * jnp.maximum(-dif, 0.0)))
    zeros = jnp.zeros((CHUNK, CHUNK), F32)
    gchunk_f = jnp.exp(zeros + lgf * CHUNK)
    gchunk_b = jnp.exp(zeros + lgb * CHUNK)
    gain = gain_ref[...]

    def kv_of(k, v):
        kf = k.astype(F32)
        kk = jnp.concatenate([(kf * tail_f).astype(BF16), (kf * tail_b).astype(BF16)], axis=1)
        return lax.dot_general(kk, v, (((0,), (0,)), ((), ())), preferred_element_type=F32)

    for c in range(N_CHUNK_CTX):
        kv_scr[c] = kv_of(kc[c * CHUNK:(c + 1) * CHUNK, :], vc[c * CHUNK:(c + 1) * CHUNK, :])

    def kv_body(c, carry):
        r = pl.multiple_of(c * CHUNK, CHUNK)
        kv_scr[N_CHUNK_CTX + c] = kv_of(kl[pl.ds(r, CHUNK), :], vl[pl.ds(r, CHUNK), :])
        return carry

    lax.fori_loop(0, N_CHUNK_LAT, kv_body, 0, unroll=RET_UNROLL)

    def fwd_body(c, s):
        st_scr[c, 0:HEAD_DIM, :] = s.astype(BF16)
        return gchunk_f * s + kv_scr[c, 0:HEAD_DIM, :]

    lax.fori_loop(0, N_CHUNK, fwd_body, zeros)

    def bwd_step(c, s):
        st_scr[c, HEAD_DIM:2 * HEAD_DIM, :] = s.astype(BF16)
        return gchunk_b * s + kv_scr[c, HEAD_DIM:2 * HEAD_DIM, :]

    s = zeros
    for c in reversed(range(N_CHUNK_CTX)):
        s = bwd_step(c, s)
    lax.fori_loop(0, N_CHUNK_LAT, lambda t, s: bwd_step(N_CHUNK - 1 - t, s), s)

    def out_of(q, k, v, g, st):
        sc = lax.dot_general(q, k, (((1,), (1,)), ((), ())), preferred_element_type=F32)
        intra = jnp.dot((sc * decay).astype(BF16), v, preferred_element_type=F32)
        qf = q.astype(F32)
        qq = jnp.concatenate([(qf * head_f).astype(BF16), (qf * head_b).astype(BF16)], axis=1)
        o = intra + jnp.dot(qq, st, preferred_element_type=F32)
        mu = jnp.mean(o, axis=-1, keepdims=True)
        var = jnp.mean(jnp.square(o - mu), axis=-1, keepdims=True)
        on = (o - mu) * lax.rsqrt(var + GN_EPS)
        return (on * gain * _silu(g.astype(F32))).astype(BF16)

    for c in range(N_CHUNK_CTX):
        sl = slice(c * CHUNK, (c + 1) * CHUNK)
        oc[sl, :] = out_of(qc[sl, :], kc[sl, :], vc[sl, :], gc[sl, :], st_scr[c])

    def out_body(c, carry):
        r = pl.multiple_of(c * CHUNK, CHUNK)
        sl = pl.ds(r, CHUNK)
        ol[sl, :] = out_of(ql[sl, :], kl[sl, :], vl[sl, :], gl[sl, :], st_scr[N_CHUNK_CTX + c])
        return carry

    lax.fori_loop(0, N_CHUNK_LAT, out_body, 0, unroll=RET_UNROLL)


def _retention(qkvgp, log_gamma, gn_g):
    def lat(part):
        return pl.BlockSpec((SEQ, HEAD_DIM), lambda b, h: (b, part * RET_HEADS + h))

    def ctx(part):
        return pl.BlockSpec((CTX_LEN, HEAD_DIM), lambda b, h: (N_LAT // CTX_LEN + b, part * RET_HEADS + h))

    return pl.pallas_call(
        _ret_kernel,
        out_shape=(jax.ShapeDtypeStruct((N_LAT, RET_WIDTH), BF16),
                   jax.ShapeDtypeStruct((N_CTX, RET_WIDTH), BF16)),
        grid=(BATCH, RET_HEADS),
        in_specs=[pl.BlockSpec(memory_space=pltpu.SMEM),
                  lat(0), lat(1), lat(2), lat(3), ctx(0), ctx(1), ctx(2), ctx(3),
                  pl.BlockSpec((1, HEAD_DIM), lambda b, h: (0, h))],
        out_specs=(pl.BlockSpec((SEQ, HEAD_DIM), lambda b, h: (b, h)),
                   pl.BlockSpec((CTX_LEN, HEAD_DIM), lambda b, h: (b, h))),
        scratch_shapes=[pltpu.VMEM((N_CHUNK, 2 * HEAD_DIM, HEAD_DIM), F32),
                        pltpu.VMEM((N_CHUNK, 2 * HEAD_DIM, HEAD_DIM), BF16)],
        compiler_params=pltpu.CompilerParams(vmem_limit_bytes=VMEM_LIMIT),
        name="retention",
    )(log_gamma, *([qkvgp] * 8), gn_g)


POOL_BLK = 256


def _window_count(idx, w, n):
    return (jnp.minimum(idx - w // 2 + w, n) - jnp.maximum(idx - w // 2, 0)).astype(F32)


def _row_window_mean(s, w):
    size = [min(r - w // 2 + w, GRID_W) - max(r - w // 2, 0) for r in range(GRID_W)]
    parts, r0 = [], 0
    for r in range(1, GRID_W + 1):
        if r == GRID_W or size[r] != size[r0]:
            parts.append(s[r0 * GRID_W:r * GRID_W] * (1.0 / size[r0]))
            r0 = r
    return jnp.concatenate(parts, axis=0)


def _pool_kernel(pl_ref, pc_ref, w_ref, scale_ref, ol, oc):
    ti = lax.broadcasted_iota(jnp.int32, (POOL_BLK, POOL_BLK), 0)
    tj = lax.broadcasted_iota(jnp.int32, (POOL_BLK, POOL_BLK), 1)
    tok_b = lax.broadcasted_iota(jnp.int32, (POOL_BLK, POOL_CH), 0)
    tok_c = lax.broadcasted_iota(jnp.int32, (CTX_LEN, POOL_CH), 0)
    for gi, w in enumerate(POOL_WINDOWS):
        cols = slice(gi * POOL_CH, (gi + 1) * POOL_CH)
        lo_off, hi_off = -(w // 2), w - 1 - w // 2
        wmat = w_ref[gi]
        scale = scale_ref[:, cols]

        ci, cj = ti & (GRID_W - 1), tj & (GRID_W - 1)
        band = ((ti >> GRID_SHIFT == tj >> GRID_SHIFT) & (cj >= ci + lo_off) & (cj <= ci + hi_off))
        band = jnp.where(band, 1.0, 0.0).astype(BF16)
        col_inv = 1.0 / _window_count(tok_b & (GRID_W - 1), w, GRID_W)
        col_mean = [jnp.dot(band, pl_ref[blk * POOL_BLK:(blk + 1) * POOL_BLK, cols],
                            preferred_element_type=F32) * col_inv for blk in range(SEQ // POOL_BLK)]
        pad = jnp.zeros((w // 2 * GRID_W, POOL_CH), F32)
        s = jnp.concatenate([pad] + col_mean + [pad], axis=0)
        span = 1
        while span < w:
            n = s.shape[0] - span * GRID_W
            s = s[:n] + s[span * GRID_W:]
            span *= 2
        box_mean = _row_window_mean(s[:SEQ], w)
        for blk in range(SEQ // POOL_BLK):
            rs = slice(blk * POOL_BLK, (blk + 1) * POOL_BLK)
            d = (box_mean[rs] - pl_ref[rs, cols].astype(F32)).astype(BF16)
            y = jnp.dot(d, wmat, preferred_element_type=F32) * scale
            ol[rs, cols] = y.astype(BF16)

        band_c = jnp.where((tj >= ti + lo_off) & (tj <= ti + hi_off), 1.0, 0.0).astype(BF16)
        mc = jnp.dot(band_c, pc_ref[:, cols], preferred_element_type=F32) / _window_count(tok_c, w, CTX_LEN)
        uc = pc_ref[:, cols].astype(F32)
        dc = (mc - uc).astype(BF16)
        oc[:, cols] = (jnp.dot(dc, wmat, preferred_element_type=F32) * scale).astype(BF16)


def _pool(qkvgp, pool_w, pool_scale):
    pcol = 4 * RET_WIDTH // POOL_WIDTH
    return pl.pallas_call(
        _pool_kernel,
        out_shape=(jax.ShapeDtypeStruct((N_LAT, POOL_WIDTH), BF16),
                   jax.ShapeDtypeStruct((N_CTX, POOL_WIDTH), BF16)),
        grid=(BATCH,),
        in_specs=[pl.BlockSpec((SEQ, POOL_WIDTH), lambda b: (b, pcol)),
                  pl.BlockSpec((CTX_LEN, POOL_WIDTH), lambda b: (N_LAT // CTX_LEN + b, pcol)),
                  pl.BlockSpec((len(POOL_WINDOWS), POOL_CH, POOL_CH), lambda b: (0, 0, 0)),
                  pl.BlockSpec((1, POOL_WIDTH), lambda b: (0, 0))],
        out_specs=(pl.BlockSpec((SEQ, POOL_WIDTH), lambda b: (b, 0)),
                   pl.BlockSpec((CTX_LEN, POOL_WIDTH), lambda b: (b, 0))),
        compiler_params=pltpu.CompilerParams(vmem_limit_bytes=VMEM_LIMIT),
        name="pool",
    )(qkvgp, qkvgp, pool_w, pool_scale)


def _mix_out(i, rl, rc, pl_, pc, w_ref, xl_ref, xc_ref, gt_ref):
    is_lat = i < LAT_TILES
    ret = jnp.where(is_lat, rl[...], rc[...])
    pool = jnp.where(is_lat, pl_[...], pc[...])
    y = (jnp.dot(ret, w_ref[0:RET_WIDTH, :], preferred_element_type=F32)
         + jnp.dot(pool, w_ref[RET_WIDTH:, :], preferred_element_type=F32))
    return _x_tile(xl_ref, xc_ref) + gt_ref[...] * y


def _outproj_router_kernel(rl, rc, pl_, pc, w_ref, xl_ref, xc_ref, gt_ref, g_ref, sh_ref, sc_ref, wr_ref,
                           xo_ref, h_ref, route_ref, cnt_ref, carry_scr):
    i = pl.program_id(0)
    x = _mix_out(i, rl, rc, pl_, pc, w_ref, xl_ref, xc_ref, gt_ref)
    xo_ref[...] = x
    h = _rms_mod(x, g_ref[...], sh_ref[...], sc_ref[...])
    h_ref[...] = h.reshape(TM, *ROW_TILE)
    logits = jnp.dot(h.astype(BF16), wr_ref[...], preferred_element_type=F32)
    lane = lax.broadcasted_iota(jnp.int32, logits.shape, 1)
    neg = jnp.float32(-jnp.inf)
    logits = jnp.where(lane < N_EXPERTS, logits, neg)
    m1 = jnp.max(logits, axis=-1, keepdims=True)
    i1 = jnp.min(jnp.where(logits == m1, lane, LANES), axis=-1, keepdims=True)
    rest = jnp.where(lane == i1, neg, logits)
    m2 = jnp.max(rest, axis=-1, keepdims=True)
    i2 = jnp.min(jnp.where(rest == m2, lane, LANES), axis=-1, keepdims=True)
    e2 = jnp.exp(m2 - m1)
    den = 1.0 + e2

    @pl.when(i == 0)
    def _():
        carry_scr[...] = jnp.zeros_like(carry_scr)

    sel1, sel2 = lane == i1, lane == i2
    picks = jnp.where(sel1 | sel2, 1.0, 0.0)
    ti = lax.broadcasted_iota(jnp.int32, (TM, TM), 0)
    tj = lax.broadcasted_iota(jnp.int32, (TM, TM), 1)
    earlier = jnp.where(tj < ti, 1.0, 0.0).astype(BF16)
    before = jnp.dot(earlier, picks.astype(BF16), preferred_element_type=F32) + carry_scr[...]
    r1 = jnp.sum(jnp.where(sel1, before, 0.0), axis=-1, keepdims=True)
    r2 = jnp.sum(jnp.where(sel2, before, 0.0), axis=-1, keepdims=True)
    carry = carry_scr[...] + jnp.sum(picks, axis=0, keepdims=True)
    carry_scr[...] = carry
    cnt_ref[...] = jnp.broadcast_to(carry, cnt_ref.shape)

    fields = zip(ROUTE_EXPERT + ROUTE_RANK + ROUTE_GATE, (i1.astype(F32), i2.astype(F32), r1, r2, 1.0 / den, e2 / den))
    route = jnp.zeros(logits.shape, F32)
    for k, v in fields:
        route = jnp.where(lane == k, v, route)
    route_ref[...] = route


def _mixer_specs(ctx_off, layer):
    def lat_spec():
        return pl.BlockSpec((TM, RET_WIDTH), lambda i: (jnp.minimum(i, LAT_TILES - 1), 0))

    def ctx_spec():
        return pl.BlockSpec((TM, RET_WIDTH), lambda i: (jnp.maximum(i - LAT_TILES, 0), 0))

    return [lat_spec(), ctx_spec(), lat_spec(), ctx_spec(),
            pl.BlockSpec((D_MODEL, D_MODEL), lambda i: (0, 0), pipeline_mode=pl.Buffered(1)),
            *_x_specs(ctx_off), _mod_spec(layer, 2),
            pl.BlockSpec((1, D_MODEL), lambda i: (0, 0)),
            _mod_spec(layer, 3), _mod_spec(layer, 4)]


def _outproj_router(ret_l, ret_c, pool_l, pool_c, w_out, x_lat, x_ctx, ctx_off, mod5, norm_g, router_w, layer,
                    n_tiles):
    row = pl.BlockSpec((TM, D_MODEL), lambda i: (i, 0))
    n_rows = n_tiles * TM
    return pl.pallas_call(
        _outproj_router_kernel,
        out_shape=(jax.ShapeDtypeStruct((n_rows, D_MODEL), F32), jax.ShapeDtypeStruct((n_rows, *ROW_TILE), F32),
                   jax.ShapeDtypeStruct((n_rows, LANES), F32), jax.ShapeDtypeStruct((SUBLANES, LANES), F32)),
        grid=(n_tiles,),
        in_specs=_mixer_specs(ctx_off, layer) + [pl.BlockSpec((D_MODEL, LANES), lambda i: (0, 0))],
        out_specs=(row, pl.BlockSpec((TM, *ROW_TILE), lambda i: (i, 0, 0)),
                   pl.BlockSpec((TM, LANES), lambda i: (i, 0)), pl.BlockSpec((SUBLANES, LANES), lambda i: (0, 0))),
        scratch_shapes=[pltpu.VMEM((1, LANES), F32)],
        compiler_params=pltpu.CompilerParams(dimension_semantics=("arbitrary",), vmem_limit_bytes=VMEM_LIMIT),
        name="outproj",
    )(ret_l, ret_c, pool_l, pool_c, w_out, x_lat, x_ctx, mod5, norm_g, mod5, mod5, router_w)


def _swiglu(h, w13_ref, w2_ref, act_scr):
    for j in range(D_FF // FF_CHUNK):
        u = jnp.dot(h, w13_ref[0, :, j * FF_CHUNK:(j + 1) * FF_CHUNK], preferred_element_type=F32)
        g = jnp.dot(h, w13_ref[0, :, D_FF + j * FF_CHUNK:D_FF + (j + 1) * FF_CHUNK], preferred_element_type=F32)
        act_scr[:, j * FF_CHUNK:(j + 1) * FF_CHUNK] = (_silu(g) * u).astype(BF16)
    return jnp.dot(act_scr[...], w2_ref[0], preferred_element_type=F32)


def _outproj_ffn_kernel(rl, rc, pl_, pc, w_ref, xl_ref, xc_ref, gt1_ref, g_ref, sh_ref, sc_ref, gt2_ref,
                        w13_ref, w2_ref, wf_ref, o_ref, wb_ref, act_scr):
    x = _mix_out(pl.program_id(0), rl, rc, pl_, pc, w_ref, xl_ref, xc_ref, gt1_ref)
    h = _rms_mod(x, g_ref[...], sh_ref[...], sc_ref[...]).astype(BF16)
    o_ref[...] = x + gt2_ref[...] * _swiglu(h, w13_ref, w2_ref, act_scr)
    wb_ref[...] = wf_ref[...].astype(BF16)


def _outproj_ffn(ret_l, ret_c, pool_l, pool_c, w_out, x_lat, x_ctx, ctx_off, mod5, norm_g, w13, w2, cast, layer,
                 n_tiles):
    resident = pl.Buffered(1)
    stacked, index = cast
    rows, cols = stacked.shape[0] // (DEPTH // 2), stacked.shape[1]
    src_spec, dst_spec = _cast_specs(rows, cols, index)
    return pl.pallas_call(
        _outproj_ffn_kernel,
        out_shape=(jax.ShapeDtypeStruct((n_tiles * TM, D_MODEL), F32), jax.ShapeDtypeStruct((rows, cols), BF16)),
        grid=(n_tiles,),
        in_specs=_mixer_specs(ctx_off, layer) + [
            _mod_spec(layer, 5),
            pl.BlockSpec((1, D_MODEL, 2 * D_FF), lambda i: (0, 0, 0), pipeline_mode=resident),
            pl.BlockSpec((1, D_FF, D_MODEL), lambda i: (0, 0, 0), pipeline_mode=resident),
            src_spec],
        out_specs=(pl.BlockSpec((TM, D_MODEL), lambda i: (i, 0)), dst_spec),
        scratch_shapes=[pltpu.VMEM((TM, D_FF), BF16)],
        compiler_params=pltpu.CompilerParams(vmem_limit_bytes=VMEM_LIMIT),
        name="outproj_ffn",
    )(ret_l, ret_c, pool_l, pool_c, w_out, x_lat, x_ctx, mod5, norm_g, mod5, mod5, mod5, w13, w2, stacked)


def _moe_plan(counts, n_steps):
    cnt = counts.astype(jnp.int32)
    end = jnp.cumsum(cnt)
    start = end - cnt
    first = start // TM
    visits = jnp.where(cnt > 0, (end - 1) // TM - first + 1, 0)
    visit_end = jnp.cumsum(visits)
    visit_start = visit_end - visits
    total = visit_end[-1]
    step = jnp.minimum(jnp.arange(n_steps, dtype=jnp.int32), total - 1)
    eid = jnp.minimum(jnp.sum(step[:, None] >= visit_end[None, :], axis=1), N_EXPERTS - 1).astype(jnp.int32)
    tile = first[eid] + step - visit_start[eid]
    lo = jnp.clip(start[eid] - tile * TM, 0, TM)
    hi = jnp.clip(end[eid] - tile * TM, 0, TM)
    hi = jnp.where(jnp.arange(n_steps) < total, hi, lo)
    return start, tile.astype(jnp.int32), eid, lo.astype(jnp.int32), hi.astype(jnp.int32)


def _row_copy_wait(src, dst, sem):
    pltpu.make_async_copy(src, dst, sem).wait()


def _dispatch_kernel(pos_ref, h_ref, xs_hbm, sem):
    i = pl.program_id(0)
    n = 0
    for r in range(TM):
        for k in range(2):
            p = pos_ref[i, k * TM + r]
            pltpu.make_async_copy(h_ref.at[r], xs_hbm.at[p], sem).start(priority=n % 2)
            n += 1
    for k in range(2):
        _row_copy_wait(h_ref, xs_hbm.at[pl.ds(0, TM)], sem)


def _dispatch(pos, h, n_tiles):
    return pl.pallas_call(
        _dispatch_kernel,
        out_shape=jax.ShapeDtypeStruct((2 * n_tiles * TM, *ROW_TILE), F32),
        grid_spec=pltpu.PrefetchScalarGridSpec(
            num_scalar_prefetch=1,
            grid=(n_tiles,),
            in_specs=[pl.BlockSpec((TM, *ROW_TILE), lambda i, pos: (i, 0, 0))],
            out_specs=pl.BlockSpec(memory_space=pl.ANY),
            scratch_shapes=[pltpu.SemaphoreType.DMA]),
        name="dispatch",
    )(pos, h)


def _expert_ffn_kernel(tile_ref, eid_ref, lo_ref, hi_ref, x_ref, w13_ref, w2_ref, o_ref, act_scr):
    s = pl.program_id(0)
    lo, hi = lo_ref[s], hi_ref[s]

    @pl.when(hi > lo)
    def _():
        row = lax.broadcasted_iota(jnp.int32, (TM, D_MODEL), 0)
        x = x_ref[...].reshape(TM, D_MODEL)
        h = jnp.where((row >= lo) & (row < hi), x, 0.0).astype(BF16)
        y = _swiglu(h, w13_ref, w2_ref, act_scr).reshape(TM, *ROW_TILE)

        @pl.when(lo == 0)
        def _():
            o_ref[...] = y

        @pl.when(lo > 0)
        def _():
            o_ref[...] += y


def _expert_ffn(plan, xs, w13, w2, n_steps):
    _, tile, eid, lo, hi = plan
    row = pl.BlockSpec((TM, *ROW_TILE), lambda s, tile, eid, lo, hi: (tile[s], 0, 0))
    return pl.pallas_call(
        _expert_ffn_kernel,
        out_shape=jax.ShapeDtypeStruct(xs.shape, F32),
        grid_spec=pltpu.PrefetchScalarGridSpec(
            num_scalar_prefetch=4,
            grid=(n_steps,),
            in_specs=[row,
                      pl.BlockSpec((1, D_MODEL, 2 * D_FF), lambda s, tile, eid, lo, hi: (eid[s], 0, 0)),
                      pl.BlockSpec((1, D_FF, D_MODEL), lambda s, tile, eid, lo, hi: (eid[s], 0, 0))],
            out_specs=row,
            scratch_shapes=[pltpu.VMEM((TM, D_FF), BF16)]),
        compiler_params=pltpu.CompilerParams(vmem_limit_bytes=VMEM_LIMIT),
        name="expert_ffn",
    )(tile, eid, lo, hi, xs, w13, w2)


def _combine_kernel(pos_ref, y_hbm, route_ref, x_ref, gt_ref, g_ref, o_ref, ybuf, sem, *, n_tiles, final_norm):
    i = pl.program_id(0)
    slot = i % 2

    def request(tile, into):
        n = 0
        for r in range(TM):
            for k in range(2):
                p = pos_ref[tile, k * TM + r]
                pltpu.make_async_copy(y_hbm.at[p], ybuf.at[into, k, r], sem.at[into]).start(priority=n % 2)
                n += 1

    @pl.when(i == 0)
    def _():
        request(0, 0)

    @pl.when(i + 1 < n_tiles)
    def _():
        request(i + 1, 1 - slot)

    for k in range(2):
        _row_copy_wait(y_hbm.at[pl.ds(0, TM)], ybuf.at[slot, k], sem.at[slot])
    w1 = route_ref[:, ROUTE_GATE[0]:ROUTE_GATE[0] + 1]
    w2 = route_ref[:, ROUTE_GATE[1]:ROUTE_GATE[1] + 1]
    y1 = ybuf[slot, 0].reshape(TM, D_MODEL)
    y2 = ybuf[slot, 1].reshape(TM, D_MODEL)
    x = x_ref[...] + gt_ref[...] * (w1 * y1 + w2 * y2)
    if final_norm:
        ms = jnp.mean(x * x, axis=-1, keepdims=True)
        x = x * lax.rsqrt(ms + NORM_EPS) * g_ref[...]
    o_ref[...] = x


def _combine(pos, y, route, x_all, mod5, final_g, layer, n_tiles, final_norm):
    row = pl.BlockSpec((TM, D_MODEL), lambda i, pos: (i, 0))
    return pl.pallas_call(
        functools.partial(_combine_kernel, n_tiles=n_tiles, final_norm=final_norm),
        out_shape=jax.ShapeDtypeStruct((n_tiles * TM, D_MODEL), F32),
        grid_spec=pltpu.PrefetchScalarGridSpec(
            num_scalar_prefetch=1,
            grid=(n_tiles,),
            in_specs=[pl.BlockSpec(memory_space=pl.ANY),
                      pl.BlockSpec((TM, LANES), lambda i, pos: (i, 0)), row, _mod_spec(layer, 5),
                      pl.BlockSpec((1, D_MODEL), lambda i, pos: (0, 0))],
            out_specs=row,
            scratch_shapes=[pltpu.VMEM((2, 2, TM, *ROW_TILE), F32), pltpu.SemaphoreType.DMA((2,))]),
        compiler_params=pltpu.CompilerParams(dimension_semantics=("arbitrary",), vmem_limit_bytes=VMEM_LIMIT),
        name="combine",
    )(pos, y, route, x_all, mod5, final_g)


def _moe(h, route, counts, x_all, mod5, w13, w2, final_g, layer, n_tiles, final_norm):
    n_steps = 2 * n_tiles + N_EXPERTS
    plan = _moe_plan(counts[0, :N_EXPERTS], n_steps)
    expert = route[:, ROUTE_EXPERT[0]:ROUTE_EXPERT[1] + 1].astype(jnp.int32)
    rank = route[:, ROUTE_RANK[0]:ROUTE_RANK[1] + 1].astype(jnp.int32)
    pos = plan[0][expert] + rank
    pos = pos.reshape(n_tiles, TM, 2).transpose(0, 2, 1).reshape(n_tiles, 2 * TM)
    xs = _dispatch(pos, h, n_tiles)
    ys = _expert_ffn(plan, xs, w13, w2, n_steps)
    return _combine(pos, ys, route, x_all, mod5, final_g, layer, n_tiles, final_norm)


def kernel(x, c, ctx, c_ctx, w_ada, b_ada, norm1_g, norm2_g, w_in, ret_decay_logit, ret_gn_g, pool_w, pool_scale,
           w_out, ffn_w13, ffn_w2, router_w, moe_w13, moe_w2, final_norm_g):
    assert DEPTH % 2 == 0, "the final norm is fused into the last routed-expert combine"
    c_all = jnp.concatenate([c, c_ctx[None, :], jnp.zeros((MOD_ROWS - BATCH - 1, D_MODEL), F32)], axis=0)
    mod5 = _ada_mod(c_all, w_ada, b_ada).reshape(DEPTH, MOD_ROWS, 6, 1, D_MODEL)
    tables = _rope_tables()
    log_gamma = jax.nn.log_sigmoid(ret_decay_logit.astype(F32))
    router_pad = jnp.pad(router_w, ((0, 0), (0, 0), (0, LANES - N_EXPERTS))).astype(BF16)
    moe_w13_rows = moe_w13.reshape(-1, 2 * D_FF)
    moe_w2_rows = moe_w2.reshape(-1, D_MODEL)
    final_g = final_norm_g[None, :]

    x_lat, x_ctx, ctx_off = x.reshape(N_LAT, D_MODEL), ctx.reshape(N_CTX, D_MODEL), 0
    for l in range(DEPTH):
        last = l == DEPTH - 1
        n_tiles = LAT_TILES if last else LAT_TILES + CTX_TILES
        i = l // 2
        if l % 2 == 0:
            qkvgp, w2_b = _inproj(x_lat, x_ctx, ctx_off, norm1_g[l][None, :], mod5, w_in[l].astype(BF16), tables, l,
                                  LAT_TILES + CTX_TILES, cast=(moe_w2_rows, i))
        else:
            qkvgp = _inproj(x_lat, x_ctx, ctx_off, norm1_g[l][None, :], mod5, w_in[l].astype(BF16), tables, l,
                            LAT_TILES + CTX_TILES)
        ret_l, ret_c = _retention(qkvgp, log_gamma[l], ret_gn_g[l][None, :])
        pool_l, pool_c = _pool(qkvgp, pool_w[l].astype(BF16), pool_scale[l][None, :])
        if l % 2 == 0:
            x_all, w13_b = _outproj_ffn(ret_l, ret_c, pool_l, pool_c, w_out[l].astype(BF16), x_lat, x_ctx, ctx_off,
                                        mod5, norm2_g[l][None, :], ffn_w13[i][None].astype(BF16),
                                        ffn_w2[i][None].astype(BF16), (moe_w13_rows, i), l, n_tiles)
        else:
            x_all, h2, route, counts = _outproj_router(ret_l, ret_c, pool_l, pool_c, w_out[l].astype(BF16), x_lat,
                                                       x_ctx, ctx_off, mod5, norm2_g[l][None, :], router_pad[i],
                                                       l, n_tiles)
            x_all = _moe(h2, route, counts, x_all, mod5, w13_b.reshape(N_EXPERTS, D_MODEL, 2 * D_FF),
                         w2_b.reshape(N_EXPERTS, D_FF, D_MODEL), final_g, l, n_tiles, last)
        x_lat, x_ctx, ctx_off = x_all, x_all, LAT_TILES
    return x_all.reshape(BATCH, SEQ, D_MODEL)
```

```python
import functools

import jax
import jax.numpy as jnp
from jax import lax
from jax.experimental import pallas as pl
from jax.experimental.pallas import tpu as pltpu

F32 = jnp.float32
BF16 = jnp.bfloat16

D_MODEL = 1024
BATCH = 8
SEQ = 4096
DEPTH = 4
GRID_W = 64
GRID_SHIFT = 6
CTX_LEN = 256
RET_WIDTH = 512
POOL_WIDTH = 512
RET_HEADS = 4
HEAD_DIM = 128
CHUNK = 128
ROPE_BASE = 10000.0
POOL_WINDOWS = (2, 4, 8, 16)
POOL_CH = 128
IN_COLS = 4 * RET_WIDTH + POOL_WIDTH
D_FF = 2816
N_EXPERTS = 8
NORM_EPS = 1e-6
GN_EPS = 1e-5

N_LAT = BATCH * SEQ
N_CTX = BATCH * CTX_LEN
N_ROWS = N_LAT + N_CTX
MOD_ROWS = 16
CTX_MOD_ROW = BATCH

TM = 512
LAT_TILES = N_LAT // TM
CTX_TILES = N_CTX // TM
TILES_PER_BATCH = SEQ // TM
FF_CHUNK = 256
LANES = 128
SUBLANES = 8
ROW_TILE = (D_MODEL // LANES, LANES)
ROUTE_EXPERT, ROUTE_RANK, ROUTE_GATE = (0, 1), (2, 3), (4, 5)
VMEM_LIMIT = 56 * 1024 * 1024


def _mod_row(i):
    return jnp.where(i < LAT_TILES, i // TILES_PER_BATCH, CTX_MOD_ROW)


def _mod_spec(layer, part):
    return pl.BlockSpec((None, None, None, 1, D_MODEL),
                        lambda i, *_: (layer, _mod_row(i), part, 0, 0))


def _silu(v):
    return v * jax.nn.sigmoid(v)


def _x_specs(ctx_block_offset):
    return [pl.BlockSpec((TM, D_MODEL), lambda i, *_: (jnp.minimum(i, LAT_TILES - 1), 0)),
            pl.BlockSpec((TM, D_MODEL), lambda i, *_: (ctx_block_offset + jnp.maximum(i - LAT_TILES, 0), 0))]


def _x_tile(xl_ref, xc_ref):
    return jnp.where(pl.program_id(0) < LAT_TILES, xl_ref[...], xc_ref[...])


def _cast_specs(rows, cols, layer_index):
    blk = rows // LAT_TILES

    def step(i):
        return jnp.minimum(i, LAT_TILES - 1)

    return (pl.BlockSpec((blk, cols), lambda i, *_: (layer_index * LAT_TILES + step(i), 0)),
            pl.BlockSpec((blk, cols), lambda i, *_: (step(i), 0)))


def _rms_mod(x, gain, shift, scale):
    ms = jnp.mean(x * x, axis=-1, keepdims=True)
    y = x * lax.rsqrt(ms + NORM_EPS) * gain
    return y * (1.0 + scale) + shift


def _ada_kernel(c_ref, w_ref, b_ref, o_ref):
    s = _silu(c_ref[...])
    o_ref[0] = jnp.dot(s.astype(BF16), w_ref[0].astype(BF16), preferred_element_type=F32) + b_ref[0]


def _ada_mod(c_all, w_ada, b_ada):
    tn = 1536
    return pl.pallas_call(
        _ada_kernel,
        out_shape=jax.ShapeDtypeStruct((DEPTH, MOD_ROWS, 6 * D_MODEL), F32),
        grid=(DEPTH, 6 * D_MODEL // tn),
        in_specs=[pl.BlockSpec((MOD_ROWS, D_MODEL), lambda l, n: (0, 0)),
                  pl.BlockSpec((1, D_MODEL, tn), lambda l, n: (l, 0, n)),
                  pl.BlockSpec((1, 1, tn), lambda l, n: (l, 0, n))],
        out_specs=pl.BlockSpec((1, MOD_ROWS, tn), lambda l, n: (l, 0, n)),
        compiler_params=pltpu.CompilerParams(vmem_limit_bytes=VMEM_LIMIT),
        name="ada_mod",
    )(c_all, w_ada, b_ada.reshape(DEPTH, 1, 6 * D_MODEL))


def _inproj_kernel(xl_ref, xc_ref, g_ref, sh_ref, sc_ref, w_ref, cos_ref, sa_ref, sb_ref, o_ref):
    h = _rms_mod(_x_tile(xl_ref, xc_ref), g_ref[...], sh_ref[...], sc_ref[...]).astype(BF16)
    _project_in(h, w_ref, cos_ref, sa_ref, sb_ref, o_ref)


def _inproj_cast_kernel(xl_ref, xc_ref, g_ref, sh_ref, sc_ref, w_ref, cos_ref, sa_ref, sb_ref, wf_ref, o_ref, wb_ref):
    _inproj_kernel(xl_ref, xc_ref, g_ref, sh_ref, sc_ref, w_ref, cos_ref, sa_ref, sb_ref, o_ref)
    wb_ref[...] = wf_ref[...].astype(BF16)


def _project_in(h, w_ref, cos_ref, sa_ref, sb_ref, o_ref):
    cos, sa, sb = cos_ref[...], sa_ref[...], sb_ref[...]
    k_scale = HEAD_DIM ** -0.5
    for part, mul in ((0, 1.0), (1, k_scale)):
        z = jnp.dot(h, w_ref[:, part * RET_WIDTH:(part + 1) * RET_WIDTH], preferred_element_type=F32)
        for hh in range(RET_HEADS):
            t = z[:, hh * HEAD_DIM:(hh + 1) * HEAD_DIM]
            r = t * cos + pltpu.roll(t, 96, axis=1) * sa + pltpu.roll(t, 32, axis=1) * sb
            if mul != 1.0:
                r = r * mul
            col = part * RET_WIDTH + hh * HEAD_DIM
            o_ref[:, col:col + HEAD_DIM] = r.astype(BF16)
    z = jnp.dot(h, w_ref[:, 2 * RET_WIDTH:], preferred_element_type=F32)
    o_ref[:, 2 * RET_WIDTH:] = z.astype(BF16)


def _rope_tables():
    half = HEAD_DIM // 2
    inv_freq = ROPE_BASE ** (-jnp.arange(0, half, 2, dtype=F32) / half)
    t = jnp.arange(SEQ)
    rows, cols = (t // GRID_W).astype(F32), (t % GRID_W).astype(F32)
    ang_r = rows[:, None] * inv_freq[None, :]
    ang_c = cols[:, None] * inv_freq[None, :]
    zero = jnp.zeros_like(ang_r)
    cos = jnp.concatenate([jnp.cos(ang_r), jnp.cos(ang_r), jnp.cos(ang_c), jnp.cos(ang_c)], axis=1)
    sa = jnp.concatenate([-jnp.sin(ang_r), zero, -jnp.sin(ang_c), zero], axis=1)
    sb = jnp.concatenate([zero, jnp.sin(ang_r), zero, jnp.sin(ang_c)], axis=1)
    ident = jnp.ones((TM, HEAD_DIM), F32)
    none = jnp.zeros((TM, HEAD_DIM), F32)
    return (jnp.concatenate([cos, ident]), jnp.concatenate([sa, none]), jnp.concatenate([sb, none]))


def _inproj(x_lat, x_ctx, ctx_off, norm_g, mod5, w_in, tables, layer, n_tiles, cast=None):
    tab_spec = pl.BlockSpec((TM, HEAD_DIM),
                            lambda i: (jnp.where(i < LAT_TILES, i % TILES_PER_BATCH, TILES_PER_BATCH), 0))
    body = _inproj_kernel
    in_specs = _x_specs(ctx_off) + [
        pl.BlockSpec((1, D_MODEL), lambda i: (0, 0)),
        _mod_spec(layer, 0), _mod_spec(layer, 1),
        pl.BlockSpec((D_MODEL, IN_COLS), lambda i: (0, 0)),
        tab_spec, tab_spec, tab_spec]
    args = [x_lat, x_ctx, norm_g, mod5, mod5, w_in, *tables]
    out_shape = [jax.ShapeDtypeStruct((N_ROWS, IN_COLS), BF16)]
    out_specs = [pl.BlockSpec((TM, IN_COLS), lambda i: (i, 0))]
    if cast is not None:
        stacked, index = cast
        rows, cols = stacked.shape[0] // (DEPTH // 2), stacked.shape[1]
        src_spec, dst_spec = _cast_specs(rows, cols, index)
        body = _inproj_cast_kernel
        in_specs.append(src_spec)
        args.append(stacked)
        out_shape.append(jax.ShapeDtypeStruct((rows, cols), BF16))
        out_specs.append(dst_spec)
    out = pl.pallas_call(
        body, out_shape=tuple(out_shape), grid=(n_tiles,), in_specs=in_specs, out_specs=tuple(out_specs),
        compiler_params=pltpu.CompilerParams(vmem_limit_bytes=VMEM_LIMIT),
        name="inproj",
    )(*args)
    return out if cast is not None else out[0]


N_CHUNK_LAT = SEQ // CHUNK
N_CHUNK_CTX = CTX_LEN // CHUNK
N_CHUNK = N_CHUNK_LAT + N_CHUNK_CTX
RET_UNROLL = 32


def _ret_kernel(lg_ref, ql, kl, vl, gl, qc, kc, vc, gc, gain_ref, ol, oc, kv_scr, st_scr):
    hd = pl.program_id(1)
    lgf = lg_ref[0, hd]
    lgb = lg_ref[1, hd]
    pi = lax.broadcasted_iota(jnp.int32, (CHUNK, CHUNK), 0).astype(F32)
    pj = lax.broadcasted_iota(jnp.int32, (CHUNK, CHUNK), 1).astype(F32)
    tail_f = jnp.exp(lgf * (CHUNK - 1.0 - pi))
    tail_b = jnp.exp(lgb * pi)
    head_f = jnp.exp(lgf * (pi + 1.0))
    head_b = jnp.exp(lgb * (CHUNK - pi))
    dif = pi - pj
    decay = jnp.where(dif >= 0, jnp.exp(lgf * jnp.maximum(dif, 0.0)), jnp.exp(lgb * jnp.maximum(-dif, 0.0)))
    zeros = jnp.zeros((CHUNK, CHUNK), F32)
    gchunk_f = jnp.exp(zeros + lgf * CHUNK)
    gchunk_b = jnp.exp(zeros + lgb * CHUNK)
    gain = gain_ref[...]

    def kv_of(k, v):
        kf = k.astype(F32)
        kk = jnp.concatenate([(kf * tail_f).astype(BF16), (kf * tail_b).astype(BF16)], axis=1)
        return lax.dot_general(kk, v, (((0,), (0,)), ((), ())), preferred_element_type=F32)

    for c in range(N_CHUNK_CTX):
        kv_scr[c] = kv_of(kc[c * CHUNK:(c + 1) * CHUNK, :], vc[c * CHUNK:(c + 1) * CHUNK, :])

    def kv_body(c, carry):
        r = pl.multiple_of(c * CHUNK, CHUNK)
        kv_scr[N_CHUNK_CTX + c] = kv_of(kl[pl.ds(r, CHUNK), :], vl[pl.ds(r, CHUNK), :])
        return carry

    lax.fori_loop(0, N_CHUNK_LAT, kv_body, 0, unroll=RET_UNROLL)

    def fwd_body(c, s):
        st_scr[c, 0:HEAD_DIM, :] = s.astype(BF16)
        return gchunk_f * s + kv_scr[c, 0:HEAD_DIM, :]

    lax.fori_loop(0, N_CHUNK, fwd_body, zeros)

    def bwd_step(c, s):
        st_scr[c, HEAD_DIM:2 * HEAD_DIM, :] = s.astype(BF16)
        return gchunk_b * s + kv_scr[c, HEAD_DIM:2 * HEAD_DIM, :]

    s = zeros
    for c in reversed(range(N_CHUNK_CTX)):
        s = bwd_step(c, s)
    lax.fori_loop(0, N_CHUNK_LAT, lambda t, s: bwd_step(N_CHUNK - 1 - t, s), s)

    def out_of(q, k, v, g, st):
        sc = lax.dot_general(q, k, (((1,), (1,)), ((), ())), preferred_element_type=F32)
        intra = jnp.dot((sc * decay).astype(BF16), v, preferred_element_type=F32)
        qf = q.astype(F32)
        qq = jnp.concatenate([(qf * head_f).astype(BF16), (qf * head_b).astype(BF16)], axis=1)
        o = intra + jnp.dot(qq, st, preferred_element_type=F32)
        mu = jnp.mean(o, axis=-1, keepdims=True)
        var = jnp.mean(jnp.square(o - mu), axis=-1, keepdims=True)
        on = (o - mu) * lax.rsqrt(var + GN_EPS)
        return (on * gain * _silu(g.astype(F32))).astype(BF16)

    for c in range(N_CHUNK_CTX):
        sl = slice(c * CHUNK, (c + 1) * CHUNK)
        oc[sl, :] = out_of(qc[sl, :], kc[sl, :], vc[sl, :], gc[sl, :], st_scr[c])

    def out_body(c, carry):
        r = pl.multiple_of(c * CHUNK, CHUNK)
        sl = pl.ds(r, CHUNK)
        ol[sl, :] = out_of(ql[sl, :], kl[sl, :], vl[sl, :], gl[sl, :], st_scr[N_CHUNK_CTX + c])
        return carry

    lax.fori_loop(0, N_CHUNK_LAT, out_body, 0, unroll=RET_UNROLL)


def _retention(qkvgp, log_gamma, gn_g):
    def lat(part):
        return pl.BlockSpec((SEQ, HEAD_DIM), lambda b, h: (b, part * RET_HEADS + h))

    def ctx(part):
        return pl.BlockSpec((CTX_LEN, HEAD_DIM), lambda b, h: (N_LAT // CTX_LEN + b, part * RET_HEADS + h))

    return pl.pallas_call(
        _ret_kernel,
        out_shape=(jax.ShapeDtypeStruct((N_LAT, RET_WIDTH), BF16),
                   jax.ShapeDtypeStruct((N_CTX, RET_WIDTH), BF16)),
        grid=(BATCH, RET_HEADS),
        in_specs=[pl.BlockSpec(memory_space=pltpu.SMEM),
                  lat(0), lat(1), lat(2), lat(3), ctx(0), ctx(1), ctx(2), ctx(3),
                  pl.BlockSpec((1, HEAD_DIM), lambda b, h: (0, h))],
        out_specs=(pl.BlockSpec((SEQ, HEAD_DIM), lambda b, h: (b, h)),
                   pl.BlockSpec((CTX_LEN, HEAD_DIM), lambda b, h: (b, h))),
        scratch_shapes=[pltpu.VMEM((N_CHUNK, 2 * HEAD_DIM, HEAD_DIM), F32),
                        pltpu.VMEM((N_CHUNK, 2 * HEAD_DIM, HEAD_DIM), BF16)],
        compiler_params=pltpu.CompilerParams(vmem_limit_bytes=VMEM_LIMIT),
        name="retention",
    )(log_gamma, *([qkvgp] * 8), gn_g)


POOL_BLK = 256


def _window_count(idx, w, n):
    return (jnp.minimum(idx - w // 2 + w, n) - jnp.maximum(idx - w // 2, 0)).astype(F32)


def _row_window_mean(s, w):
    size = [min(r - w // 2 + w, GRID_W) - max(r - w // 2, 0) for r in range(GRID_W)]
    parts, r0 = [], 0
    for r in range(1, GRID_W + 1):
        if r == GRID_W or size[r] != size[r0]:
            parts.append(s[r0 * GRID_W:r * GRID_W] * (1.0 / size[r0]))
            r0 = r
    return jnp.concatenate(parts, axis=0)


def _pool_kernel(pl_ref, pc_ref, w_ref, scale_ref, ol, oc):
    ti = lax.broadcasted_iota(jnp.int32, (POOL_BLK, POOL_BLK), 0)
    tj = lax.broadcasted_iota(jnp.int32, (POOL_BLK, POOL_BLK), 1)
    tok_b = lax.broadcasted_iota(jnp.int32, (POOL_BLK, POOL_CH), 0)
    tok_c = lax.broadcasted_iota(jnp.int32, (CTX_LEN, POOL_CH), 0)
    for gi, w in enumerate(POOL_WINDOWS):
        cols = slice(gi * POOL_CH, (gi + 1) * POOL_CH)
        lo_off, hi_off = -(w // 2), w - 1 - w // 2
        wmat = w_ref[gi]
        scale = scale_ref[:, cols]

        ci, cj = ti & (GRID_W - 1), tj & (GRID_W - 1)
        band = ((ti >> GRID_SHIFT == tj >> GRID_SHIFT) & (cj >= ci + lo_off) & (cj <= ci + hi_off))
        band = jnp.where(band, 1.0, 0.0).astype(BF16)
        col_inv = 1.0 / _window_count(tok_b & (GRID_W - 1), w, GRID_W)
        col_mean = [jnp.dot(band, pl_ref[blk * POOL_BLK:(blk + 1) * POOL_BLK, cols],
                            preferred_element_type=F32) * col_inv for blk in range(SEQ // POOL_BLK)]
        pad = jnp.zeros((w // 2 * GRID_W, POOL_CH), F32)
        s = jnp.concatenate([pad] + col_mean + [pad], axis=0)
        span = 1
        while span < w:
            n = s.shape[0] - span * GRID_W
            s = s[:n] + s[span * GRID_W:]
            span *= 2
        box_mean = _row_window_mean(s[:SEQ], w)
        for blk in range(SEQ // POOL_BLK):
            rs = slice(blk * POOL_BLK, (blk + 1) * POOL_BLK)
            d = (box_mean[rs] - pl_ref[rs, cols].astype(F32)).astype(BF16)
            y = jnp.dot(d, wmat, preferred_element_type=F32) * scale
            ol[rs, cols] = y.astype(BF16)

        band_c = jnp.where((tj >= ti + lo_off) & (tj <= ti + hi_off), 1.0, 0.0).astype(BF16)
        mc = jnp.dot(band_c, pc_ref[:, cols], preferred_element_type=F32) / _window_count(tok_c, w, CTX_LEN)
        uc = pc_ref[:, cols].astype(F32)
        dc = (mc - uc).astype(BF16)
        oc[:, cols] = (jnp.dot(dc, wmat, preferred_element_type=F32) * scale).astype(BF16)


def _pool(qkvgp, pool_w, pool_scale):
    pcol = 4 * RET_WIDTH // POOL_WIDTH
    return pl.pallas_call(
        _pool_kernel,
        out_shape=(jax.ShapeDtypeStruct((N_LAT, POOL_WIDTH), BF16),
                   jax.ShapeDtypeStruct((N_CTX, POOL_WIDTH), BF16)),
        grid=(BATCH,),
        in_specs=[pl.BlockSpec((SEQ, POOL_WIDTH), lambda b: (b, pcol)),
                  pl.BlockSpec((CTX_LEN, POOL_WIDTH), lambda b: (N_LAT // CTX_LEN + b, pcol)),
                  pl.BlockSpec((len(POOL_WINDOWS), POOL_CH, POOL_CH), lambda b: (0, 0, 0)),
                  pl.BlockSpec((1, POOL_WIDTH), lambda b: (0, 0))],
        out_specs=(pl.BlockSpec((SEQ, POOL_WIDTH), lambda b: (b, 0)),
                   pl.BlockSpec((CTX_LEN, POOL_WIDTH), lambda b: (b, 0))),
        compiler_params=pltpu.CompilerParams(vmem_limit_bytes=VMEM_LIMIT),
        name="pool",
    )(qkvgp, qkvgp, pool_w, pool_scale)


def _mix_out(i, rl, rc, pl_, pc, w_ref, xl_ref, xc_ref, gt_ref):
    is_lat = i < LAT_TILES
    ret = jnp.where(is_lat, rl[...], rc[...])
    pool = jnp.where(is_lat, pl_[...], pc[...])
    y = (jnp.dot(ret, w_ref[0:RET_WIDTH, :], preferred_element_type=F32)
         + jnp.dot(pool, w_ref[RET_WIDTH:, :], preferred_element_type=F32))
    return _x_tile(xl_ref, xc_ref) + gt_ref[...] * y


def _outproj_router_kernel(rl, rc, pl_, pc, w_ref, xl_ref, xc_ref, gt_ref, g_ref, sh_ref, sc_ref, wr_ref,
                           xo_ref, h_ref, route_ref, cnt_ref, carry_scr):
    i = pl.program_id(0)
    x = _mix_out(i, rl, rc, pl_, pc, w_ref, xl_ref, xc_ref, gt_ref)
    xo_ref[...] = x
    h = _rms_mod(x, g_ref[...], sh_ref[...], sc_ref[...])
    h_ref[...] = h.reshape(TM, *ROW_TILE)
    logits = jnp.dot(h.astype(BF16), wr_ref[...], preferred_element_type=F32)
    lane = lax.broadcasted_iota(jnp.int32, logits.shape, 1)
    neg = jnp.float32(-jnp.inf)
    logits = jnp.where(lane < N_EXPERTS, logits, neg)
    m1 = jnp.max(logits, axis=-1, keepdims=True)
    i1 = jnp.min(jnp.where(logits == m1, lane, LANES), axis=-1, keepdims=True)
    rest = jnp.where(lane == i1, neg, logits)
    m2 = jnp.max(rest, axis=-1, keepdims=True)
    i2 = jnp.min(jnp.where(rest == m2, lane, LANES), axis=-1, keepdims=True)
    e2 = jnp.exp(m2 - m1)
    den = 1.0 + e2

    @pl.when(i == 0)
    def _():
        carry_scr[...] = jnp.zeros_like(carry_scr)

    sel1, sel2 = lane == i1, lane == i2
    picks = jnp.where(sel1 | sel2, 1.0, 0.0)
    ti = lax.broadcasted_iota(jnp.int32, (TM, TM), 0)
    tj = lax.broadcasted_iota(jnp.int32, (TM, TM), 1)
    earlier = jnp.where(tj < ti, 1.0, 0.0).astype(BF16)
    before = jnp.dot(earlier, picks.astype(BF16), preferred_element_type=F32) + carry_scr[...]
    r1 = jnp.sum(jnp.where(sel1, before, 0.0), axis=-1, keepdims=True)
    r2 = jnp.sum(jnp.where(sel2, before, 0.0), axis=-1, keepdims=True)
    carry = carry_scr[...] + jnp.sum(picks, axis=0, keepdims=True)
    carry_scr[...] = carry
    cnt_ref[...] = jnp.broadcast_to(carry, cnt_ref.shape)

    fields = zip(ROUTE_EXPERT + ROUTE_RANK + ROUTE_GATE, (i1.astype(F32), i2.astype(F32), r1, r2, 1.0 / den, e2 / den))
    route = jnp.zeros(logits.shape, F32)
    for k, v in fields:
        route = jnp.where(lane == k, v, route)
    route_ref[...] = route


def _mixer_specs(ctx_off, layer):
    def lat_spec():
        return pl.BlockSpec((TM, RET_WIDTH), lambda i: (jnp.minimum(i, LAT_TILES - 1), 0))

    def ctx_spec():
        return pl.BlockSpec((TM, RET_WIDTH), lambda i: (jnp.maximum(i - LAT_TILES, 0), 0))

    return [lat_spec(), ctx_spec(), lat_spec(), ctx_spec(),
            pl.BlockSpec((D_MODEL, D_MODEL), lambda i: (0, 0), pipeline_mode=pl.Buffered(1)),
            *_x_specs(ctx_off), _mod_spec(layer, 2),
            pl.BlockSpec((1, D_MODEL), lambda i: (0, 0)),
            _mod_spec(layer, 3), _mod_spec(layer, 4)]


def _outproj_router(ret_l, ret_c, pool_l, pool_c, w_out, x_lat, x_ctx, ctx_off, mod5, norm_g, router_w, layer,
                    n_tiles):
    row = pl.BlockSpec((TM, D_MODEL), lambda i: (i, 0))
    n_rows = n_tiles * TM
    return pl.pallas_call(
        _outproj_router_kernel,
        out_shape=(jax.ShapeDtypeStruct((n_rows, D_MODEL), F32), jax.ShapeDtypeStruct((n_rows, *ROW_TILE), F32),
                   jax.ShapeDtypeStruct((n_rows, LANES), F32), jax.ShapeDtypeStruct((SUBLANES, LANES), F32)),
        grid=(n_tiles,),
        in_specs=_mixer_specs(ctx_off, layer) + [pl.BlockSpec((D_MODEL, LANES), lambda i: (0, 0))],
        out_specs=(row, pl.BlockSpec((TM, *ROW_TILE), lambda i: (i, 0, 0)),
                   pl.BlockSpec((TM, LANES), lambda i: (i, 0)), pl.BlockSpec((SUBLANES, LANES), lambda i: (0, 0))),
        scratch_shapes=[pltpu.VMEM((1, LANES), F32)],
        compiler_params=pltpu.CompilerParams(dimension_semantics=("arbitrary",), vmem_limit_bytes=VMEM_LIMIT),
        name="outproj",
    )(ret_l, ret_c, pool_l, pool_c, w_out, x_lat, x_ctx, mod5, norm_g, mod5, mod5, router_w)


def _swiglu(h, w13_ref, w2_ref, act_scr):
    for j in range(D_FF // FF_CHUNK):
        u = jnp.dot(h, w13_ref[0, :, j * FF_CHUNK:(j + 1) * FF_CHUNK], preferred_element_type=F32)
        g = jnp.dot(h, w13_ref[0, :, D_FF + j * FF_CHUNK:D_FF + (j + 1) * FF_CHUNK], preferred_element_type=F32)
        act_scr[:, j * FF_CHUNK:(j + 1) * FF_CHUNK] = (_silu(g) * u).astype(BF16)
    return jnp.dot(act_scr[...], w2_ref[0], preferred_element_type=F32)


def _outproj_ffn_kernel(rl, rc, pl_, pc, w_ref, xl_ref, xc_ref, gt1_ref, g_ref, sh_ref, sc_ref, gt2_ref,
                        w13_ref, w2_ref, wf_ref, o_ref, wb_ref, act_scr):
    x = _mix_out(pl.program_id(0), rl, rc, pl_, pc, w_ref, xl_ref, xc_ref, gt1_ref)
    h = _rms_mod(x, g_ref[...], sh_ref[...], sc_ref[...]).astype(BF16)
    o_ref[...] = x + gt2_ref[...] * _swiglu(h, w13_ref, w2_ref, act_scr)
    wb_ref[...] = wf_ref[...].astype(BF16)


def _outproj_ffn(ret_l, ret_c, pool_l, pool_c, w_out, x_lat, x_ctx, ctx_off, mod5, norm_g, w13, w2, cast, layer,
                 n_tiles):
    resident = pl.Buffered(1)
    stacked, index = cast
    rows, cols = stacked.shape[0] // (DEPTH // 2), stacked.shape[1]
    src_spec, dst_spec = _cast_specs(rows, cols, index)
    return pl.pallas_call(
        _outproj_ffn_kernel,
        out_shape=(jax.ShapeDtypeStruct((n_tiles * TM, D_MODEL), F32), jax.ShapeDtypeStruct((rows, cols), BF16)),
        grid=(n_tiles,),
        in_specs=_mixer_specs(ctx_off, layer) + [
            _mod_spec(layer, 5),
            pl.BlockSpec((1, D_MODEL, 2 * D_FF), lambda i: (0, 0, 0), pipeline_mode=resident),
            pl.BlockSpec((1, D_FF, D_MODEL), lambda i: (0, 0, 0), pipeline_mode=resident),
            src_spec],
        out_specs=(pl.BlockSpec((TM, D_MODEL), lambda i: (i, 0)), dst_spec),
        scratch_shapes=[pltpu.VMEM((TM, D_FF), BF16)],
        compiler_params=pltpu.CompilerParams(vmem_limit_bytes=VMEM_LIMIT),
        name="outproj_ffn",
    )(ret_l, ret_c, pool_l, pool_c, w_out, x_lat, x_ctx, mod5, norm_g, mod5, mod5, mod5, w13, w2, stacked)


def _moe_plan(counts, n_steps):
    cnt = counts.astype(jnp.int32)
    end = jnp.cumsum(cnt)
    start = end - cnt
    first = start // TM
    visits = jnp.where(cnt > 0, (end - 1) // TM - first + 1, 0)
    visit_end = jnp.cumsum(visits)
    visit_start = visit_end - visits
    total = visit_end[-1]
    step = jnp.minimum(jnp.arange(n_steps, dtype=jnp.int32), total - 1)
    eid = jnp.minimum(jnp.sum(step[:, None] >= visit_end[None, :], axis=1), N_EXPERTS - 1).astype(jnp.int32)
    tile = first[eid] + step - visit_start[eid]
    lo = jnp.clip(start[eid] - tile * TM, 0, TM)
    hi = jnp.clip(end[eid] - tile * TM, 0, TM)
    hi = jnp.where(jnp.arange(n_steps) < total, hi, lo)
    return start, tile.astype(jnp.int32), eid, lo.astype(jnp.int32), hi.astype(jnp.int32)


def _row_copy_wait(src, dst, sem):
    pltpu.make_async_copy(src, dst, sem).wait()


def _dispatch_kernel(pos_ref, h_ref, xs_hbm, sem):
    i = pl.program_id(0)
    n = 0
    for r in range(TM):
        for k in range(2):
            p = pos_ref[i, k * TM + r]
            pltpu.make_async_copy(h_ref.at[r], xs_hbm.at[p], sem).start(priority=n % 2)
            n += 1
    for k in range(2):
        _row_copy_wait(h_ref, xs_hbm.at[pl.ds(0, TM)], sem)


def _dispatch(pos, h, n_tiles):
    return pl.pallas_call(
        _dispatch_kernel,
        out_shape=jax.ShapeDtypeStruct((2 * n_tiles * TM, *ROW_TILE), F32),
        grid_spec=pltpu.PrefetchScalarGridSpec(
            num_scalar_prefetch=1,
            grid=(n_tiles,),
            in_specs=[pl.BlockSpec((TM, *ROW_TILE), lambda i, pos: (i, 0, 0))],
            out_specs=pl.BlockSpec(memory_space=pl.ANY),
            scratch_shapes=[pltpu.SemaphoreType.DMA]),
        name="dispatch",
    )(pos, h)


def _expert_ffn_kernel(tile_ref, eid_ref, lo_ref, hi_ref, x_ref, w13_ref, w2_ref, o_ref, act_scr):
    s = pl.program_id(0)
    lo, hi = lo_ref[s], hi_ref[s]

    @pl.when(hi > lo)
    def _():
        row = lax.broadcasted_iota(jnp.int32, (TM, D_MODEL), 0)
        x = x_ref[...].reshape(TM, D_MODEL)
        h = jnp.where((row >= lo) & (row < hi), x, 0.0).astype(BF16)
        y = _swiglu(h, w13_ref, w2_ref, act_scr).reshape(TM, *ROW_TILE)

        @pl.when(lo == 0)
        def _():
            o_ref[...] = y

        @pl.when(lo > 0)
        def _():
            o_ref[...] += y


def _expert_ffn(plan, xs, w13, w2, n_steps):
    _, tile, eid, lo, hi = plan
    row = pl.BlockSpec((TM, *ROW_TILE), lambda s, tile, eid, lo, hi: (tile[s], 0, 0))
    return pl.pallas_call(
        _expert_ffn_kernel,
        out_shape=jax.ShapeDtypeStruct(xs.shape, F32),
        grid_spec=pltpu.PrefetchScalarGridSpec(
            num_scalar_prefetch=4,
            grid=(n_steps,),
            in_specs=[row,
                      pl.BlockSpec((1, D_MODEL, 2 * D_FF), lambda s, tile, eid, lo, hi: (eid[s], 0, 0)),
                      pl.BlockSpec((1, D_FF, D_MODEL), lambda s, tile, eid, lo, hi: (eid[s], 0, 0))],
            out_specs=row,
            scratch_shapes=[pltpu.VMEM((TM, D_FF), BF16)]),
        compiler_params=pltpu.CompilerParams(vmem_limit_bytes=VMEM_LIMIT),
        name="expert_ffn",
    )(tile, eid, lo, hi, xs, w13, w2)


def _combined_tile(pos_ref, y_hbm, route_ref, x_ref, gt_ref, ybuf, sem, n_tiles):
    i = pl.program_id(0)
    slot = i % 2

    def request(tile, into):
        n = 0
        for r in range(TM):
            for k in range(2):
                p = pos_ref[tile, k * TM + r]
                pltpu.make_async_copy(y_hbm.at[p], ybuf.at[into, k, r], sem.at[into]).start(priority=n % 2)
                n += 1

    @pl.when(i == 0)
    def _():
        request(0, 0)

    @pl.when(i + 1 < n_tiles)
    def _():
        request(i + 1, 1 - slot)

    for k in range(2):
        _row_copy_wait(y_hbm.at[pl.ds(0, TM)], ybuf.at[slot, k], sem.at[slot])
    w1 = route_ref[:, ROUTE_GATE[0]:ROUTE_GATE[0] + 1]
    w2 = route_ref[:, ROUTE_GATE[1]:ROUTE_GATE[1] + 1]
    y1 = ybuf[slot, 0].reshape(TM, D_MODEL)
    y2 = ybuf[slot, 1].reshape(TM, D_MODEL)
    return x_ref[...] + gt_ref[...] * (w1 * y1 + w2 * y2)


def _combine_norm_kernel(pos_ref, y_hbm, route_ref, x_ref, gt_ref, g_ref, o_ref, ybuf, sem, *, n_tiles):
    x = _combined_tile(pos_ref, y_hbm, route_ref, x_ref, gt_ref, ybuf, sem, n_tiles)
    ms = jnp.mean(x * x, axis=-1, keepdims=True)
    o_ref[...] = x * lax.rsqrt(ms + NORM_EPS) * g_ref[...]


def _combine_inproj_kernel(pos_ref, y_hbm, route_ref, x_ref, gt_ref, g_ref, sh_ref, sc_ref, w_ref,
                           cos_ref, sa_ref, sb_ref, wf_ref, o_ref, xo_ref, wb_ref, ybuf, sem, *, n_tiles):
    x = _combined_tile(pos_ref, y_hbm, route_ref, x_ref, gt_ref, ybuf, sem, n_tiles)
    xo_ref[...] = x
    h = _rms_mod(x, g_ref[...], sh_ref[...], sc_ref[...]).astype(BF16)
    _project_in(h, w_ref, cos_ref, sa_ref, sb_ref, o_ref)
    wb_ref[...] = wf_ref[...].astype(BF16)


def _combine_specs(layer):
    return [pl.BlockSpec(memory_space=pl.ANY),
            pl.BlockSpec((TM, LANES), lambda i, pos: (i, 0)),
            pl.BlockSpec((TM, D_MODEL), lambda i, pos: (i, 0)), _mod_spec(layer, 5)]


COMBINE_SCRATCH = [pltpu.VMEM((2, 2, TM, *ROW_TILE), F32), pltpu.SemaphoreType.DMA((2,))]


def _combine_norm(pos, y, route, x_all, mod5, final_g, layer, n_tiles):
    row = pl.BlockSpec((TM, D_MODEL), lambda i, pos: (i, 0))
    return pl.pallas_call(
        functools.partial(_combine_norm_kernel, n_tiles=n_tiles),
        out_shape=jax.ShapeDtypeStruct((n_tiles * TM, D_MODEL), F32),
        grid_spec=pltpu.PrefetchScalarGridSpec(
            num_scalar_prefetch=1,
            grid=(n_tiles,),
            in_specs=_combine_specs(layer) + [pl.BlockSpec((1, D_MODEL), lambda i, pos: (0, 0))],
            out_specs=row,
            scratch_shapes=COMBINE_SCRATCH),
        compiler_params=pltpu.CompilerParams(dimension_semantics=("arbitrary",), vmem_limit_bytes=VMEM_LIMIT),
        name="combine",
    )(pos, y, route, x_all, mod5, final_g)


def _combine_inproj(pos, y, route, x_all, mod5, norm_g, w_in, tables, cast, layer, n_tiles):
    row = pl.BlockSpec((TM, D_MODEL), lambda i, pos: (i, 0))
    tab_spec = pl.BlockSpec((TM, HEAD_DIM),
                            lambda i, pos: (jnp.where(i < LAT_TILES, i % TILES_PER_BATCH, TILES_PER_BATCH), 0))
    stacked, index = cast
    rows, cols = stacked.shape[0] // (DEPTH // 2), stacked.shape[1]
    src_spec, dst_spec = _cast_specs(rows, cols, index)
    return pl.pallas_call(
        functools.partial(_combine_inproj_kernel, n_tiles=n_tiles),
        out_shape=(jax.ShapeDtypeStruct((n_tiles * TM, IN_COLS), BF16),
                   jax.ShapeDtypeStruct((n_tiles * TM, D_MODEL), F32),
                   jax.ShapeDtypeStruct((rows, cols), BF16)),
        grid_spec=pltpu.PrefetchScalarGridSpec(
            num_scalar_prefetch=1,
            grid=(n_tiles,),
            in_specs=_combine_specs(layer - 1) + [
                pl.BlockSpec((1, D_MODEL), lambda i, pos: (0, 0)),
                _mod_spec(layer, 0), _mod_spec(layer, 1),
                pl.BlockSpec((D_MODEL, IN_COLS), lambda i, pos: (0, 0)),
                tab_spec, tab_spec, tab_spec, src_spec],
            out_specs=(pl.BlockSpec((TM, IN_COLS), lambda i, pos: (i, 0)), row, dst_spec),
            scratch_shapes=COMBINE_SCRATCH),
        compiler_params=pltpu.CompilerParams(dimension_semantics=("arbitrary",), vmem_limit_bytes=VMEM_LIMIT),
        name="combine_inproj",
    )(pos, y, route, x_all, mod5, norm_g, mod5, mod5, w_in, *tables, stacked)


def _moe_experts(h, route, counts, w13, w2, n_tiles):
    n_steps = 2 * n_tiles + N_EXPERTS
    plan = _moe_plan(counts[0, :N_EXPERTS], n_steps)
    expert = route[:, ROUTE_EXPERT[0]:ROUTE_EXPERT[1] + 1].astype(jnp.int32)
    rank = route[:, ROUTE_RANK[0]:ROUTE_RANK[1] + 1].astype(jnp.int32)
    pos = plan[0][expert] + rank
    pos = pos.reshape(n_tiles, TM, 2).transpose(0, 2, 1).reshape(n_tiles, 2 * TM)
    xs = _dispatch(pos, h, n_tiles)
    return pos, _expert_ffn(plan, xs, w13, w2, n_steps)


def kernel(x, c, ctx, c_ctx, w_ada, b_ada, norm1_g, norm2_g, w_in, ret_decay_logit, ret_gn_g, pool_w, pool_scale,
           w_out, ffn_w13, ffn_w2, router_w, moe_w13, moe_w2, final_norm_g):
    assert DEPTH % 2 == 0, "the final norm is fused into the last routed-expert combine"
    c_all = jnp.concatenate([c, c_ctx[None, :], jnp.zeros((MOD_ROWS - BATCH - 1, D_MODEL), F32)], axis=0)
    mod5 = _ada_mod(c_all, w_ada, b_ada).reshape(DEPTH, MOD_ROWS, 6, 1, D_MODEL)
    tables = _rope_tables()
    log_gamma = jax.nn.log_sigmoid(ret_decay_logit.astype(F32))
    router_pad = jnp.pad(router_w, ((0, 0), (0, 0), (0, LANES - N_EXPERTS))).astype(BF16)
    moe_w13_rows = moe_w13.reshape(-1, 2 * D_FF)
    moe_w2_rows = moe_w2.reshape(-1, D_MODEL)
    final_g = final_norm_g[None, :]

    x_lat, x_ctx, ctx_off = x.reshape(N_LAT, D_MODEL), ctx.reshape(N_CTX, D_MODEL), 0
    routed = None
    for l in range(DEPTH):
        last = l == DEPTH - 1
        n_tiles = LAT_TILES if last else LAT_TILES + CTX_TILES
        i = l // 2
        if routed is not None:
            qkvgp, x_all, w2_b = _combine_inproj(*routed, x_all, mod5, norm1_g[l][None, :], w_in[l].astype(BF16),
                                                 tables, (moe_w2_rows, i), l, LAT_TILES + CTX_TILES)
            x_lat, x_ctx, routed = x_all, x_all, None
        elif l % 2 == 0:
            qkvgp, w2_b = _inproj(x_lat, x_ctx, ctx_off, norm1_g[l][None, :], mod5, w_in[l].astype(BF16), tables, l,
                                  LAT_TILES + CTX_TILES, cast=(moe_w2_rows, i))
        else:
            qkvgp = _inproj(x_lat, x_ctx, ctx_off, norm1_g[l][None, :], mod5, w_in[l].astype(BF16), tables, l,
                            LAT_TILES + CTX_TILES)
        ret_l, ret_c = _retention(qkvgp, log_gamma[l], ret_gn_g[l][None, :])
        pool_l, pool_c = _pool(qkvgp, pool_w[l].astype(BF16), pool_scale[l][None, :])
        if l % 2 == 0:
            x_all, w13_b = _outproj_ffn(ret_l, ret_c, pool_l, pool_c, w_out[l].astype(BF16), x_lat, x_ctx, ctx_off,
                                        mod5, norm2_g[l][None, :], ffn_w13[i][None].astype(BF16),
                                        ffn_w2[i][None].astype(BF16), (moe_w13_rows, i), l, n_tiles)
        else:
            x_all, h2, route, counts = _outproj_router(ret_l, ret_c, pool_l, pool_c, w_out[l].astype(BF16), x_lat,
                                                       x_ctx, ctx_off, mod5, norm2_g[l][None, :], router_pad[i],
                                                       l, n_tiles)
            pos, ys = _moe_experts(h2, route, counts, w13_b.reshape(N_EXPERTS, D_MODEL, 2 * D_FF),
                                   w2_b.reshape(N_EXPERTS, D_FF, D_MODEL), n_tiles)
            if last:
                x_all = _combine_norm(pos, ys, route, x_all, mod5, final_g, l, n_tiles)
            else:
                routed = (pos, ys, route)
        x_lat, x_ctx, ctx_off = x_all, x_all, LAT_TILES
    return x_all.reshape(BATCH, SEQ, D_MODEL)
```

```python
import functools

import jax
import jax.numpy as jnp
import numpy as np
from jax import lax
from jax.experimental import pallas as pl
from jax.experimental.pallas import tpu as pltpu

F32 = jnp.float32
BF16 = jnp.bfloat16

D_MODEL = 1024
BATCH = 8
SEQ = 4096
DEPTH = 4
GRID_W = 64
GRID_SHIFT = 6
CTX_LEN = 256
RET_WIDTH = 512
POOL_WIDTH = 512
RET_HEADS = 4
HEAD_DIM = 128
CHUNK = 128
ROPE_BASE = 10000.0
POOL_WINDOWS = (2, 4, 8, 16)
POOL_CH = 128
IN_COLS = 4 * RET_WIDTH + POOL_WIDTH
D_FF = 2816
N_EXPERTS = 8
NORM_EPS = 1e-6
GN_EPS = 1e-5

N_LAT = BATCH * SEQ
N_CTX = BATCH * CTX_LEN
N_ROWS = N_LAT + N_CTX
MOD_ROWS = 16
CTX_MOD_ROW = BATCH

TM = 512
LAT_TILES = N_LAT // TM
CTX_TILES = N_CTX // TM
TILES_PER_BATCH = SEQ // TM
FF_CHUNK = 256
LANES = 128
SUBLANES = 8
ROW_TILE = (D_MODEL // LANES, LANES)
ROUTE_EXPERT, ROUTE_RANK, ROUTE_GATE = (0, 1), (2, 3), (4, 5)
VMEM_LIMIT = 56 * 1024 * 1024


def _mod_row(i):
    return jnp.where(i < LAT_TILES, i // TILES_PER_BATCH, CTX_MOD_ROW)


def _mod_spec(layer, part):
    return pl.BlockSpec((None, None, None, 1, D_MODEL),
                        lambda i, *_: (layer, _mod_row(i), part, 0, 0))


def _silu(v):
    return v * jax.nn.sigmoid(v)


def _x_specs(ctx_block_offset):
    return [pl.BlockSpec((TM, D_MODEL), lambda i, *_: (jnp.minimum(i, LAT_TILES - 1), 0)),
            pl.BlockSpec((TM, D_MODEL), lambda i, *_: (ctx_block_offset + jnp.maximum(i - LAT_TILES, 0), 0))]


def _x_tile(xl_ref, xc_ref):
    return jnp.where(pl.program_id(0) < LAT_TILES, xl_ref[...], xc_ref[...])


def _cast_specs(rows, cols, layer_index):
    blk = rows // LAT_TILES

    def step(i):
        return jnp.minimum(i, LAT_TILES - 1)

    return (pl.BlockSpec((blk, cols), lambda i: (layer_index * LAT_TILES + step(i), 0)),
            pl.BlockSpec((blk, cols), lambda i: (step(i), 0)))


def _rms_mod(x, gain, shift, scale):
    ms = jnp.mean(x * x, axis=-1, keepdims=True)
    y = x * lax.rsqrt(ms + NORM_EPS) * gain
    return y * (1.0 + scale) + shift


def _ada_kernel(c_ref, w_ref, b_ref, o_ref):
    s = _silu(c_ref[...])
    o_ref[0] = jnp.dot(s.astype(BF16), w_ref[0].astype(BF16), preferred_element_type=F32) + b_ref[0]


def _ada_mod(c_all, w_ada, b_ada):
    tn = 1536
    return pl.pallas_call(
        _ada_kernel,
        out_shape=jax.ShapeDtypeStruct((DEPTH, MOD_ROWS, 6 * D_MODEL), F32),
        grid=(DEPTH, 6 * D_MODEL // tn),
        in_specs=[pl.BlockSpec((MOD_ROWS, D_MODEL), lambda l, n: (0, 0)),
                  pl.BlockSpec((1, D_MODEL, tn), lambda l, n: (l, 0, n)),
                  pl.BlockSpec((1, 1, tn), lambda l, n: (l, 0, n))],
        out_specs=pl.BlockSpec((1, MOD_ROWS, tn), lambda l, n: (l, 0, n)),
        compiler_params=pltpu.CompilerParams(vmem_limit_bytes=VMEM_LIMIT),
        name="ada_mod",
    )(c_all, w_ada, b_ada.reshape(DEPTH, 1, 6 * D_MODEL))


def _inproj_kernel(xl_ref, xc_ref, g_ref, sh_ref, sc_ref, w_ref, cos_ref, sa_ref, sb_ref, o_ref):
    h = _rms_mod(_x_tile(xl_ref, xc_ref), g_ref[...], sh_ref[...], sc_ref[...]).astype(BF16)
    _project_in(h, w_ref, cos_ref, sa_ref, sb_ref, o_ref)


def _inproj_cast_kernel(xl_ref, xc_ref, g_ref, sh_ref, sc_ref, w_ref, cos_ref, sa_ref, sb_ref, wf_ref, o_ref, wb_ref):
    _inproj_kernel(xl_ref, xc_ref, g_ref, sh_ref, sc_ref, w_ref, cos_ref, sa_ref, sb_ref, o_ref)
    wb_ref[...] = wf_ref[...].astype(BF16)


def _project_in(h, w_ref, cos_ref, sa_ref, sb_ref, o_ref):
    cos, sa, sb = cos_ref[...], sa_ref[...], sb_ref[...]
    k_scale = HEAD_DIM ** -0.5
    for part, mul in ((0, 1.0), (1, k_scale)):
        z = jnp.dot(h, w_ref[:, part * RET_WIDTH:(part + 1) * RET_WIDTH], preferred_element_type=F32)
        for hh in range(RET_HEADS):
            t = z[:, hh * HEAD_DIM:(hh + 1) * HEAD_DIM]
            r = t * cos + pltpu.roll(t, 96, axis=1) * sa + pltpu.roll(t, 32, axis=1) * sb
            if mul != 1.0:
                r = r * mul
            col = part * RET_WIDTH + hh * HEAD_DIM
            o_ref[:, col:col + HEAD_DIM] = r.astype(BF16)
    z = jnp.dot(h, w_ref[:, 2 * RET_WIDTH:], preferred_element_type=F32)
    o_ref[:, 2 * RET_WIDTH:] = z.astype(BF16)


def _rope_tables():
    half = HEAD_DIM // 2
    inv_freq = ROPE_BASE ** (-np.arange(0, half, 2, dtype=np.float64) / half)
    t = np.arange(SEQ)
    rows, cols = (t // GRID_W).astype(np.float64), (t % GRID_W).astype(np.float64)
    ang_r = rows[:, None] * inv_freq[None, :]
    ang_c = cols[:, None] * inv_freq[None, :]
    zero = np.zeros_like(ang_r)
    cos = np.concatenate([np.cos(ang_r), np.cos(ang_r), np.cos(ang_c), np.cos(ang_c)], axis=1)
    sa = np.concatenate([-np.sin(ang_r), zero, -np.sin(ang_c), zero], axis=1)
    sb = np.concatenate([zero, np.sin(ang_r), zero, np.sin(ang_c)], axis=1)
    ident = np.ones((TM, HEAD_DIM))
    none = np.zeros((TM, HEAD_DIM))
    return tuple(jnp.asarray(np.concatenate(parts), F32) for parts in ((cos, ident), (sa, none), (sb, none)))


def _inproj(x_lat, x_ctx, ctx_off, norm_g, mod5, w_in, tables, layer, n_tiles, cast=None):
    tab_spec = pl.BlockSpec((TM, HEAD_DIM),
                            lambda i: (jnp.where(i < LAT_TILES, i % TILES_PER_BATCH, TILES_PER_BATCH), 0))
    body = _inproj_kernel
    in_specs = _x_specs(ctx_off) + [
        pl.BlockSpec((1, D_MODEL), lambda i: (0, 0)),
        _mod_spec(layer, 0), _mod_spec(layer, 1),
        pl.BlockSpec((D_MODEL, IN_COLS), lambda i: (0, 0)),
        tab_spec, tab_spec, tab_spec]
    args = [x_lat, x_ctx, norm_g, mod5, mod5, w_in, *tables]
    out_shape = [jax.ShapeDtypeStruct((N_ROWS, IN_COLS), BF16)]
    out_specs = [pl.BlockSpec((TM, IN_COLS), lambda i: (i, 0))]
    if cast is not None:
        stacked, index = cast
        rows, cols = stacked.shape[0] // (DEPTH // 2), stacked.shape[1]
        src_spec, dst_spec = _cast_specs(rows, cols, index)
        body = _inproj_cast_kernel
        in_specs.append(src_spec)
        args.append(stacked)
        out_shape.append(jax.ShapeDtypeStruct((rows, cols), BF16))
        out_specs.append(dst_spec)
    out = pl.pallas_call(
        body, out_shape=tuple(out_shape), grid=(n_tiles,), in_specs=in_specs, out_specs=tuple(out_specs),
        compiler_params=pltpu.CompilerParams(vmem_limit_bytes=VMEM_LIMIT),
        name="inproj",
    )(*args)
    return out if cast is not None else out[0]


N_CHUNK_LAT = SEQ // CHUNK
N_CHUNK_CTX = CTX_LEN // CHUNK
N_CHUNK = N_CHUNK_LAT + N_CHUNK_CTX
RET_UNROLL = 32


def _ret_kernel(lg_ref, ql, kl, vl, gl, qc, kc, vc, gc, gain_ref, ol, oc, kv_scr, st_scr):
    hd = pl.program_id(1)
    lgf = lg_ref[0, hd]
    lgb = lg_ref[1, hd]
    pi = lax.broadcasted_iota(jnp.int32, (CHUNK, CHUNK), 0).astype(F32)
    pj = lax.broadcasted_iota(jnp.int32, (CHUNK, CHUNK), 1).astype(F32)
    tail_f = jnp.exp(lgf * (CHUNK - 1.0 - pi))
    tail_b = jnp.exp(lgb * pi)
    head_f = jnp.exp(lgf * (pi + 1.0))
    head_b = jnp.exp(lgb * (CHUNK - pi))
    dif = pi - pj
    decay = jnp.where(dif >= 0, jnp.exp(lgf * jnp.maximum(dif, 0.0)), jnp.exp(lgb * jnp.maximum(-dif, 0.0)))
    zeros = jnp.zeros((CHUNK, CHUNK), F32)
    gchunk_f = jnp.exp(zeros + lgf * CHUNK)
    gchunk_b = jnp.exp(zeros + lgb * CHUNK)
    gain = gain_ref[...]

    def kv_of(k, v):
        kf = k.astype(F32)
        kk = jnp.concatenate([(kf * tail_f).astype(BF16), (kf * tail_b).astype(BF16)], axis=1)
        return lax.dot_general(kk, v, (((0,), (0,)), ((), ())), preferred_element_type=F32)

    for c in range(N_CHUNK_CTX):
        kv_scr[c] = kv_of(kc[c * CHUNK:(c + 1) * CHUNK, :], vc[c * CHUNK:(c + 1) * CHUNK, :])

    def kv_body(c, carry):
        r = pl.multiple_of(c * CHUNK, CHUNK)
        kv_scr[N_CHUNK_CTX + c] = kv_of(kl[pl.ds(r, CHUNK), :], vl[pl.ds(r, CHUNK), :])
        return carry

    lax.fori_loop(0, N_CHUNK_LAT, kv_body, 0, unroll=RET_UNROLL)

    def fwd_body(c, s):
        st_scr[c, 0:HEAD_DIM, :] = s.astype(BF16)
        return gchunk_f * s + kv_scr[c, 0:HEAD_DIM, :]

    lax.fori_loop(0, N_CHUNK, fwd_body, zeros)

    def bwd_step(c, s):
        st_scr[c, HEAD_DIM:2 * HEAD_DIM, :] = s.astype(BF16)
        return gchunk_b * s + kv_scr[c, HEAD_DIM:2 * HEAD_DIM, :]

    s = zeros
    for c in reversed(range(N_CHUNK_CTX)):
        s = bwd_step(c, s)
    lax.fori_loop(0, N_CHUNK_LAT, lambda t, s: bwd_step(N_CHUNK - 1 - t, s), s)

    def out_of(q, k, v, g, st):
        sc = lax.dot_general(q, k, (((1,), (1,)), ((), ())), preferred_element_type=F32)
        intra = jnp.dot((sc * decay).astype(BF16), v, preferred_element_type=F32)
        qf = q.astype(F32)
        qq = jnp.concatenate([(qf * head_f).astype(BF16), (qf * head_b).astype(BF16)], axis=1)
        o = intra + jnp.dot(qq, st, preferred_element_type=F32)
        mu = jnp.mean(o, axis=-1, keepdims=True)
        var = jnp.mean(jnp.square(o - mu), axis=-1, keepdims=True)
        on = (o - mu) * lax.rsqrt(var + GN_EPS)
        return (on * gain * _silu(g.astype(F32))).astype(BF16)

    for c in range(N_CHUNK_CTX):
        sl = slice(c * CHUNK, (c + 1) * CHUNK)
        oc[sl, :] = out_of(qc[sl, :], kc[sl, :], vc[sl, :], gc[sl, :], st_scr[c])

    def out_body(c, carry):
        r = pl.multiple_of(c * CHUNK, CHUNK)
        sl = pl.ds(r, CHUNK)
        ol[sl, :] = out_of(ql[sl, :], kl[sl, :], vl[sl, :], gl[sl, :], st_scr[N_CHUNK_CTX + c])
        return carry

    lax.fori_loop(0, N_CHUNK_LAT, out_body, 0, unroll=RET_UNROLL)


def _retention(qkvgp, log_gamma, gn_g):
    def lat(part):
        return pl.BlockSpec((SEQ, HEAD_DIM), lambda b, h: (b, part * RET_HEADS + h))

    def ctx(part):
        return pl.BlockSpec((CTX_LEN, HEAD_DIM), lambda b, h: (N_LAT // CTX_LEN + b, part * RET_HEADS + h))

    return pl.pallas_call(
        _ret_kernel,
        out_shape=(jax.ShapeDtypeStruct((N_LAT, RET_WIDTH), BF16),
                   jax.ShapeDtypeStruct((N_CTX, RET_WIDTH), BF16)),
        grid=(BATCH, RET_HEADS),
        in_specs=[pl.BlockSpec(memory_space=pltpu.SMEM),
                  lat(0), lat(1), lat(2), lat(3), ctx(0), ctx(1), ctx(2), ctx(3),
                  pl.BlockSpec((1, HEAD_DIM), lambda b, h: (0, h))],
        out_specs=(pl.BlockSpec((SEQ, HEAD_DIM), lambda b, h: (b, h)),
                   pl.BlockSpec((CTX_LEN, HEAD_DIM), lambda b, h: (b, h))),
        scratch_shapes=[pltpu.VMEM((N_CHUNK, 2 * HEAD_DIM, HEAD_DIM), F32),
                        pltpu.VMEM((N_CHUNK, 2 * HEAD_DIM, HEAD_DIM), BF16)],
        compiler_params=pltpu.CompilerParams(vmem_limit_bytes=VMEM_LIMIT),
        name="retention",
    )(log_gamma, *([qkvgp] * 8), gn_g)


POOL_BLK = 256


def _window_count(idx, w, n):
    return (jnp.minimum(idx - w // 2 + w, n) - jnp.maximum(idx - w // 2, 0)).astype(F32)


def _row_window_mean(s, w):
    size = [min(r - w // 2 + w, GRID_W) - max(r - w // 2, 0) for r in range(GRID_W)]
    parts, r0 = [], 0
    for r in range(1, GRID_W + 1):
        if r == GRID_W or size[r] != size[r0]:
            parts.append(s[r0 * GRID_W:r * GRID_W] * (1.0 / size[r0]))
            r0 = r
    return jnp.concatenate(parts, axis=0)


def _pool_kernel(pl_ref, pc_ref, w_ref, scale_ref, ol, oc):
    ti = lax.broadcasted_iota(jnp.int32, (POOL_BLK, POOL_BLK), 0)
    tj = lax.broadcasted_iota(jnp.int32, (POOL_BLK, POOL_BLK), 1)
    tok_b = lax.broadcasted_iota(jnp.int32, (POOL_BLK, POOL_CH), 0)
    tok_c = lax.broadcasted_iota(jnp.int32, (CTX_LEN, POOL_CH), 0)
    for gi, w in enumerate(POOL_WINDOWS):
        cols = slice(gi * POOL_CH, (gi + 1) * POOL_CH)
        lo_off, hi_off = -(w // 2), w - 1 - w // 2
        wmat = w_ref[gi]
        scale = scale_ref[:, cols]

        ci, cj = ti & (GRID_W - 1), tj & (GRID_W - 1)
        band = ((ti >> GRID_SHIFT == tj >> GRID_SHIFT) & (cj >= ci + lo_off) & (cj <= ci + hi_off))
        band = jnp.where(band, 1.0, 0.0).astype(BF16)
        col_inv = 1.0 / _window_count(tok_b & (GRID_W - 1), w, GRID_W)
        col_mean = [jnp.dot(band, pl_ref[blk * POOL_BLK:(blk + 1) * POOL_BLK, cols],
                            preferred_element_type=F32) * col_inv for blk in range(SEQ // POOL_BLK)]
        pad = jnp.zeros((w // 2 * GRID_W, POOL_CH), F32)
        s = jnp.concatenate([pad] + col_mean + [pad], axis=0)
        span = 1
        while span < w:
            n = s.shape[0] - span * GRID_W
            s = s[:n] + s[span * GRID_W:]
            span *= 2
        box_mean = _row_window_mean(s[:SEQ], w)
        for blk in range(SEQ // POOL_BLK):
            rs = slice(blk * POOL_BLK, (blk + 1) * POOL_BLK)
            d = (box_mean[rs] - pl_ref[rs, cols].astype(F32)).astype(BF16)
            y = jnp.dot(d, wmat, preferred_element_type=F32) * scale
            ol[rs, cols] = y.astype(BF16)

        band_c = jnp.where((tj >= ti + lo_off) & (tj <= ti + hi_off), 1.0, 0.0).astype(BF16)
        mc = jnp.dot(band_c, pc_ref[:, cols], preferred_element_type=F32) / _window_count(tok_c, w, CTX_LEN)
        uc = pc_ref[:, cols].astype(F32)
        dc = (mc - uc).astype(BF16)
        oc[:, cols] = (jnp.dot(dc, wmat, preferred_element_type=F32) * scale).astype(BF16)


def _pool(qkvgp, pool_w, pool_scale):
    pcol = 4 * RET_WIDTH // POOL_WIDTH
    return pl.pallas_call(
        _pool_kernel,
        out_shape=(jax.ShapeDtypeStruct((N_LAT, POOL_WIDTH), BF16),
                   jax.ShapeDtypeStruct((N_CTX, POOL_WIDTH), BF16)),
        grid=(BATCH,),
        in_specs=[pl.BlockSpec((SEQ, POOL_WIDTH), lambda b: (b, pcol)),
                  pl.BlockSpec((CTX_LEN, POOL_WIDTH), lambda b: (N_LAT // CTX_LEN + b, pcol)),
                  pl.BlockSpec((len(POOL_WINDOWS), POOL_CH, POOL_CH), lambda b: (0, 0, 0)),
                  pl.BlockSpec((1, POOL_WIDTH), lambda b: (0, 0))],
        out_specs=(pl.BlockSpec((SEQ, POOL_WIDTH), lambda b: (b, 0)),
                   pl.BlockSpec((CTX_LEN, POOL_WIDTH), lambda b: (b, 0))),
        compiler_params=pltpu.CompilerParams(vmem_limit_bytes=VMEM_LIMIT),
        name="pool",
    )(qkvgp, qkvgp, pool_w, pool_scale)


def _mix_out(i, rl, rc, pl_, pc, w_ref, xl_ref, xc_ref, gt_ref):
    is_lat = i < LAT_TILES
    ret = jnp.where(is_lat, rl[...], rc[...])
    pool = jnp.where(is_lat, pl_[...], pc[...])
    y = (jnp.dot(ret, w_ref[0:RET_WIDTH, :], preferred_element_type=F32)
         + jnp.dot(pool, w_ref[RET_WIDTH:, :], preferred_element_type=F32))
    return _x_tile(xl_ref, xc_ref) + gt_ref[...] * y


def _outproj_router_kernel(rl, rc, pl_, pc, w_ref, xl_ref, xc_ref, gt_ref, g_ref, sh_ref, sc_ref, wr_ref,
                           xo_ref, h_ref, route_ref, cnt_ref, carry_scr):
    i = pl.program_id(0)
    x = _mix_out(i, rl, rc, pl_, pc, w_ref, xl_ref, xc_ref, gt_ref)
    xo_ref[...] = x
    h = _rms_mod(x, g_ref[...], sh_ref[...], sc_ref[...])
    h_ref[...] = h.reshape(TM, *ROW_TILE)
    logits = jnp.dot(h.astype(BF16), wr_ref[...], preferred_element_type=F32)
    lane = lax.broadcasted_iota(jnp.int32, logits.shape, 1)
    neg = jnp.float32(-jnp.inf)
    logits = jnp.where(lane < N_EXPERTS, logits, neg)
    m1 = jnp.max(logits, axis=-1, keepdims=True)
    i1 = jnp.min(jnp.where(logits == m1, lane, LANES), axis=-1, keepdims=True)
    rest = jnp.where(lane == i1, neg, logits)
    m2 = jnp.max(rest, axis=-1, keepdims=True)
    i2 = jnp.min(jnp.where(rest == m2, lane, LANES), axis=-1, keepdims=True)
    e2 = jnp.exp(m2 - m1)
    den = 1.0 + e2

    @pl.when(i == 0)
    def _():
        carry_scr[...] = jnp.zeros_like(carry_scr)

    sel1, sel2 = lane == i1, lane == i2
    picks = jnp.where(sel1 | sel2, 1.0, 0.0)
    ti = lax.broadcasted_iota(jnp.int32, (TM, TM), 0)
    tj = lax.broadcasted_iota(jnp.int32, (TM, TM), 1)
    earlier = jnp.where(tj < ti, 1.0, 0.0).astype(BF16)
    before = jnp.dot(earlier, picks.astype(BF16), preferred_element_type=F32) + carry_scr[...]
    r1 = jnp.sum(jnp.where(sel1, before, 0.0), axis=-1, keepdims=True)
    r2 = jnp.sum(jnp.where(sel2, before, 0.0), axis=-1, keepdims=True)
    carry = carry_scr[...] + jnp.sum(picks, axis=0, keepdims=True)
    carry_scr[...] = carry
    cnt_ref[...] = jnp.broadcast_to(carry, cnt_ref.shape)

    fields = zip(ROUTE_EXPERT + ROUTE_RANK + ROUTE_GATE, (i1.astype(F32), i2.astype(F32), r1, r2, 1.0 / den, e2 / den))
    route = jnp.zeros(logits.shape, F32)
    for k, v in fields:
        route = jnp.where(lane == k, v, route)
    route_ref[...] = route


def _mixer_specs(ctx_off, layer):
    def lat_spec():
        return pl.BlockSpec((TM, RET_WIDTH), lambda i: (jnp.minimum(i, LAT_TILES - 1), 0))

    def ctx_spec():
        return pl.BlockSpec((TM, RET_WIDTH), lambda i: (jnp.maximum(i - LAT_TILES, 0), 0))

    return [lat_spec(), ctx_spec(), lat_spec(), ctx_spec(),
            pl.BlockSpec((D_MODEL, D_MODEL), lambda i: (0, 0), pipeline_mode=pl.Buffered(1)),
            *_x_specs(ctx_off), _mod_spec(layer, 2),
            pl.BlockSpec((1, D_MODEL), lambda i: (0, 0)),
            _mod_spec(layer, 3), _mod_spec(layer, 4)]


def _outproj_router(ret_l, ret_c, pool_l, pool_c, w_out, x_lat, x_ctx, ctx_off, mod5, norm_g, router_w, layer,
                    n_tiles):
    row = pl.BlockSpec((TM, D_MODEL), lambda i: (i, 0))
    n_rows = n_tiles * TM
    return pl.pallas_call(
        _outproj_router_kernel,
        out_shape=(jax.ShapeDtypeStruct((n_rows, D_MODEL), F32), jax.ShapeDtypeStruct((n_rows, *ROW_TILE), F32),
                   jax.ShapeDtypeStruct((n_rows, LANES), F32), jax.ShapeDtypeStruct((SUBLANES, LANES), F32)),
        grid=(n_tiles,),
        in_specs=_mixer_specs(ctx_off, layer) + [pl.BlockSpec((D_MODEL, LANES), lambda i: (0, 0))],
        out_specs=(row, pl.BlockSpec((TM, *ROW_TILE), lambda i: (i, 0, 0)),
                   pl.BlockSpec((TM, LANES), lambda i: (i, 0)), pl.BlockSpec((SUBLANES, LANES), lambda i: (0, 0))),
        scratch_shapes=[pltpu.VMEM((1, LANES), F32)],
        compiler_params=pltpu.CompilerParams(dimension_semantics=("arbitrary",), vmem_limit_bytes=VMEM_LIMIT),
        name="outproj",
    )(ret_l, ret_c, pool_l, pool_c, w_out, x_lat, x_ctx, mod5, norm_g, mod5, mod5, router_w)


def _swiglu(h, w13_ref, w2_ref, act_scr):
    for j in range(D_FF // FF_CHUNK):
        u = jnp.dot(h, w13_ref[0, :, j * FF_CHUNK:(j + 1) * FF_CHUNK], preferred_element_type=F32)
        g = jnp.dot(h, w13_ref[0, :, D_FF + j * FF_CHUNK:D_FF + (j + 1) * FF_CHUNK], preferred_element_type=F32)
        act_scr[:, j * FF_CHUNK:(j + 1) * FF_CHUNK] = (_silu(g) * u).astype(BF16)
    return jnp.dot(act_scr[...], w2_ref[0], preferred_element_type=F32)


def _outproj_ffn_kernel(rl, rc, pl_, pc, w_ref, xl_ref, xc_ref, gt1_ref, g_ref, sh_ref, sc_ref, gt2_ref,
                        w13_ref, w2_ref, wf_ref, o_ref, wb_ref, act_scr):
    x = _mix_out(pl.program_id(0), rl, rc, pl_, pc, w_ref, xl_ref, xc_ref, gt1_ref)
    h = _rms_mod(x, g_ref[...], sh_ref[...], sc_ref[...]).astype(BF16)
    o_ref[...] = x + gt2_ref[...] * _swiglu(h, w13_ref, w2_ref, act_scr)
    wb_ref[...] = wf_ref[...].astype(BF16)


def _outproj_ffn(ret_l, ret_c, pool_l, pool_c, w_out, x_lat, x_ctx, ctx_off, mod5, norm_g, w13, w2, cast, layer,
                 n_tiles):
    resident = pl.Buffered(1)
    stacked, index = cast
    rows, cols = stacked.shape[0] // (DEPTH // 2), stacked.shape[1]
    src_spec, dst_spec = _cast_specs(rows, cols, index)
    return pl.pallas_call(
        _outproj_ffn_kernel,
        out_shape=(jax.ShapeDtypeStruct((n_tiles * TM, D_MODEL), F32), jax.ShapeDtypeStruct((rows, cols), BF16)),
        grid=(n_tiles,),
        in_specs=_mixer_specs(ctx_off, layer) + [
            _mod_spec(layer, 5),
            pl.BlockSpec((1, D_MODEL, 2 * D_FF), lambda i: (0, 0, 0), pipeline_mode=resident),
            pl.BlockSpec((1, D_FF, D_MODEL), lambda i: (0, 0, 0), pipeline_mode=resident),
            src_spec],
        out_specs=(pl.BlockSpec((TM, D_MODEL), lambda i: (i, 0)), dst_spec),
        scratch_shapes=[pltpu.VMEM((TM, D_FF), BF16)],
        compiler_params=pltpu.CompilerParams(vmem_limit_bytes=VMEM_LIMIT),
        name="outproj_ffn",
    )(ret_l, ret_c, pool_l, pool_c, w_out, x_lat, x_ctx, mod5, norm_g, mod5, mod5, mod5, w13, w2, stacked)


def _moe_plan(counts, n_steps):
    cnt = counts.astype(jnp.int32)
    end = jnp.cumsum(cnt)
    start = end - cnt
    first = start // TM
    visits = jnp.where(cnt > 0, (end - 1) // TM - first + 1, 0)
    visit_end = jnp.cumsum(visits)
    visit_start = visit_end - visits
    total = visit_end[-1]
    step = jnp.minimum(jnp.arange(n_steps, dtype=jnp.int32), total - 1)
    eid = jnp.minimum(jnp.sum(step[:, None] >= visit_end[None, :], axis=1), N_EXPERTS - 1).astype(jnp.int32)
    tile = first[eid] + step - visit_start[eid]
    lo = jnp.clip(start[eid] - tile * TM, 0, TM)
    hi = jnp.clip(end[eid] - tile * TM, 0, TM)
    hi = jnp.where(jnp.arange(n_steps) < total, hi, lo)
    return start, tile.astype(jnp.int32), eid, lo.astype(jnp.int32), hi.astype(jnp.int32)


def _row_copy_wait(src, dst, sem):
    pltpu.make_async_copy(src, dst, sem).wait()


def _dispatch_kernel(pos_ref, h_ref, xs_hbm, sem):
    i = pl.program_id(0)
    n = 0
    for r in range(TM):
        for k in range(2):
            p = pos_ref[i, k * TM + r]
            pltpu.make_async_copy(h_ref.at[r], xs_hbm.at[p], sem).start(priority=n % 2)
            n += 1
    for k in range(2):
        _row_copy_wait(h_ref, xs_hbm.at[pl.ds(0, TM)], sem)


def _dispatch(pos, h, n_tiles):
    return pl.pallas_call(
        _dispatch_kernel,
        out_shape=jax.ShapeDtypeStruct((2 * n_tiles * TM, *ROW_TILE), F32),
        grid_spec=pltpu.PrefetchScalarGridSpec(
            num_scalar_prefetch=1,
            grid=(n_tiles,),
            in_specs=[pl.BlockSpec((TM, *ROW_TILE), lambda i, pos: (i, 0, 0))],
            out_specs=pl.BlockSpec(memory_space=pl.ANY),
            scratch_shapes=[pltpu.SemaphoreType.DMA]),
        name="dispatch",
    )(pos, h)


def _expert_ffn_kernel(tile_ref, eid_ref, lo_ref, hi_ref, x_ref, w13_ref, w2_ref, o_ref, act_scr):
    s = pl.program_id(0)
    lo, hi = lo_ref[s], hi_ref[s]

    @pl.when(hi > lo)
    def _():
        row = lax.broadcasted_iota(jnp.int32, (TM, D_MODEL), 0)
        x = x_ref[...].reshape(TM, D_MODEL)
        h = jnp.where((row >= lo) & (row < hi), x, 0.0).astype(BF16)
        y = _swiglu(h, w13_ref, w2_ref, act_scr).reshape(TM, *ROW_TILE)

        @pl.when(lo == 0)
        def _():
            o_ref[...] = y

        @pl.when(lo > 0)
        def _():
            o_ref[...] += y


def _expert_ffn(plan, xs, w13, w2, n_steps):
    _, tile, eid, lo, hi = plan
    row = pl.BlockSpec((TM, *ROW_TILE), lambda s, tile, eid, lo, hi: (tile[s], 0, 0))
    return pl.pallas_call(
        _expert_ffn_kernel,
        out_shape=jax.ShapeDtypeStruct(xs.shape, F32),
        grid_spec=pltpu.PrefetchScalarGridSpec(
            num_scalar_prefetch=4,
            grid=(n_steps,),
            in_specs=[row,
                      pl.BlockSpec((1, D_MODEL, 2 * D_FF), lambda s, tile, eid, lo, hi: (eid[s], 0, 0)),
                      pl.BlockSpec((1, D_FF, D_MODEL), lambda s, tile, eid, lo, hi: (eid[s], 0, 0))],
            out_specs=row,
            scratch_shapes=[pltpu.VMEM((TM, D_FF), BF16)]),
        compiler_params=pltpu.CompilerParams(vmem_limit_bytes=VMEM_LIMIT),
        name="expert_ffn",
    )(tile, eid, lo, hi, xs, w13, w2)


def _combine_kernel(pos_ref, y_hbm, route_ref, x_ref, gt_ref, g_ref, o_ref, ybuf, sem, *, n_tiles, final_norm):
    i = pl.program_id(0)
    slot = i % 2

    def request(tile, into):
        n = 0
        for r in range(TM):
            for k in range(2):
                p = pos_ref[tile, k * TM + r]
                pltpu.make_async_copy(y_hbm.at[p], ybuf.at[into, k, r], sem.at[into]).start(priority=n % 2)
                n += 1

    @pl.when(i == 0)
    def _():
        request(0, 0)

    @pl.when(i + 1 < n_tiles)
    def _():
        request(i + 1, 1 - slot)

    for k in range(2):
        _row_copy_wait(y_hbm.at[pl.ds(0, TM)], ybuf.at[slot, k], sem.at[slot])
    w1 = route_ref[:, ROUTE_GATE[0]:ROUTE_GATE[0] + 1]
    w2 = route_ref[:, ROUTE_GATE[1]:ROUTE_GATE[1] + 1]
    y1 = ybuf[slot, 0].reshape(TM, D_MODEL)
    y2 = ybuf[slot, 1].reshape(TM, D_MODEL)
    x = x_ref[...] + gt_ref[...] * (w1 * y1 + w2 * y2)
    if final_norm:
        ms = jnp.mean(x * x, axis=-1, keepdims=True)
        x = x * lax.rsqrt(ms + NORM_EPS) * g_ref[...]
    o_ref[...] = x


def _combine(pos, y, route, x_all, mod5, final_g, layer, n_tiles, final_norm):
    row = pl.BlockSpec((TM, D_MODEL), lambda i, pos: (i, 0))
    return pl.pallas_call(
        functools.partial(_combine_kernel, n_tiles=n_tiles, final_norm=final_norm),
        out_shape=jax.ShapeDtypeStruct((n_tiles * TM, D_MODEL), F32),
        grid_spec=pltpu.PrefetchScalarGridSpec(
            num_scalar_prefetch=1,
            grid=(n_tiles,),
            in_specs=[pl.BlockSpec(memory_space=pl.ANY),
                      pl.BlockSpec((TM, LANES), lambda i, pos: (i, 0)), row, _mod_spec(layer, 5),
                      pl.BlockSpec((1, D_MODEL), lambda i, pos: (0, 0))],
            out_specs=row,
            scratch_shapes=[pltpu.VMEM((2, 2, TM, *ROW_TILE), F32), pltpu.SemaphoreType.DMA((2,))]),
        compiler_params=pltpu.CompilerParams(dimension_semantics=("arbitrary",), vmem_limit_bytes=VMEM_LIMIT),
        name="combine",
    )(pos, y, route, x_all, mod5, final_g)


def _moe(h, route, counts, x_all, mod5, w13, w2, final_g, layer, n_tiles, final_norm):
    n_steps = 2 * n_tiles + N_EXPERTS
    plan = _moe_plan(counts[0, :N_EXPERTS], n_steps)
    expert = route[:, ROUTE_EXPERT[0]:ROUTE_EXPERT[1] + 1].astype(jnp.int32)
    rank = route[:, ROUTE_RANK[0]:ROUTE_RANK[1] + 1].astype(jnp.int32)
    run_start = jnp.sum(jnp.where(expert[..., None] == jnp.arange(N_EXPERTS), plan[0], 0), axis=-1)
    pos = run_start + rank
    pos = pos.reshape(n_tiles, TM, 2).transpose(0, 2, 1).reshape(n_tiles, 2 * TM)
    xs = _dispatch(pos, h, n_tiles)
    ys = _expert_ffn(plan, xs, w13, w2, n_steps)
    return _combine(pos, ys, route, x_all, mod5, final_g, layer, n_tiles, final_norm)


def kernel(x, c, ctx, c_ctx, w_ada, b_ada, norm1_g, norm2_g, w_in, ret_decay_logit, ret_gn_g, pool_w, pool_scale,
           w_out, ffn_w13, ffn_w2, router_w, moe_w13, moe_w2, final_norm_g):
    assert DEPTH % 2 == 0, "the final norm is fused into the last routed-expert combine"
    c_all = jnp.concatenate([c, c_ctx[None, :], jnp.zeros((MOD_ROWS - BATCH - 1, D_MODEL), F32)], axis=0)
    mod5 = _ada_mod(c_all, w_ada, b_ada).reshape(DEPTH, MOD_ROWS, 6, 1, D_MODEL)
    tables = _rope_tables()
    log_gamma = jax.nn.log_sigmoid(ret_decay_logit.astype(F32))
    router_pad = jnp.pad(router_w, ((0, 0), (0, 0), (0, LANES - N_EXPERTS))).astype(BF16)
    moe_w13_rows = moe_w13.reshape(-1, 2 * D_FF)
    moe_w2_rows = moe_w2.reshape(-1, D_MODEL)
    final_g = final_norm_g[None, :]

    x_lat, x_ctx, ctx_off = x.reshape(N_LAT, D_MODEL), ctx.reshape(N_CTX, D_MODEL), 0
    for l in range(DEPTH):
        last = l == DEPTH - 1
        n_tiles = LAT_TILES if last else LAT_TILES + CTX_TILES
        i = l // 2
        if l % 2 == 0:
            qkvgp, w2_b = _inproj(x_lat, x_ctx, ctx_off, norm1_g[l][None, :], mod5, w_in[l].astype(BF16), tables, l,
                                  LAT_TILES + CTX_TILES, cast=(moe_w2_rows, i))
        else:
            qkvgp = _inproj(x_lat, x_ctx, ctx_off, norm1_g[l][None, :], mod5, w_in[l].astype(BF16), tables, l,
                            LAT_TILES + CTX_TILES)
        ret_l, ret_c = _retention(qkvgp, log_gamma[l], ret_gn_g[l][None, :])
        pool_l, pool_c = _pool(qkvgp, pool_w[l].astype(BF16), pool_scale[l][None, :])
        if l % 2 == 0:
            x_all, w13_b = _outproj_ffn(ret_l, ret_c, pool_l, pool_c, w_out[l].astype(BF16), x_lat, x_ctx, ctx_off,
                                        mod5, norm2_g[l][None, :], ffn_w13[i][None].astype(BF16),
                                        ffn_w2[i][None].astype(BF16), (moe_w13_rows, i), l, n_tiles)
        else:
            x_all, h2, route, counts = _outproj_router(ret_l, ret_c, pool_l, pool_c, w_out[l].astype(BF16), x_lat,
                                                       x_ctx, ctx_off, mod5, norm2_g[l][None, :], router_pad[i],
                                                       l, n_tiles)
            x_all = _moe(h2, route, counts, x_all, mod5, w13_b.reshape(N_EXPERTS, D_MODEL, 2 * D_FF),
                         w2_b.reshape(N_EXPERTS, D_FF, D_MODEL), final_g, l, n_tiles, last)
        x_lat, x_ctx, ctx_off = x_all, x_all, LAT_TILES
    return x_all.reshape(BATCH, SEQ, D_MODEL)
```
